```python
import jax, jax.numpy as jnp
from jax import lax
import numpy as np


D_MODEL = 1024
BATCH = 16
SEQ = 256
DEPTH = 1
DEC_BATCH = 8
DEC_SEQ = 1024
PAST_LEN = 512

GRID_W = 64
CHUNK = 128
N_DIR = 2
H_RET = 4
DK_RET = 128
DV_RET = 256
H_M = 4
DH_M = 128
CONV_W = 3
D_FF = 2816
ROPE_BASE = 10000.0
EPS = 1e-6
GN_EPS = 1e-5
RET_QK = H_RET * DK_RET
RET_V = H_RET * DV_RET
M_W = H_M * DH_M
D_IN = 2 * RET_QK + 2 * RET_V + 4 * M_W + 2 * N_DIR * H_M + N_DIR * D_MODEL

kernel_name = 'bidir_retention_mlstm_diffusion_step'


def _rmsnorm(x, g):
    xf = x.astype(jnp.float32)
    y = xf * lax.rsqrt(jnp.mean(xf * xf, axis=-1, keepdims=True) + EPS)
    return (y * g.astype(jnp.float32)).astype(x.dtype)


def _modulate(u, shift, scale):
    return u * (1 + scale) + shift


def _swiglu(u, w1, w3, w2):
    return (jax.nn.silu(u @ w1) * (u @ w3)) @ w2


def _head_norm(h, g):
    mu = jnp.mean(h, axis=-1, keepdims=True)
    hc = h - mu
    y = hc * lax.rsqrt(jnp.mean(hc * hc, axis=-1, keepdims=True) + GN_EPS)
    B, L, H, d = h.shape
    return y.reshape(B, L, H * d) * g.astype(jnp.float32)


def _short_conv(x, w, b):
    L = x.shape[1]
    pad = CONV_W // 2
    xp = jnp.pad(x, ((0, 0), (pad, CONV_W - 1 - pad), (0, 0)))
    y = b
    for j in range(CONV_W):
        y = y + w[j] * xp[:, j:j + L]
    return y


def _axial_rope(L):
    rows = L // GRID_W
    r = jnp.repeat(jnp.arange(rows, dtype=jnp.float32), GRID_W)
    col = (jnp.arange(L) % GRID_W).astype(jnp.float32)
    n_f = DK_RET // 4
    freqs = ROPE_BASE ** (-jnp.arange(n_f, dtype=jnp.float32) / n_f)
    ang = jnp.concatenate([r[:, None] * freqs, col[:, None] * freqs], axis=-1)
    return jnp.cos(ang), jnp.sin(ang)


def _apply_rope(x, cos, sin):
    half = x.shape[-1] // 2
    x1, x2 = x[..., :half], x[..., half:]
    c = cos[None, :, None, :]
    s = sin[None, :, None, :]
    return jnp.concatenate([x1 * c - x2 * s, x2 * c + x1 * s], axis=-1)


def _split_cols(proj):
    sizes = [RET_QK, RET_QK, RET_V, RET_V, M_W, M_W, M_W, M_W, N_DIR * H_M, N_DIR * H_M, N_DIR * D_MODEL]
    offs = [int(o) for o in np.cumsum(sizes)[:-1]]
    return jnp.split(proj, offs, axis=-1)


def _flip(a, d):
    return jnp.flip(a, axis=1) if d == 1 else a


def _retention_dir(q, k, v, lg, s0):
    B, L, H, dk = q.shape
    dv = v.shape[-1]
    N = L // CHUNK
    qc = q.reshape(B, N, CHUNK, H, dk)
    kc = k.reshape(B, N, CHUNK, H, dk)
    vc = v.reshape(B, N, CHUNK, H, dv)
    pos = jnp.arange(CHUNK, dtype=jnp.float32)
    diff = pos[:, None] - pos[None, :]
    mask = diff >= 0
    e = diff[None] * lg[:, None, None]
    dmat = jnp.where(mask, jnp.exp(jnp.where(mask, e, 0.0)), 0.0)
    scores = jnp.einsum('bnihd,bnjhd->bnhij', qc, kc) * dmat
    intra = jnp.einsum('bnhij,bnjhe->bnihe', scores, vc)
    wk = jnp.exp((CHUNK - 1 - pos)[:, None] * lg[None, :])
    kv = jnp.einsum('bnjhd,jh,bnjhe->bnhde', kc, wk, vc)
    g_chunk = jnp.exp(CHUNK * lg)[:, None, None]

    def step(s, kv_n):
        return g_chunk * s + kv_n, s

    s_fin, s_prev = lax.scan(step, s0, jnp.moveaxis(kv, 1, 0))
    s_prev = jnp.moveaxis(s_prev, 0, 1)
    wq = jnp.exp((pos + 1)[:, None] * lg[None, :])
    cross = jnp.einsum('bnihd,ih,bnhde->bnihe', qc, wq, s_prev)
    return (intra + cross).reshape(B, L, H, dv), s_fin


def _mlstm_dir(q, k, v, ig, lf, c0, n0, m0):
    B, L, H, d = q.shape
    N = L // CHUNK
    qc = q.reshape(B, N, CHUNK, H, d)
    kc = k.reshape(B, N, CHUNK, H, d)
    vc = v.reshape(B, N, CHUNK, H, d)
    igc = jnp.swapaxes(ig.reshape(B, N, CHUNK, H), 2, 3)
    b = jnp.cumsum(jnp.swapaxes(lf.reshape(B, N, CHUNK, H), 2, 3), axis=-1)
    b_last = b[..., -1]
    pos = jnp.arange(CHUNK)
    mask = pos[:, None] >= pos[None, :]
    dlog = jnp.where(mask, b[..., :, None] - b[..., None, :] + igc[..., None, :], -jnp.inf)
    g = b_last[..., None] - b + igc
    m_loc = jnp.max(g, axis=-1)
    w = jnp.exp(g - m_loc[..., None])
    c_loc = jnp.einsum('bnhj,bnjhv,bnjhk->bnhvk', w, vc, kc)
    n_loc = jnp.einsum('bnhj,bnjhk->bnhk', w, kc)

    def step(carry, xs):
        cs, ns, ms = carry
        bl, ml, cl, nl = xs
        m_new = jnp.maximum(bl + ms, ml)
        a1 = jnp.exp(bl + ms - m_new)
        a2 = jnp.exp(ml - m_new)
        c_new = a1[..., None, None] * cs + a2[..., None, None] * cl
        n_new = a1[..., None] * ns + a2[..., None] * nl
        return (c_new, n_new, m_new), (cs, ns, ms)

    xs = (jnp.moveaxis(b_last, 1, 0), jnp.moveaxis(m_loc, 1, 0), jnp.moveaxis(c_loc, 1, 0), jnp.moveaxis(n_loc, 1, 0))
    fin, prev = lax.scan(step, (c0, n0, m0), xs)
    c_prev, n_prev, m_prev = (jnp.moveaxis(a, 0, 1) for a in prev)
    lin = b + m_prev[..., None]
    m_i = jnp.maximum(jnp.max(dlog, axis=-1), lin)
    s = jnp.einsum('bnihk,bnjhk->bnhij', qc, kc) * jnp.exp(dlog - m_i[..., None])
    w_int = jnp.exp(lin - m_i)
    num = jnp.einsum('bnhij,bnjhv->bnihv', s, vc) + jnp.einsum('bnihk,bnhvk->bnihv', qc, c_prev) * jnp.swapaxes(w_int, 2, 3)[..., None]
    den = jnp.sum(s, axis=-1) + jnp.einsum('bnihk,bnhk->bnhi', qc, n_prev) * w_int
    denom = jnp.maximum(jnp.abs(den), jnp.exp(-m_i))
    h = num / jnp.swapaxes(denom, 2, 3)[..., None]
    return h.reshape(B, L, H, d), fin


def _mixer(u, p, s0, rope):
    f32 = jnp.float32
    B, L, _ = u.shape
    dt = u.dtype
    rq, rk, rv, rg, mq, mk, mv, mo, mi, mf, bg = _split_cols(u @ p['w_in'])
    ret_s0, c0, n0, m0 = s0
    rq = rq.astype(f32).reshape(B, L, H_RET, DK_RET)
    rk = rk.astype(f32).reshape(B, L, H_RET, DK_RET) * DK_RET ** -0.5
    rv = rv.astype(f32).reshape(B, L, H_RET, DV_RET)
    if rope is not None:
        rq = _apply_rope(rq, rope[0], rope[1])
        rk = _apply_rope(rk, rope[0], rope[1])
    lg = jax.nn.log_sigmoid(p['ret_decay_logit'].astype(f32))
    r_outs, r_fins = [], []
    for d in range(N_DIR):
        o, sf = _retention_dir(_flip(rq, d), _flip(rk, d), _flip(rv, d), lg[d], ret_s0[:, d])
        r_outs.append(_flip(o, d))
        r_fins.append(sf)
    r = _head_norm(r_outs[0] + r_outs[1], p['ret_gn']) * jax.nn.silu(rg.astype(f32))
    ret_branch = r.astype(dt) @ p['w_ret_up']
    qk = jax.nn.silu(_short_conv(jnp.concatenate([mq, mk], axis=-1), p['conv_w'], p['conv_b']))
    mq = qk[..., :M_W].astype(f32).reshape(B, L, H_M, DH_M)
    mk = qk[..., M_W:].astype(f32).reshape(B, L, H_M, DH_M) * DH_M ** -0.5
    mv = mv.astype(f32).reshape(B, L, H_M, DH_M)
    ig = (mi + p['b_igate']).astype(f32).reshape(B, L, N_DIR, H_M)
    lf = jax.nn.log_sigmoid((mf + p['b_fgate']).astype(f32)).reshape(B, L, N_DIR, H_M)
    m_outs, c_fins, n_fins, m_fins = [], [], [], []
    for d in range(N_DIR):
        h, (cf, nf, mfin) = _mlstm_dir(_flip(mq, d), _flip(mk, d), _flip(mv, d), _flip(ig[:, :, d], d),
                                       _flip(lf[:, :, d], d), c0[:, d], n0[:, d], m0[:, d])
        m_outs.append(_flip(h, d))
        c_fins.append(cf)
        n_fins.append(nf)
        m_fins.append(mfin)
    hm = jax.nn.sigmoid(mo.astype(f32)).reshape(B, L, H_M, DH_M) * (m_outs[0] + m_outs[1])
    m_branch = _head_norm(hm, p['m_gn']).astype(dt) @ p['w_m_up']
    gates = jax.nn.sigmoid(bg.astype(f32)).reshape(B, L, N_DIR, D_MODEL)
    merged = gates[:, :, 0] * ret_branch.astype(f32) + gates[:, :, 1] * m_branch.astype(f32)
    fin = (jnp.stack(r_fins, axis=1), jnp.stack(c_fins, axis=1), jnp.stack(n_fins, axis=1), jnp.stack(m_fins, axis=1))
    return merged.astype(dt) @ p['w_out'], fin


def _layer(x, mod, p, s0, rope):
    sh1, sc1, g1, sh2, sc2, g2, sh3, sc3, g3 = jnp.split(mod, 9, axis=-1)
    x = x + 0.5 * g1 * _swiglu(_modulate(_rmsnorm(x, p['norm_ffn1']), sh1, sc1), p['w1_ffn1'], p['w3_ffn1'], p['w2_ffn1'])
    y, fin = _mixer(_modulate(_rmsnorm(x, p['norm_mix']), sh2, sc2), p, s0, rope)
    x = x + g2 * y
    x = x + 0.5 * g3 * _swiglu(_modulate(_rmsnorm(x, p['norm_ffn2']), sh3, sc3), p['w1_ffn2'], p['w3_ffn2'], p['w2_ffn2'])
    return x, fin


def setup_inputs(seed: int = 0) -> dict:
    key = jax.random.key(seed)
    ks = iter(jax.random.split(key, 40))
    f32 = jnp.float32

    def nrm(shape, scale):
        return scale * jax.random.normal(next(ks), shape, f32)

    gam = 1.0 - 2.0 ** (-5.0 - jnp.arange(H_RET, dtype=f32))
    ret_logit0 = jnp.log(gam) - jnp.log1p(-gam)
    fb = jnp.linspace(3.0, 6.0, H_M, dtype=f32)
    fbias0 = jnp.concatenate([fb, fb])
    return {
        'x_prompt': nrm((BATCH, SEQ, D_MODEL), 1.0),
        'x_sample': nrm((DEC_BATCH, DEC_SEQ, D_MODEL), 1.0),
        'c': nrm((DEC_BATCH, D_MODEL), 1.0),
        'state_ret': nrm((DEC_BATCH, DEPTH, N_DIR, H_RET, DK_RET, DV_RET), 0.3),
        'state_mlstm_C': nrm((DEC_BATCH, DEPTH, N_DIR, H_M, DH_M, DH_M), 0.3),
        'state_mlstm_n': nrm((DEC_BATCH, DEPTH, N_DIR, H_M, DH_M), 0.3),
        'state_mlstm_m': nrm((DEC_BATCH, DEPTH, N_DIR, H_M), 0.5),
        'c_ctx': nrm((D_MODEL,), 1.0),
        'w_ada': nrm((DEPTH, D_MODEL, 9 * D_MODEL), D_MODEL ** -0.5),
        'b_ada': nrm((DEPTH, 9 * D_MODEL), 0.02),
        'norm_ffn1': 1.0 + nrm((DEPTH, D_MODEL), 0.02),
        'w1_ffn1': nrm((DEPTH, D_MODEL, D_FF), D_MODEL ** -0.5),
        'w3_ffn1': nrm((DEPTH, D_MODEL, D_FF), D_MODEL ** -0.5),
        'w2_ffn1': nrm((DEPTH, D_FF, D_MODEL), D_FF ** -0.5),
        'norm_mix': 1.0 + nrm((DEPTH, D_MODEL), 0.02),
        'w_in': nrm((DEPTH, D_MODEL, D_IN), D_MODEL ** -0.5),
        'conv_w': nrm((DEPTH, CONV_W, 2 * M_W), CONV_W ** -0.5),
        'conv_b': nrm((DEPTH, 2 * M_W), 0.02),
        'ret_decay_logit': ret_logit0[None, None, :] + nrm((DEPTH, N_DIR, H_RET), 0.01),
        'b_igate': nrm((DEPTH, N_DIR * H_M), 0.1),
        'b_fgate': fbias0[None, :] + nrm((DEPTH, N_DIR * H_M), 0.1),
        'ret_gn': 1.0 + nrm((DEPTH, RET_V), 0.02),
        'm_gn': 1.0 + nrm((DEPTH, M_W), 0.02),
        'w_ret_up': nrm((DEPTH, RET_V, D_MODEL), RET_V ** -0.5),
        'w_m_up': nrm((DEPTH, M_W, D_MODEL), M_W ** -0.5),
        'w_out': nrm((DEPTH, D_MODEL, D_MODEL), D_MODEL ** -0.5),
        'norm_ffn2': 1.0 + nrm((DEPTH, D_MODEL), 0.02),
        'w1_ffn2': nrm((DEPTH, D_MODEL, D_FF), D_MODEL ** -0.5),
        'w3_ffn2': nrm((DEPTH, D_MODEL, D_FF), D_MODEL ** -0.5),
        'w2_ffn2': nrm((DEPTH, D_FF, D_MODEL), D_FF ** -0.5),
        'norm_final': 1.0 + nrm((D_MODEL,), 0.02),
    }


def reference(x_prompt, x_sample, c, state_ret, state_mlstm_C, state_mlstm_n, state_mlstm_m, c_ctx,
              w_ada, b_ada, norm_ffn1, w1_ffn1, w3_ffn1, w2_ffn1, norm_mix, w_in, conv_w, conv_b,
              ret_decay_logit, b_igate, b_fgate, ret_gn, m_gn, w_ret_up, w_m_up, w_out,
              norm_ffn2, w1_ffn2, w3_ffn2, w2_ffn2, norm_final):
    f32 = jnp.float32
    bp = x_prompt.shape[0]
    rope = _axial_rope(x_sample.shape[1])
    zero_state = (jnp.zeros((bp, N_DIR, H_RET, DK_RET, DV_RET), f32),
                  jnp.zeros((bp, N_DIR, H_M, DH_M, DH_M), f32),
                  jnp.zeros((bp, N_DIR, H_M, DH_M), f32),
                  jnp.zeros((bp, N_DIR, H_M), f32))
    xp, xs = x_prompt, x_sample
    fins = []
    for l in range(DEPTH):
        p = {'norm_ffn1': norm_ffn1[l], 'w1_ffn1': w1_ffn1[l], 'w3_ffn1': w3_ffn1[l], 'w2_ffn1': w2_ffn1[l],
             'norm_mix': norm_mix[l], 'w_in': w_in[l], 'conv_w': conv_w[l], 'conv_b': conv_b[l],
             'ret_decay_logit': ret_decay_logit[l], 'b_igate': b_igate[l], 'b_fgate': b_fgate[l],
             'ret_gn': ret_gn[l], 'm_gn': m_gn[l], 'w_ret_up': w_ret_up[l], 'w_m_up': w_m_up[l], 'w_out': w_out[l],
             'norm_ffn2': norm_ffn2[l], 'w1_ffn2': w1_ffn2[l], 'w3_ffn2': w3_ffn2[l], 'w2_ffn2': w2_ffn2[l]}
        mod_ctx = (jax.nn.silu(c_ctx) @ w_ada[l] + b_ada[l])[None, None, :]
        mod_lat = (jax.nn.silu(c) @ w_ada[l] + b_ada[l])[:, None, :]
        xp, fin = _layer(xp, mod_ctx, p, zero_state, None)
        lat_state = (state_ret[:, l].astype(f32), state_mlstm_C[:, l].astype(f32),
                     state_mlstm_n[:, l].astype(f32), state_mlstm_m[:, l].astype(f32))
        xs, _ = _layer(xs, mod_lat, p, lat_state, rope)
        fins.append(fin)
    dt = x_prompt.dtype
    y_prompt = _rmsnorm(xp, norm_final)
    y_sample = _rmsnorm(xs, norm_final)
    new_state_ret = jnp.stack([f[0] for f in fins], axis=1).astype(dt)
    new_state_mlstm_C = jnp.stack([f[1] for f in fins], axis=1).astype(dt)
    new_state_mlstm_n = jnp.stack([f[2] for f in fins], axis=1).astype(dt)
    new_state_mlstm_m = jnp.stack([f[3] for f in fins], axis=1).astype(dt)
    return (y_prompt, y_sample, new_state_ret, new_state_mlstm_C, new_state_mlstm_n, new_state_mlstm_m)
```

```python
import functools

import jax
import jax.numpy as jnp
from jax import lax
from jax.experimental import pallas as pl
from jax.experimental.pallas import tpu as pltpu

F32 = jnp.float32
BF16 = jnp.bfloat16

D_MODEL = 1024
D_FF = 2816
CHUNK = 128
N_DIR = 2
H_RET = 4
DK_RET = 128
DV_RET = 256
H_M = 4
DH_M = 128
RET_QK = H_RET * DK_RET
RET_V = H_RET * DV_RET
M_W = H_M * DH_M
GRID_W = 64
ROPE_BASE = 10000.0
EPS = 1e-6
GN_EPS = 1e-5
N_MOD = 9
N_PAIR = N_DIR * H_M
GATE_LANES = 128

MAX_GROUP_ROWS = 1024
PROMPT_GROUP_ROWS = 512
ROW_BLOCK = 256
FFN_TILE = 512
VMEM_LIMIT = 60 * 1024 * 1024


def _sigmoid(x):
    return 1.0 / (1.0 + jnp.exp(-x))


def _silu(x):
    return x * _sigmoid(x)


def _log_sigmoid(x):
    return -(jnp.maximum(-x, 0.0) + jnp.log1p(jnp.exp(-jnp.abs(x))))


def _norm_mod(x, gain, shift, scale):
    ms = jnp.mean(x * x, axis=-1, keepdims=True)
    y = x * lax.rsqrt(ms + EPS) * gain
    return y * (1.0 + scale) + shift


def _dot(a, b):
    return jnp.dot(a, b, preferred_element_type=F32)


def _dot_nt(a, b):
    return lax.dot_general(a, b, (((1,), (1,)), ((), ())), preferred_element_type=F32)


def _head_norm(o, gain):
    mu = jnp.mean(o, axis=-1, keepdims=True)
    oc = o - mu
    var = jnp.mean(oc * oc, axis=-1, keepdims=True)
    return oc * lax.rsqrt(var + GN_EPS) * gain


def _ada_kernel(c_ref, w_ref, b_ref, o_ref):
    s = _silu(c_ref[...]).astype(BF16)
    o_ref[...] = _dot(s, w_ref[...].astype(BF16)) + b_ref[...]


def _ada_call(cvec, w_ada, b_ada):
    rows = cvec.shape[0]
    n_out = w_ada.shape[1]
    tile = D_MODEL
    return pl.pallas_call(
        _ada_kernel,
        out_shape=jax.ShapeDtypeStruct((rows, n_out), F32),
        grid=(n_out // tile,),
        in_specs=[
            pl.BlockSpec((rows, D_MODEL), lambda j: (0, 0)),
            pl.BlockSpec((D_MODEL, tile), lambda j: (0, j)),
            pl.BlockSpec((1, tile), lambda j: (0, j)),
        ],
        out_specs=pl.BlockSpec((rows, tile), lambda j: (0, j)),
        compiler_params=pltpu.CompilerParams(
            dimension_semantics=("arbitrary",), vmem_limit_bytes=VMEM_LIMIT),
        name="ada",
    )(cvec, w_ada, b_ada)


def _swiglu_half_step(x, mod, k0, gain, w1_ref, w3_ref, w2_ref):
    u = _norm_mod(x, gain, mod[k0:k0 + 1], mod[k0 + 1:k0 + 2]).astype(BF16)
    h1 = _dot(u, w1_ref[...])
    h3 = _dot(u, w3_ref[...])
    h = (_silu(h1) * h3).astype(BF16)
    y = _dot(h, w2_ref[...])
    return x + 0.5 * mod[k0 + 2:k0 + 3] * y


def _ffn_kernel(x_ref, mod_ref, g_ref, w1_ref, w3_ref, w2_ref, o_ref):
    o_ref[...] = _swiglu_half_step(x_ref[...], mod_ref[0], 0, g_ref[...], w1_ref, w3_ref, w2_ref)


def _const_spec(shape):
    nd = len(shape)
    return pl.BlockSpec(shape, lambda i: (0,) * nd)


def _ffn_call(x, mod, mod_row, gain, w1, w3, w2):
    t = x.shape[0]
    tm = FFN_TILE
    return pl.pallas_call(
        _ffn_kernel,
        out_shape=jax.ShapeDtypeStruct((t, D_MODEL), F32),
        grid=(t // tm,),
        in_specs=[
            pl.BlockSpec((tm, D_MODEL), lambda i: (i, 0)),
            pl.BlockSpec((1, N_MOD, D_MODEL), lambda i: (mod_row(i * tm), 0, 0)),
            _const_spec((1, D_MODEL)),
            _const_spec((D_MODEL, D_FF)),
            _const_spec((D_MODEL, D_FF)),
            _const_spec((D_FF, D_MODEL)),
        ],
        out_specs=pl.BlockSpec((tm, D_MODEL), lambda i: (i, 0)),
        compiler_params=pltpu.CompilerParams(
            dimension_semantics=("arbitrary",), vmem_limit_bytes=VMEM_LIMIT),
        name="ffn",
    )(x, mod, gain, w1, w3, w2)


def _mixer_core_kernel(*refs, group_rows, seq_len, use_rope, has_init, emit_state):
    it = iter(refs)
    x_ref, mod_ref, nmix_ref = next(it), next(it), next(it)
    wrqk_ref, wrv_ref, wrg_ref = next(it), next(it), next(it)
    wmqk_ref, wmv_ref, wmo_ref, wgt_ref = next(it), next(it), next(it), next(it)
    convw_ref, convb_ref, gbias_ref, decay_ref = next(it), next(it), next(it), next(it)
    rgn_ref, mgn_ref = next(it), next(it)
    if use_rope:
        cos_ref, sin_ref = next(it), next(it)
    if has_init:
        s0_ref, e0_ref, m0_ref = next(it), next(it), next(it)
    rg_out, hm_out = next(it), next(it)
    if emit_state:
        sfin_ref, efin_ref, mfin_ref = next(it), next(it), next(it)
    u_s, q_s, k_s, v_s, kv_s, wide_s, igb_s, dec_s, stat_s, mprev_s = (next(it) for _ in range(10))

    rows_total = group_rows
    n_chunks = rows_total // CHUNK
    cps = seq_len // CHUNK
    n_seq = rows_total // seq_len
    n_rb = rows_total // ROW_BLOCK
    mod = mod_ref[0]

    def rb_rows(rb):
        return slice(rb * ROW_BLOCK, (rb + 1) * ROW_BLOCK)

    def head_cols(h, width):
        return slice(h * width, (h + 1) * width)

    for rb in range(n_rb):
        rows = rb_rows(rb)
        ub = _norm_mod(x_ref[rows, :], nmix_ref[...], mod[3:4], mod[4:5]).astype(BF16)
        u_s[rows, :] = ub
        qk = _dot(ub, wrqk_ref[...])
        q = qk[:, :RET_QK]
        k = qk[:, RET_QK:] * (DK_RET ** -0.5)
        for h in range(H_RET):
            cols = head_cols(h, DK_RET)
            qh, kh = q[:, cols], k[:, cols]
            if use_rope:
                cs, sn = cos_ref[rows, :], sin_ref[rows, :]
                qh = qh * cs + pltpu.roll(qh, DK_RET // 2, 1) * sn
                kh = kh * cs + pltpu.roll(kh, DK_RET // 2, 1) * sn
            q_s[rows, cols] = qh
            k_s[rows, cols] = kh
        v_s[rows, :] = _dot(ub, wrv_ref[...]).astype(BF16)

    pos_i = lax.broadcasted_iota(jnp.int32, (CHUNK, CHUNK), 0).astype(F32)
    pos_j = lax.broadcasted_iota(jnp.int32, (CHUNK, CHUNK), 1).astype(F32)
    lg_rows = _log_sigmoid(decay_ref[...])
    for h in range(H_RET):
        lgf = lg_rows[h:h + 1, :CHUNK]
        lgb = lg_rows[H_RET + h:H_RET + h + 1, :CHUNK]
        dec_s[h, 0] = jnp.where(pos_i > pos_j, jnp.exp((pos_i - pos_j) * lgf),
                                jnp.where(pos_i < pos_j, jnp.exp((pos_j - pos_i) * lgb), 2.0))
        dec_s[h, 1] = jnp.exp((CHUNK - 1.0 - pos_j) * lgf)
        dec_s[h, 2] = jnp.exp(pos_j * lgb)
        dec_s[h, 3] = jnp.exp((pos_i + 1.0) * lgf)
        dec_s[h, 4] = jnp.exp((CHUNK - pos_i) * lgb)
    g_chunk = jnp.exp(CHUNK * lg_rows)

    def ret_pass1(n, carry):
        rows = pl.ds(pl.multiple_of(n * CHUNK, CHUNK), CHUNK)
        for h in range(H_RET):
            k_t = k_s[rows, head_cols(h, DK_RET)].T
            kcat = jnp.concatenate([k_t * dec_s[h, 1], k_t * dec_s[h, 2]], axis=0).astype(BF16)
            kv = _dot(kcat, v_s[rows, head_cols(h, DV_RET)])
            kv_s[n, 2 * h] = kv[:DK_RET]
            kv_s[n, 2 * h + 1] = kv[DK_RET:]
        return carry

    lax.fori_loop(0, n_chunks, ret_pass1, 0)

    for g in range(n_seq):
        for h in range(H_RET):
            for d in range(N_DIR):
                slot = 2 * h + d
                decay = g_chunk[d * H_RET + h:d * H_RET + h + 1, :]
                state = s0_ref[0, 0, d, h] if has_init else jnp.zeros((DK_RET, DV_RET), F32)
                order = range(g * cps, (g + 1) * cps)
                for n in (order if d == 0 else reversed(order)):
                    local = kv_s[n, slot]
                    kv_s[n, slot] = state
                    state = decay * state + local
                if emit_state:
                    sfin_ref[g, 0, d, h] = state

    def ret_pass2(n, carry):
        rows = pl.ds(pl.multiple_of(n * CHUNK, CHUNK), CHUNK)
        for h in range(H_RET):
            qh = q_s[rows, head_cols(h, DK_RET)]
            kh = k_s[rows, head_cols(h, DK_RET)]
            vh = v_s[rows, head_cols(h, DV_RET)]
            scores = _dot_nt(qh.astype(BF16), kh.astype(BF16))
            sm = (scores * dec_s[h, 0]).astype(BF16)
            qcat = jnp.concatenate([qh * dec_s[h, 3], qh * dec_s[h, 4]], axis=1).astype(BF16)
            scat = jnp.concatenate([kv_s[n, 2 * h], kv_s[n, 2 * h + 1]], axis=0).astype(BF16)
            o = _dot(sm, vh) + _dot(qcat, scat)
            wide_s[rows, head_cols(h, DV_RET)] = _head_norm(o, rgn_ref[:, head_cols(h, DV_RET)])
        return carry

    lax.fori_loop(0, n_chunks, ret_pass2, 0)

    for rb in range(n_rb):
        rows = rb_rows(rb)
        rg = _dot(u_s[rows, :], wrg_ref[...])
        rg_out[rows, :] = (wide_s[rows, :] * _silu(rg)).astype(BF16)

    for rb in range(n_rb):
        rows = rb_rows(rb)
        ub = u_s[rows, :]
        wide_s[rows, :] = _dot(ub, wmqk_ref[...])
        mv = _dot(ub, wmv_ref[...])
        lane = lax.broadcasted_iota(jnp.int32, (ROW_BLOCK, DH_M), 1)
        ones_col = jnp.where(lane == 0, 1.0, 0.0).astype(BF16)
        for h in range(H_M):
            v_s[rows, h * 2 * DH_M:h * 2 * DH_M + DH_M] = mv[:, head_cols(h, DH_M)].astype(BF16)
            v_s[rows, h * 2 * DH_M + DH_M:(h + 1) * 2 * DH_M] = ones_col
        gt = _dot(ub, wgt_ref[...]) + gbias_ref[...]
        glane = lax.broadcasted_iota(jnp.int32, (ROW_BLOCK, GATE_LANES), 1)
        igb_s[rows, :] = jnp.where(glane < N_PAIR, gt, _log_sigmoid(gt))

    row_id = lax.broadcasted_iota(jnp.int32, (rows_total, CHUNK), 0)
    seq_pos = row_id & (seq_len - 1)
    for cb in range(2 * M_W // CHUNK):
        cols = slice(cb * CHUNK, (cb + 1) * CHUNK)
        xc = wide_s[:, cols]
        prev = jnp.where(seq_pos == 0, 0.0, pltpu.roll(xc, 1, 0))
        nxt = jnp.where(seq_pos == seq_len - 1, 0.0, pltpu.roll(xc, rows_total - 1, 0))
        y = convb_ref[:, cols] + convw_ref[0:1, cols] * prev
        y = y + convw_ref[1:2, cols] * xc
        y = y + convw_ref[2:3, cols] * nxt
        y = _silu(y)
        if cb < M_W // CHUNK:
            q_s[:, cols] = y
        else:
            k_s[:, slice((cb - M_W // CHUNK) * CHUNK, (cb - M_W // CHUNK + 1) * CHUNK)] = y * (DH_M ** -0.5)

    z = igb_s[...]
    cpos = row_id & (CHUNK - 1)
    glane = lax.broadcasted_iota(jnp.int32, (rows_total, GATE_LANES), 1)
    pre, suf = z, z
    step = 1
    while step < CHUNK:
        pre = pre + jnp.where(cpos >= step, pltpu.roll(pre, step, 0), 0.0)
        suf = suf + jnp.where(cpos < CHUNK - step, pltpu.roll(suf, rows_total - step, 0), 0.0)
        step *= 2
    igb_s[...] = jnp.where(glane < N_PAIR, z, jnp.where(glane < N_PAIR + H_M, pre, suf))

    def m_pass1(n, carry):
        rows = pl.ds(pl.multiple_of(n * CHUNK, CHUNK), CHUNK)
        zt = igb_s[rows, :].T
        ig_t = zt[0:N_PAIR, :]
        b_t = zt[N_PAIR:2 * N_PAIR, :]
        pair = lax.broadcasted_iota(jnp.int32, (N_PAIR, 1), 0)
        b_last = jnp.where(pair < H_M, b_t[:, CHUNK - 1:CHUNK], b_t[:, 0:1])
        g_t = (b_last - b_t) + ig_t
        m_loc = jnp.max(g_t, axis=-1, keepdims=True)
        w_t = jnp.exp(g_t - m_loc)
        stat_s[n, 0] = jnp.broadcast_to(b_last, (N_PAIR, CHUNK))
        stat_s[n, 1] = jnp.broadcast_to(m_loc, (N_PAIR, CHUNK))
        for h in range(H_M):
            k_t = k_s[rows, head_cols(h, DH_M)].T
            vext = v_s[rows, head_cols(h, 2 * DH_M)]
            for d in range(N_DIR):
                c = d * H_M + h
                kv_s[n, c] = _dot((k_t * w_t[c:c + 1, :]).astype(BF16), vext)
        return carry

    lax.fori_loop(0, n_chunks, m_pass1, 0)

    for g in range(n_seq):
        for d in range(N_DIR):
            order = list(range(g * cps, (g + 1) * cps))
            if d == 1:
                order.reverse()
            m_prev = m0_ref[0] if has_init else jnp.zeros((N_PAIR, CHUNK), F32)
            a1s, a2s = [], []
            for n in order:
                b_last, m_loc = stat_s[n, 0], stat_s[n, 1]
                m_new = jnp.maximum(b_last + m_prev, m_loc)
                a1s.append(jnp.exp(b_last + m_prev - m_new))
                a2s.append(jnp.exp(m_loc - m_new))
                mprev_s[n, d] = m_prev
                m_prev = m_new
            if emit_state:
                mfin_ref[g, d] = m_prev
            for h in range(H_M):
                c = d * H_M + h
                mem = e0_ref[0, c] if has_init else jnp.zeros((DH_M, 2 * DH_M), F32)
                for idx, n in enumerate(order):
                    a1 = jnp.concatenate([a1s[idx][c:c + 1, :]] * 2, axis=1)
                    a2 = jnp.concatenate([a2s[idx][c:c + 1, :]] * 2, axis=1)
                    local = kv_s[n, c]
                    kv_s[n, c] = mem
                    mem = a1 * mem + a2 * local
                if emit_state:
                    efin_ref[g, c] = mem

    tri_i = lax.broadcasted_iota(jnp.int32, (CHUNK, CHUNK), 0)
    tri_j = lax.broadcasted_iota(jnp.int32, (CHUNK, CHUNK), 1)

    def m_pass2(n, carry):
        rows = pl.ds(pl.multiple_of(n * CHUNK, CHUNK), CHUNK)
        zc = igb_s[rows, :]
        zt = zc.T
        for h in range(H_M):
            qh = q_s[rows, head_cols(h, DH_M)].astype(BF16)
            kh = k_s[rows, head_cols(h, DH_M)].astype(BF16)
            vext = v_s[rows, head_cols(h, 2 * DH_M)]
            scores = _dot_nt(qh, kh)
            h_sum = jnp.zeros((CHUNK, DH_M), F32)
            for d in range(N_DIR):
                c = d * H_M + h
                b_col = zc[:, N_PAIR + c:N_PAIR + c + 1]
                b_row = zt[N_PAIR + c:N_PAIR + c + 1, :]
                ig_row = zt[c:c + 1, :]
                mask = (tri_i >= tri_j) if d == 0 else (tri_j >= tri_i)
                dlog = jnp.where(mask, (b_col - b_row) + ig_row, -jnp.inf)
                m_prev = mprev_s[n, d][c:c + 1, 0:1]
                lin = b_col + m_prev
                m_i = jnp.maximum(jnp.max(dlog, axis=-1, keepdims=True), lin)
                s = scores * jnp.exp(dlog - m_i)
                w_int = jnp.exp(lin - m_i)
                intra = _dot(s.astype(BF16), vext)
                cross = _dot(qh, kv_s[n, c].astype(BF16))
                num = intra[:, :DH_M] + cross[:, :DH_M] * w_int
                den = jnp.sum(s, axis=-1, keepdims=True) + cross[:, DH_M:DH_M + 1] * w_int
                denom = jnp.maximum(jnp.abs(den), jnp.exp(-m_i))
                h_sum = h_sum + num / denom
            wide_s[rows, head_cols(h, DH_M)] = h_sum
        return carry

    lax.fori_loop(0, n_chunks, m_pass2, 0)

    for rb in range(n_rb):
        rows = rb_rows(rb)
        mo = _dot(u_s[rows, :], wmo_ref[...])
        hm = _sigmoid(mo) * wide_s[rows, :M_W]
        for h in range(H_M):
            cols = head_cols(h, DH_M)
            hm_out[rows, cols] = _head_norm(hm[:, cols], mgn_ref[:, cols]).astype(BF16)


def _mixer_core_call(x, mod, mod_row, p, seq_len, group_rows, rope, init):
    t = x.shape[0]
    n_groups = t // group_rows
    n_chunks = group_rows // CHUNK
    seq_per_group = group_rows // seq_len
    use_rope = rope is not None
    has_init = init is not None
    emit_state = not has_init

    inputs = [x, mod, p["norm_mix"], p["w_rqk"], p["w_rv"], p["w_rg"], p["w_mqk"], p["w_mv"], p["w_mo"],
              p["w_gates"], p["conv_w"], p["conv_b"], p["gate_bias"], p["decay_rows"], p["ret_gn"], p["m_gn"]]
    in_specs = [
        pl.BlockSpec((group_rows, D_MODEL), lambda i: (i, 0)),
        pl.BlockSpec((1, N_MOD, D_MODEL), lambda i: (mod_row(i * group_rows), 0, 0)),
    ] + [_const_spec(a.shape) for a in inputs[2:]]
    if use_rope:
        inputs += [rope[0], rope[1]]
        in_specs += [_const_spec(rope[0].shape), _const_spec(rope[1].shape)]
    if has_init:
        s0, e0, m0 = init
        inputs += [s0, e0, m0]
        in_specs += [
            pl.BlockSpec((1, 1, N_DIR, H_RET, DK_RET, DV_RET), lambda i: (i, 0, 0, 0, 0, 0)),
            pl.BlockSpec((1, N_PAIR, DH_M, 2 * DH_M), lambda i: (i, 0, 0, 0)),
            pl.BlockSpec((1, N_PAIR, CHUNK), lambda i: (i, 0, 0)),
        ]

    out_shape = [jax.ShapeDtypeStruct((t, RET_V), BF16), jax.ShapeDtypeStruct((t, M_W), BF16)]
    out_specs = [pl.BlockSpec((group_rows, RET_V), lambda i: (i, 0)),
                 pl.BlockSpec((group_rows, M_W), lambda i: (i, 0))]
    if emit_state:
        n_seq = t // seq_len
        out_shape += [
            jax.ShapeDtypeStruct((n_seq, 1, N_DIR, H_RET, DK_RET, DV_RET), F32),
            jax.ShapeDtypeStruct((n_seq, N_PAIR, DH_M, 2 * DH_M), F32),
            jax.ShapeDtypeStruct((n_seq, N_DIR, N_PAIR, CHUNK), F32),
        ]
        out_specs += [
            pl.BlockSpec((seq_per_group, 1, N_DIR, H_RET, DK_RET, DV_RET), lambda i: (i, 0, 0, 0, 0, 0)),
            pl.BlockSpec((seq_per_group, N_PAIR, DH_M, 2 * DH_M), lambda i: (i, 0, 0, 0)),
            pl.BlockSpec((seq_per_group, N_DIR, N_PAIR, CHUNK), lambda i: (i, 0, 0, 0)),
        ]

    scratch = [
        pltpu.VMEM((group_rows, D_MODEL), BF16),
        pltpu.VMEM((group_rows, RET_QK), F32),
        pltpu.VMEM((group_rows, RET_QK), F32),
        pltpu.VMEM((group_rows, RET_V), BF16),
        pltpu.VMEM((n_chunks, N_PAIR, DK_RET, DV_RET), F32),
        pltpu.VMEM((group_rows, RET_V), F32),
        pltpu.VMEM((group_rows, GATE_LANES), F32),
        pltpu.VMEM((H_RET, 5, CHUNK, CHUNK), F32),
        pltpu.VMEM((n_chunks, 2, N_PAIR, CHUNK), F32),
        pltpu.VMEM((n_chunks, N_DIR, N_PAIR, CHUNK), F32),
    ]
    kern = functools.partial(_mixer_core_kernel, group_rows=group_rows, seq_len=seq_len, use_rope=use_rope,
                             has_init=has_init, emit_state=emit_state)
    return pl.pallas_call(
        kern,
        out_shape=out_shape,
        grid=(n_groups,),
        in_specs=in_specs,
        out_specs=out_specs,
        scratch_shapes=scratch,
        compiler_params=pltpu.CompilerParams(
            dimension_semantics=("arbitrary",), vmem_limit_bytes=VMEM_LIMIT),
        name="mixer_core",
    )(*inputs)


def _mixer_out_kernel(x_ref, rg_ref, hm_ref, mod_ref, nmix_ref, wbg_ref, wru_ref, wmu_ref, wout_ref,
                      nffn_ref, w1_ref, w3_ref, w2_ref, nfin_ref, o_ref):
    mod = mod_ref[0]
    x = x_ref[...]
    u = _norm_mod(x, nmix_ref[...], mod[3:4], mod[4:5]).astype(BF16)
    gates = _sigmoid(_dot(u, wbg_ref[...]))
    ret_branch = _dot(rg_ref[...], wru_ref[...])
    m_branch = _dot(hm_ref[...], wmu_ref[...])
    merged = (gates[:, :D_MODEL] * ret_branch + gates[:, D_MODEL:] * m_branch).astype(BF16)
    x = x + mod[5:6] * _dot(merged, wout_ref[...])
    x = _swiglu_half_step(x, mod, 6, nffn_ref[...], w1_ref, w3_ref, w2_ref)
    ms = jnp.mean(x * x, axis=-1, keepdims=True)
    o_ref[...] = x * lax.rsqrt(ms + EPS) * nfin_ref[...]


def _mixer_out_call(x, rgated, hmn, mod, mod_row, p):
    t = x.shape[0]
    tm = FFN_TILE
    consts = [p["norm_mix"], p["w_bg"], p["w_ret_up"], p["w_m_up"], p["w_out"],
              p["norm_ffn2"], p["w1_ffn2"], p["w3_ffn2"], p["w2_ffn2"], p["norm_final"]]
    return pl.pallas_call(
        _mixer_out_kernel,
        out_shape=jax.ShapeDtypeStruct((t, D_MODEL), F32),
        grid=(t // tm,),
        in_specs=[
            pl.BlockSpec((tm, D_MODEL), lambda i: (i, 0)),
            pl.BlockSpec((tm, RET_V), lambda i: (i, 0)),
            pl.BlockSpec((tm, M_W), lambda i: (i, 0)),
            pl.BlockSpec((1, N_MOD, D_MODEL), lambda i: (mod_row(i * tm), 0, 0)),
        ] + [_const_spec(a.shape) for a in consts],
        out_specs=pl.BlockSpec((tm, D_MODEL), lambda i: (i, 0)),
        compiler_params=pltpu.CompilerParams(
            dimension_semantics=("arbitrary",), vmem_limit_bytes=VMEM_LIMIT),
        name="mixer_out",
    )(x, rgated, hmn, mod, *consts)


def _rope_tables(seq_len):
    rows = seq_len // GRID_W
    r = jnp.repeat(jnp.arange(rows, dtype=F32), GRID_W)
    col = (jnp.arange(seq_len) % GRID_W).astype(F32)
    n_f = DK_RET // 4
    freqs = ROPE_BASE ** (-jnp.arange(n_f, dtype=F32) / n_f)
    ang = jnp.concatenate([r[:, None] * freqs, col[:, None] * freqs], axis=-1)
    cos, sin = jnp.cos(ang), jnp.sin(ang)
    return jnp.concatenate([cos, cos], axis=-1), jnp.concatenate([-sin, sin], axis=-1)


def _layer_params(l, norm_ffn1, w1_ffn1, w3_ffn1, w2_ffn1, norm_mix, w_in, conv_w, conv_b,
                  ret_decay_logit, b_igate, b_fgate, ret_gn, m_gn, w_ret_up, w_m_up, w_out,
                  norm_ffn2, w1_ffn2, w3_ffn2, w2_ffn2, norm_final):
    w = w_in[l]
    o = 0
    cols = {}
    for name, width in (("w_rqk", 2 * RET_QK), ("w_rv", RET_V), ("w_rg", RET_V), ("w_mqk", 2 * M_W),
                        ("w_mv", M_W), ("w_mo", M_W), ("w_gates", 2 * N_PAIR), ("w_bg", N_DIR * D_MODEL)):
        cols[name] = w[:, o:o + width].astype(BF16)
        o += width
    cols["w_gates"] = jnp.pad(cols["w_gates"], ((0, 0), (0, GATE_LANES - 2 * N_PAIR)))
    gate_bias = jnp.pad(jnp.concatenate([b_igate[l], b_fgate[l]]), (0, GATE_LANES - 2 * N_PAIR))
    p = dict(cols)
    p.update(
        norm_ffn1=norm_ffn1[l][None, :], norm_mix=norm_mix[l][None, :], norm_ffn2=norm_ffn2[l][None, :],
        norm_final=norm_final[None, :],
        w1_ffn1=w1_ffn1[l].astype(BF16), w3_ffn1=w3_ffn1[l].astype(BF16), w2_ffn1=w2_ffn1[l].astype(BF16),
        w1_ffn2=w1_ffn2[l].astype(BF16), w3_ffn2=w3_ffn2[l].astype(BF16), w2_ffn2=w2_ffn2[l].astype(BF16),
        w_ret_up=w_ret_up[l].astype(BF16), w_m_up=w_m_up[l].astype(BF16), w_out=w_out[l].astype(BF16),
        conv_w=conv_w[l], conv_b=conv_b[l][None, :],
        gate_bias=gate_bias[None, :],
        decay_rows=jnp.broadcast_to(ret_decay_logit[l].reshape(N_DIR * H_RET, 1), (N_DIR * H_RET, DV_RET)),
        ret_gn=ret_gn[l][None, :], m_gn=m_gn[l][None, :],
    )
    return p


def kernel(x_prompt, x_sample, c, state_ret, state_mlstm_C, state_mlstm_n, state_mlstm_m, c_ctx, w_ada, b_ada, norm_ffn1, w1_ffn1, w3_ffn1, w2_ffn1, norm_mix, w_in, conv_w, conv_b, ret_decay_logit, b_igate, b_fgate, ret_gn, m_gn, w_ret_up, w_m_up, w_out, norm_ffn2, w1_ffn2, w3_ffn2, w2_ffn2, norm_final):
    bp, lp, _ = x_prompt.shape
    bs, ls, _ = x_sample.shape
    depth = w_ada.shape[0]
    assert depth == 1 and lp % CHUNK == 0 and PROMPT_GROUP_ROWS % lp == 0 and ls <= MAX_GROUP_ROWS
    assert lp & (lp - 1) == 0 and ls & (ls - 1) == 0

    ctx_row = 8 * ((bs + 7) // 8)
    cvec = jnp.zeros((ctx_row + 8, D_MODEL), F32).at[:bs].set(c).at[ctx_row].set(c_ctx)
    rope = _rope_tables(ls)

    xp = x_prompt.reshape(bp * lp, D_MODEL)
    xs = x_sample.reshape(bs * ls, D_MODEL)
    prompt_row = lambda r: ctx_row
    sample_row = lambda r: r // ls

    l = 0
    p = _layer_params(l, norm_ffn1, w1_ffn1, w3_ffn1, w2_ffn1, norm_mix, w_in, conv_w, conv_b,
                      ret_decay_logit, b_igate, b_fgate, ret_gn, m_gn, w_ret_up, w_m_up, w_out,
                      norm_ffn2, w1_ffn2, w3_ffn2, w2_ffn2, norm_final)
    mod = _ada_call(cvec, w_ada[l], b_ada[l][None, :]).reshape(ctx_row + 8, N_MOD, D_MODEL)

    c0_t = jnp.swapaxes(state_mlstm_C[:, l].astype(F32), -1, -2).reshape(bs, N_PAIR, DH_M, DH_M)
    n0 = state_mlstm_n[:, l].astype(F32).reshape(bs, N_PAIR, DH_M, 1)
    e0 = jnp.concatenate([c0_t, n0, jnp.zeros((bs, N_PAIR, DH_M, DH_M - 1), F32)], axis=-1)
    m0 = jnp.broadcast_to(state_mlstm_m[:, l].astype(F32).reshape(bs, N_PAIR, 1), (bs, N_PAIR, CHUNK))
    init = (state_ret.astype(F32), e0, m0)

    xp1 = _ffn_call(xp, mod, prompt_row, p["norm_ffn1"], p["w1_ffn1"], p["w3_ffn1"], p["w2_ffn1"])
    xs1 = _ffn_call(xs, mod, sample_row, p["norm_ffn1"], p["w1_ffn1"], p["w3_ffn1"], p["w2_ffn1"])

    rg_p, hm_p, s_fin, e_fin, m_fin = _mixer_core_call(xp1, mod, prompt_row, p, lp, PROMPT_GROUP_ROWS, None, None)
    rg_s, hm_s = _mixer_core_call(xs1, mod, sample_row, p, ls, ls, rope, init)

    yp = _mixer_out_call(xp1, rg_p, hm_p, mod, prompt_row, p)
    ys = _mixer_out_call(xs1, rg_s, hm_s, mod, sample_row, p)

    dt = x_prompt.dtype
    new_c = jnp.swapaxes(e_fin[..., :DH_M], -1, -2).reshape(bp, 1, N_DIR, H_M, DH_M, DH_M)
    new_n = e_fin[..., DH_M].reshape(bp, 1, N_DIR, H_M, DH_M)
    m_diag = jnp.stack([m_fin[:, d, d * H_M:(d + 1) * H_M, 0] for d in range(N_DIR)], axis=1)
    new_m = m_diag.reshape(bp, 1, N_DIR, H_M)
    return (yp.reshape(bp, lp, D_MODEL).astype(dt), ys.reshape(bs, ls, D_MODEL).astype(dt),
            s_fin.astype(dt), new_c.astype(dt), new_n.astype(dt), new_m.astype(dt))
```

```python
import functools

import jax
import jax.numpy as jnp
from jax import lax
from jax.experimental import pallas as pl
from jax.experimental.pallas import tpu as pltpu

F32 = jnp.float32
BF16 = jnp.bfloat16

D_MODEL = 1024
D_FF = 2816
CHUNK = 128
N_DIR = 2
H_RET = 4
DK_RET = 128
DV_RET = 256
H_M = 4
DH_M = 128
RET_QK = H_RET * DK_RET
RET_V = H_RET * DV_RET
M_W = H_M * DH_M
GRID_W = 64
ROPE_BASE = 10000.0
EPS = 1e-6
GN_EPS = 1e-5
N_MOD = 9
N_PAIR = N_DIR * H_M
GATE_LANES = 128
SUBLANES = 8

MAX_GROUP_ROWS = 1024
PROMPT_GROUP_ROWS = 512
ROW_BLOCK = 256
FFN_TILE = 512
VMEM_LIMIT = 60 * 1024 * 1024


def _sigmoid(x):
    return 1.0 / (1.0 + jnp.exp(-x))


def _silu(x):
    return x * _sigmoid(x)


def _log_sigmoid(x):
    return -(jnp.maximum(-x, 0.0) + jnp.log1p(jnp.exp(-jnp.abs(x))))


def _norm_mod(x, gain, shift, scale):
    ms = jnp.mean(x * x, axis=-1, keepdims=True)
    y = x * lax.rsqrt(ms + EPS) * gain
    return y * (1.0 + scale) + shift


def _dot(a, b):
    return jnp.dot(a, b, preferred_element_type=F32)


def _dot_nt(a, b):
    return lax.dot_general(a, b, (((1,), (1,)), ((), ())), preferred_element_type=F32)


def _head_norm(o, gain):
    mu = jnp.mean(o, axis=-1, keepdims=True)
    oc = o - mu
    var = jnp.mean(oc * oc, axis=-1, keepdims=True)
    return oc * lax.rsqrt(var + GN_EPS) * gain


def _ada_kernel(c_ref, w_ref, b_ref, o_ref):
    s = _silu(c_ref[...]).astype(BF16)
    o_ref[...] = _dot(s, w_ref[...].astype(BF16)) + b_ref[...]


def _ada_call(cvec, w_ada, b_ada):
    rows = cvec.shape[0]
    n_out = w_ada.shape[1]
    tile = D_MODEL
    return pl.pallas_call(
        _ada_kernel,
        out_shape=jax.ShapeDtypeStruct((rows, n_out), F32),
        grid=(n_out // tile,),
        in_specs=[
            pl.BlockSpec((rows, D_MODEL), lambda j: (0, 0)),
            pl.BlockSpec((D_MODEL, tile), lambda j: (0, j)),
            pl.BlockSpec((1, tile), lambda j: (0, j)),
        ],
        out_specs=pl.BlockSpec((rows, tile), lambda j: (0, j)),
        compiler_params=pltpu.CompilerParams(
            dimension_semantics=("arbitrary",), vmem_limit_bytes=VMEM_LIMIT),
        name="ada",
    )(cvec, w_ada, b_ada)


def _swiglu_half_step(x, mod, k0, gain, w1_ref, w3_ref, w2_ref):
    u = _norm_mod(x, gain, mod[k0:k0 + 1], mod[k0 + 1:k0 + 2]).astype(BF16)
    h1 = _dot(u, w1_ref[...])
    h3 = _dot(u, w3_ref[...])
    h = (_silu(h1) * h3).astype(BF16)
    y = _dot(h, w2_ref[...])
    return x + 0.5 * mod[k0 + 2:k0 + 3] * y


def _ffn_kernel(x_ref, mod_ref, g_ref, w1_ref, w3_ref, w2_ref, o_ref):
    o_ref[...] = _swiglu_half_step(x_ref[...], mod_ref[0], 0, g_ref[...], w1_ref, w3_ref, w2_ref)


def _const_spec(shape):
    nd = len(shape)
    return pl.BlockSpec(shape, lambda i: (0,) * nd)


def _ffn_call(x, mod, mod_row, gain, w1, w3, w2):
    t = x.shape[0]
    tm = FFN_TILE
    return pl.pallas_call(
        _ffn_kernel,
        out_shape=jax.ShapeDtypeStruct((t, D_MODEL), F32),
        grid=(t // tm,),
        in_specs=[
            pl.BlockSpec((tm, D_MODEL), lambda i: (i, 0)),
            pl.BlockSpec((1, N_MOD, D_MODEL), lambda i: (mod_row(i * tm), 0, 0)),
            _const_spec((1, D_MODEL)),
            _const_spec((D_MODEL, D_FF)),
            _const_spec((D_MODEL, D_FF)),
            _const_spec((D_FF, D_MODEL)),
        ],
        out_specs=pl.BlockSpec((tm, D_MODEL), lambda i: (i, 0)),
        compiler_params=pltpu.CompilerParams(
            dimension_semantics=("arbitrary",), vmem_limit_bytes=VMEM_LIMIT),
        name="ffn",
    )(x, mod, gain, w1, w3, w2)


def _mixer_core_kernel(*refs, group_rows, seq_len, use_rope, has_init, emit_state):
    it = iter(refs)
    x_ref, mod_ref, nmix_ref = next(it), next(it), next(it)
    wrqk_ref, wrv_ref, wrg_ref = next(it), next(it), next(it)
    wmqk_ref, wmv_ref, wmo_ref, wgt_ref = next(it), next(it), next(it), next(it)
    convw_ref, convb_ref, gbias_ref, decay_ref = next(it), next(it), next(it), next(it)
    rgn_ref, mgn_ref = next(it), next(it)
    if use_rope:
        cos_ref, sin_ref = next(it), next(it)
    if has_init:
        s0_ref, c0_ref, n0_ref, m0_ref = next(it), next(it), next(it), next(it)
    rg_out, hm_out = next(it), next(it)
    if emit_state:
        sfin_ref, cfin_ref, nfin_ref, mfin_ref = next(it), next(it), next(it), next(it)
    u_s, q_s, k_s, v_s, vt_s, kv_s, wide_s, igb_s, dec_s, stat_s, mprev_s = (next(it) for _ in range(11))

    rows_total = group_rows
    n_chunks = rows_total // CHUNK
    cps = seq_len // CHUNK
    n_seq = rows_total // seq_len
    n_rb = rows_total // ROW_BLOCK
    mod = mod_ref[0]

    def rb_rows(rb):
        return slice(rb * ROW_BLOCK, (rb + 1) * ROW_BLOCK)

    def head_cols(h, width):
        return slice(h * width, (h + 1) * width)

    for rb in range(n_rb):
        rows = rb_rows(rb)
        ub = _norm_mod(x_ref[rows, :], nmix_ref[...], mod[3:4], mod[4:5]).astype(BF16)
        u_s[rows, :] = ub
        qk = _dot(ub, wrqk_ref[...])
        q = qk[:, :RET_QK]
        k = qk[:, RET_QK:] * (DK_RET ** -0.5)
        for h in range(H_RET):
            cols = head_cols(h, DK_RET)
            qh, kh = q[:, cols], k[:, cols]
            if use_rope:
                cs, sn = cos_ref[rows, :], sin_ref[rows, :]
                qh = qh * cs + pltpu.roll(qh, DK_RET // 2, 1) * sn
                kh = kh * cs + pltpu.roll(kh, DK_RET // 2, 1) * sn
            q_s[rows, cols] = qh
            k_s[rows, cols] = kh
        v_s[rows, :] = _dot(ub, wrv_ref[...]).astype(BF16)

    pos_i = lax.broadcasted_iota(jnp.int32, (CHUNK, CHUNK), 0).astype(F32)
    pos_j = lax.broadcasted_iota(jnp.int32, (CHUNK, CHUNK), 1).astype(F32)
    lg_rows = _log_sigmoid(decay_ref[...])
    for h in range(H_RET):
        lgf = lg_rows[h:h + 1, :CHUNK]
        lgb = lg_rows[H_RET + h:H_RET + h + 1, :CHUNK]
        dec_s[h, 0] = jnp.where(pos_i > pos_j, jnp.exp((pos_i - pos_j) * lgf),
                                jnp.where(pos_i < pos_j, jnp.exp((pos_j - pos_i) * lgb), 2.0))
        dec_s[h, 1] = jnp.exp((CHUNK - 1.0 - pos_j) * lgf)
        dec_s[h, 2] = jnp.exp(pos_j * lgb)
        dec_s[h, 3] = jnp.exp((pos_i + 1.0) * lgf)
        dec_s[h, 4] = jnp.exp((CHUNK - pos_i) * lgb)
    g_chunk = jnp.exp(CHUNK * lg_rows)

    def ret_pass1(n, carry):
        rows = pl.ds(pl.multiple_of(n * CHUNK, CHUNK), CHUNK)
        for h in range(H_RET):
            k_t = k_s[rows, head_cols(h, DK_RET)].T
            kcat = jnp.concatenate([k_t * dec_s[h, 1], k_t * dec_s[h, 2]], axis=0).astype(BF16)
            kv = _dot(kcat, v_s[rows, head_cols(h, DV_RET)])
            kv_s[n, 2 * h] = kv[:DK_RET]
            kv_s[n, 2 * h + 1] = kv[DK_RET:]
        return carry

    lax.fori_loop(0, n_chunks, ret_pass1, 0)

    for g in range(n_seq):
        for h in range(H_RET):
            for d in range(N_DIR):
                slot = 2 * h + d
                decay = g_chunk[d * H_RET + h:d * H_RET + h + 1, :]
                state = s0_ref[0, 0, d, h] if has_init else jnp.zeros((DK_RET, DV_RET), F32)
                order = range(g * cps, (g + 1) * cps)
                for n in (order if d == 0 else reversed(order)):
                    local = kv_s[n, slot]
                    kv_s[n, slot] = state
                    state = decay * state + local
                if emit_state:
                    sfin_ref[g, 0, d, h] = state

    def ret_pass2(n, carry):
        rows = pl.ds(pl.multiple_of(n * CHUNK, CHUNK), CHUNK)
        for h in range(H_RET):
            qh = q_s[rows, head_cols(h, DK_RET)]
            kh = k_s[rows, head_cols(h, DK_RET)]
            vh = v_s[rows, head_cols(h, DV_RET)]
            scores = _dot_nt(qh.astype(BF16), kh.astype(BF16))
            sm = (scores * dec_s[h, 0]).astype(BF16)
            qcat = jnp.concatenate([qh * dec_s[h, 3], qh * dec_s[h, 4]], axis=1).astype(BF16)
            scat = jnp.concatenate([kv_s[n, 2 * h], kv_s[n, 2 * h + 1]], axis=0).astype(BF16)
            o = _dot(sm, vh) + _dot(qcat, scat)
            wide_s[rows, head_cols(h, DV_RET)] = _head_norm(o, rgn_ref[:, head_cols(h, DV_RET)])
        return carry

    lax.fori_loop(0, n_chunks, ret_pass2, 0)

    for rb in range(n_rb):
        rows = rb_rows(rb)
        rg = _dot(u_s[rows, :], wrg_ref[...])
        rg_out[rows, :] = (wide_s[rows, :] * _silu(rg)).astype(BF16)

    chunks_per_rb = ROW_BLOCK // CHUNK
    for rb in range(n_rb):
        rows = rb_rows(rb)
        ub = u_s[rows, :]
        wide_s[rows, :] = _dot(ub, wmqk_ref[...])
        mv = _dot(ub, wmv_ref[...])
        for ch in range(chunks_per_rb):
            for h in range(H_M):
                vt_s[rb * chunks_per_rb + ch, h] = mv[ch * CHUNK:(ch + 1) * CHUNK, head_cols(h, DH_M)].T
        gt = _dot(ub, wgt_ref[...]) + gbias_ref[...]
        glane = lax.broadcasted_iota(jnp.int32, (ROW_BLOCK, GATE_LANES), 1)
        igb_s[rows, :] = jnp.where(glane < N_PAIR, gt, _log_sigmoid(gt))

    row_id = lax.broadcasted_iota(jnp.int32, (rows_total, CHUNK), 0)
    seq_pos = row_id & (seq_len - 1)
    for cb in range(2 * M_W // CHUNK):
        cols = slice(cb * CHUNK, (cb + 1) * CHUNK)
        xc = wide_s[:, cols]
        prev = jnp.where(seq_pos == 0, 0.0, pltpu.roll(xc, 1, 0))
        nxt = jnp.where(seq_pos == seq_len - 1, 0.0, pltpu.roll(xc, rows_total - 1, 0))
        y = convb_ref[:, cols] + convw_ref[0:1, cols] * prev
        y = y + convw_ref[1:2, cols] * xc
        y = y + convw_ref[2:3, cols] * nxt
        y = _silu(y)
        if cb < M_W // CHUNK:
            q_s[:, cols] = y
        else:
            k_s[:, slice((cb - M_W // CHUNK) * CHUNK, (cb - M_W // CHUNK + 1) * CHUNK)] = y * (DH_M ** -0.5)

    z = igb_s[...]
    cpos = row_id & (CHUNK - 1)
    glane = lax.broadcasted_iota(jnp.int32, (rows_total, GATE_LANES), 1)
    pre, suf = z, z
    step = 1
    while step < CHUNK:
        pre = pre + jnp.where(cpos >= step, pltpu.roll(pre, step, 0), 0.0)
        suf = suf + jnp.where(cpos < CHUNK - step, pltpu.roll(suf, rows_total - step, 0), 0.0)
        step *= 2
    cum = jnp.where(glane < N_PAIR + H_M, pre, suf)
    i_minus_b = z - pltpu.roll(cum, GATE_LANES - N_PAIR, 1)
    igb_s[...] = jnp.where(glane < N_PAIR, z,
                           jnp.where(glane < 2 * N_PAIR, cum, pltpu.roll(i_minus_b, 2 * N_PAIR, 1)))

    def m_pass1(n, carry):
        rows = pl.ds(pl.multiple_of(n * CHUNK, CHUNK), CHUNK)
        zt = igb_s[rows, :].T
        ig_t = zt[0:N_PAIR, :]
        b_t = zt[N_PAIR:2 * N_PAIR, :]
        pair = lax.broadcasted_iota(jnp.int32, (N_PAIR, 1), 0)
        b_last = jnp.where(pair < H_M, b_t[:, CHUNK - 1:CHUNK], b_t[:, 0:1])
        g_t = (b_last - b_t) + ig_t
        m_loc = jnp.max(g_t, axis=-1, keepdims=True)
        w_t = jnp.exp(g_t - m_loc)
        stat_s[n, 0] = jnp.broadcast_to(b_last, (N_PAIR, CHUNK))
        stat_s[n, 1] = jnp.broadcast_to(m_loc, (N_PAIR, CHUNK))
        for h in range(H_M):
            kh = k_s[rows, head_cols(h, DH_M)].astype(BF16)
            v_t = vt_s[n, h]
            for d in range(N_DIR):
                c = d * H_M + h
                w_row = w_t[c:c + 1, :]
                lhs = jnp.concatenate([v_t * w_row, jnp.broadcast_to(w_row, (SUBLANES, CHUNK))], axis=0)
                local = _dot(lhs.astype(BF16), kh)
                kv_s[n, c, :, :DH_M] = local[:DH_M]
                kv_s[n, c, 0:SUBLANES, DH_M:] = local[DH_M:]
        return carry

    lax.fori_loop(0, n_chunks, m_pass1, 0)

    for g in range(n_seq):
        for d in range(N_DIR):
            order = list(range(g * cps, (g + 1) * cps))
            if d == 1:
                order.reverse()
            m_prev = m0_ref[0] if has_init else jnp.zeros((N_PAIR, CHUNK), F32)
            a1s, a2s = [], []
            for n in order:
                b_last, m_loc = stat_s[n, 0], stat_s[n, 1]
                m_new = jnp.maximum(b_last + m_prev, m_loc)
                a1s.append(jnp.exp(b_last + m_prev - m_new))
                a2s.append(jnp.exp(m_loc - m_new))
                mprev_s[n, d] = m_prev
                m_prev = m_new
            if emit_state:
                mfin_ref[g, d] = m_prev
            for h in range(H_M):
                c = d * H_M + h
                if has_init:
                    mem = c0_ref[0, 0, d, h]
                    nrm = jnp.broadcast_to(n0_ref[0][c:c + 1, :], (SUBLANES, DH_M))
                else:
                    mem = jnp.zeros((DH_M, DH_M), F32)
                    nrm = jnp.zeros((SUBLANES, DH_M), F32)
                for idx, n in enumerate(order):
                    a1 = a1s[idx][c:c + 1, :]
                    a2 = a2s[idx][c:c + 1, :]
                    local_mem = kv_s[n, c, :, :DH_M]
                    local_nrm = kv_s[n, c, 0:SUBLANES, DH_M:]
                    kv_s[n, c, :, :DH_M] = mem
                    kv_s[n, c, 0:SUBLANES, DH_M:] = nrm
                    mem = a1 * mem + a2 * local_mem
                    nrm = a1 * nrm + a2 * local_nrm
                if emit_state:
                    cfin_ref[g, 0, d, h] = mem
                    nfin_ref[g, c:c + 1, :] = nrm[0:1]

    tri_j = lax.broadcasted_iota(jnp.int32, (CHUNK, CHUNK), 0)
    tri_i = lax.broadcasted_iota(jnp.int32, (CHUNK, CHUNK), 1)

    def m_pass2(n, carry):
        rows = pl.ds(pl.multiple_of(n * CHUNK, CHUNK), CHUNK)
        zc = igb_s[rows, :]
        zt = zc.T
        for h in range(H_M):
            qh = q_s[rows, head_cols(h, DH_M)].astype(BF16)
            kh = k_s[rows, head_cols(h, DH_M)].astype(BF16)
            v_t = vt_s[n, h].astype(BF16)
            scores_t = _dot_nt(kh, qh)
            h_sum_t = jnp.zeros((DH_M, CHUNK), F32)
            for d in range(N_DIR):
                c = d * H_M + h
                a_col = zc[:, 2 * N_PAIR + c:2 * N_PAIR + c + 1]
                b_row = zt[N_PAIR + c:N_PAIR + c + 1, :]
                mask = (tri_j <= tri_i) if d == 0 else (tri_j >= tri_i)
                dlog = jnp.where(mask, b_row + a_col, -jnp.inf)
                lin = b_row + mprev_s[n, d][c:c + 1, :]
                m_i = jnp.maximum(jnp.max(dlog, axis=0, keepdims=True), lin)
                s_t = scores_t * jnp.exp(dlog - m_i)
                w_int = jnp.exp(lin - m_i)
                mem = jnp.concatenate([kv_s[n, c, :, :DH_M], kv_s[n, c, 0:SUBLANES, DH_M:]], axis=0)
                cross = _dot_nt(mem.astype(BF16), qh)
                num = _dot(v_t, s_t.astype(BF16)) + cross[:DH_M] * w_int
                den = jnp.sum(s_t, axis=0, keepdims=True) + cross[DH_M:DH_M + 1] * w_int
                denom = jnp.maximum(jnp.abs(den), jnp.exp(-m_i))
                h_sum_t = h_sum_t + num * (1.0 / denom)
            wide_s[rows, head_cols(h, DH_M)] = h_sum_t.T
        return carry

    lax.fori_loop(0, n_chunks, m_pass2, 0)

    for rb in range(n_rb):
        rows = rb_rows(rb)
        mo = _dot(u_s[rows, :], wmo_ref[...])
        hm = _sigmoid(mo) * wide_s[rows, :M_W]
        for h in range(H_M):
            cols = head_cols(h, DH_M)
            hm_out[rows, cols] = _head_norm(hm[:, cols], mgn_ref[:, cols]).astype(BF16)


def _mixer_core_call(x, mod, mod_row, p, seq_len, group_rows, rope, init):
    t = x.shape[0]
    n_groups = t // group_rows
    n_chunks = group_rows // CHUNK
    seq_per_group = group_rows // seq_len
    use_rope = rope is not None
    has_init = init is not None
    emit_state = not has_init

    inputs = [x, mod, p["norm_mix"], p["w_rqk"], p["w_rv"], p["w_rg"], p["w_mqk"], p["w_mv"], p["w_mo"],
              p["w_gates"], p["conv_w"], p["conv_b"], p["gate_bias"], p["decay_rows"], p["ret_gn"], p["m_gn"]]
    in_specs = [
        pl.BlockSpec((group_rows, D_MODEL), lambda i: (i, 0), pipeline_mode=pl.Buffered(1)),
        pl.BlockSpec((1, N_MOD, D_MODEL), lambda i: (mod_row(i * group_rows), 0, 0)),
    ] + [_const_spec(a.shape) for a in inputs[2:]]
    if use_rope:
        inputs += [rope[0], rope[1]]
        in_specs += [_const_spec(rope[0].shape), _const_spec(rope[1].shape)]
    if has_init:
        s0, c0, n0, m0 = init
        inputs += [s0, c0, n0, m0]
        in_specs += [
            pl.BlockSpec((1, 1, N_DIR, H_RET, DK_RET, DV_RET), lambda i: (i, 0, 0, 0, 0, 0)),
            pl.BlockSpec((1, 1, N_DIR, H_M, DH_M, DH_M), lambda i: (i, 0, 0, 0, 0, 0)),
            pl.BlockSpec((1, N_PAIR, DH_M), lambda i: (i, 0, 0)),
            pl.BlockSpec((1, N_PAIR, CHUNK), lambda i: (i, 0, 0)),
        ]

    out_shape = [jax.ShapeDtypeStruct((t, RET_V), BF16), jax.ShapeDtypeStruct((t, M_W), BF16)]
    out_specs = [pl.BlockSpec((group_rows, RET_V), lambda i: (i, 0)),
                 pl.BlockSpec((group_rows, M_W), lambda i: (i, 0))]
    if emit_state:
        n_seq = t // seq_len
        out_shape += [
            jax.ShapeDtypeStruct((n_seq, 1, N_DIR, H_RET, DK_RET, DV_RET), F32),
            jax.ShapeDtypeStruct((n_seq, 1, N_DIR, H_M, DH_M, DH_M), F32),
            jax.ShapeDtypeStruct((n_seq, N_PAIR, DH_M), F32),
            jax.ShapeDtypeStruct((n_seq, N_DIR, N_PAIR, CHUNK), F32),
        ]
        out_specs += [
            pl.BlockSpec((seq_per_group, 1, N_DIR, H_RET, DK_RET, DV_RET), lambda i: (i, 0, 0, 0, 0, 0)),
            pl.BlockSpec((seq_per_group, 1, N_DIR, H_M, DH_M, DH_M), lambda i: (i, 0, 0, 0, 0, 0)),
            pl.BlockSpec((seq_per_group, N_PAIR, DH_M), lambda i: (i, 0, 0)),
            pl.BlockSpec((seq_per_group, N_DIR, N_PAIR, CHUNK), lambda i: (i, 0, 0, 0)),
        ]

    scratch = [
        pltpu.VMEM((group_rows, D_MODEL), BF16),
        pltpu.VMEM((group_rows, RET_QK), F32),
        pltpu.VMEM((group_rows, RET_QK), F32),
        pltpu.VMEM((group_rows, RET_V), BF16),
        pltpu.VMEM((n_chunks, H_M, DH_M, CHUNK), F32),
        pltpu.VMEM((n_chunks, N_PAIR, DK_RET, DV_RET), F32),
        pltpu.VMEM((group_rows, RET_V), F32),
        pltpu.VMEM((group_rows, GATE_LANES), F32),
        pltpu.VMEM((H_RET, 5, CHUNK, CHUNK), F32),
        pltpu.VMEM((n_chunks, 2, N_PAIR, CHUNK), F32),
        pltpu.VMEM((n_chunks, N_DIR, N_PAIR, CHUNK), F32),
    ]
    kern = functools.partial(_mixer_core_kernel, group_rows=group_rows, seq_len=seq_len, use_rope=use_rope,
                             has_init=has_init, emit_state=emit_state)
    return pl.pallas_call(
        kern,
        out_shape=out_shape,
        grid=(n_groups,),
        in_specs=in_specs,
        out_specs=out_specs,
        scratch_shapes=scratch,
        compiler_params=pltpu.CompilerParams(
            dimension_semantics=("arbitrary",), vmem_limit_bytes=VMEM_LIMIT),
        name="mixer_core",
    )(*inputs)


def _mixer_out_kernel(x_ref, rg_ref, hm_ref, mod_ref, nmix_ref, wbg_ref, wru_ref, wmu_ref, wout_ref,
                      nffn_ref, w1_ref, w3_ref, w2_ref, nfin_ref, o_ref):
    mod = mod_ref[0]
    x = x_ref[...]
    u = _norm_mod(x, nmix_ref[...], mod[3:4], mod[4:5]).astype(BF16)
    gates = _sigmoid(_dot(u, wbg_ref[...]))
    ret_branch = _dot(rg_ref[...], wru_ref[...])
    m_branch = _dot(hm_ref[...], wmu_ref[...])
    merged = (gates[:, :D_MODEL] * ret_branch + gates[:, D_MODEL:] * m_branch).astype(BF16)
    x = x + mod[5:6] * _dot(merged, wout_ref[...])
    x = _swiglu_half_step(x, mod, 6, nffn_ref[...], w1_ref, w3_ref, w2_ref)
    ms = jnp.mean(x * x, axis=-1, keepdims=True)
    o_ref[...] = x * lax.rsqrt(ms + EPS) * nfin_ref[...]


def _mixer_out_call(x, rgated, hmn, mod, mod_row, p):
    t = x.shape[0]
    tm = FFN_TILE
    consts = [p["norm_mix"], p["w_bg"], p["w_ret_up"], p["w_m_up"], p["w_out"],
              p["norm_ffn2"], p["w1_ffn2"], p["w3_ffn2"], p["w2_ffn2"], p["norm_final"]]
    return pl.pallas_call(
        _mixer_out_kernel,
        out_shape=jax.ShapeDtypeStruct((t, D_MODEL), F32),
        grid=(t // tm,),
        in_specs=[
            pl.BlockSpec((tm, D_MODEL), lambda i: (i, 0)),
            pl.BlockSpec((tm, RET_V), lambda i: (i, 0)),
            pl.BlockSpec((tm, M_W), lambda i: (i, 0)),
            pl.BlockSpec((1, N_MOD, D_MODEL), lambda i: (mod_row(i * tm), 0, 0)),
        ] + [_const_spec(a.shape) for a in consts],
        out_specs=pl.BlockSpec((tm, D_MODEL), lambda i: (i, 0)),
        compiler_params=pltpu.CompilerParams(
            dimension_semantics=("arbitrary",), vmem_limit_bytes=VMEM_LIMIT),
        name="mixer_out",
    )(x, rgated, hmn, mod, *consts)


def _rope_tables(seq_len):
    rows = seq_len // GRID_W
    r = jnp.repeat(jnp.arange(rows, dtype=F32), GRID_W)
    col = (jnp.arange(seq_len) % GRID_W).astype(F32)
    n_f = DK_RET // 4
    freqs = ROPE_BASE ** (-jnp.arange(n_f, dtype=F32) / n_f)
    ang = jnp.concatenate([r[:, None] * freqs, col[:, None] * freqs], axis=-1)
    cos, sin = jnp.cos(ang), jnp.sin(ang)
    return jnp.concatenate([cos, cos], axis=-1), jnp.concatenate([-sin, sin], axis=-1)


def _layer_params(l, norm_ffn1, w1_ffn1, w3_ffn1, w2_ffn1, norm_mix, w_in, conv_w, conv_b,
                  ret_decay_logit, b_igate, b_fgate, ret_gn, m_gn, w_ret_up, w_m_up, w_out,
                  norm_ffn2, w1_ffn2, w3_ffn2, w2_ffn2, norm_final):
    w = w_in[l]
    o = 0
    cols = {}
    for name, width in (("w_rqk", 2 * RET_QK), ("w_rv", RET_V), ("w_rg", RET_V), ("w_mqk", 2 * M_W),
                        ("w_mv", M_W), ("w_mo", M_W), ("w_gates", 2 * N_PAIR), ("w_bg", N_DIR * D_MODEL)):
        cols[name] = w[:, o:o + width].astype(BF16)
        o += width
    cols["w_gates"] = jnp.pad(cols["w_gates"], ((0, 0), (0, GATE_LANES - 2 * N_PAIR)))
    gate_bias = jnp.pad(jnp.concatenate([b_igate[l], b_fgate[l]]), (0, GATE_LANES - 2 * N_PAIR))
    p = dict(cols)
    p.update(
        norm_ffn1=norm_ffn1[l][None, :], norm_mix=norm_mix[l][None, :], norm_ffn2=norm_ffn2[l][None, :],
        norm_final=norm_final[None, :],
        w1_ffn1=w1_ffn1[l].astype(BF16), w3_ffn1=w3_ffn1[l].astype(BF16), w2_ffn1=w2_ffn1[l].astype(BF16),
        w1_ffn2=w1_ffn2[l].astype(BF16), w3_ffn2=w3_ffn2[l].astype(BF16), w2_ffn2=w2_ffn2[l].astype(BF16),
        w_ret_up=w_ret_up[l].astype(BF16), w_m_up=w_m_up[l].astype(BF16), w_out=w_out[l].astype(BF16),
        conv_w=conv_w[l], conv_b=conv_b[l][None, :],
        gate_bias=gate_bias[None, :],
        decay_rows=jnp.broadcast_to(ret_decay_logit[l].reshape(N_DIR * H_RET, 1), (N_DIR * H_RET, DV_RET)),
        ret_gn=ret_gn[l][None, :], m_gn=m_gn[l][None, :],
    )
    return p


def kernel(x_prompt, x_sample, c, state_ret, state_mlstm_C, state_mlstm_n, state_mlstm_m, c_ctx, w_ada, b_ada, norm_ffn1, w1_ffn1, w3_ffn1, w2_ffn1, norm_mix, w_in, conv_w, conv_b, ret_decay_logit, b_igate, b_fgate, ret_gn, m_gn, w_ret_up, w_m_up, w_out, norm_ffn2, w1_ffn2, w3_ffn2, w2_ffn2, norm_final):
    bp, lp, _ = x_prompt.shape
    bs, ls, _ = x_sample.shape
    depth = w_ada.shape[0]
    assert depth == 1 and lp % CHUNK == 0 and PROMPT_GROUP_ROWS % lp == 0 and ls <= MAX_GROUP_ROWS
    assert lp & (lp - 1) == 0 and ls & (ls - 1) == 0

    ctx_row = 8 * ((bs + 7) // 8)
    cvec = jnp.zeros((ctx_row + 8, D_MODEL), F32).at[:bs].set(c).at[ctx_row].set(c_ctx)
    rope = _rope_tables(ls)

    xp = x_prompt.reshape(bp * lp, D_MODEL)
    xs = x_sample.reshape(bs * ls, D_MODEL)
    prompt_row = lambda r: ctx_row
    sample_row = lambda r: r // ls

    l = 0
    p = _layer_params(l, norm_ffn1, w1_ffn1, w3_ffn1, w2_ffn1, norm_mix, w_in, conv_w, conv_b,
                      ret_decay_logit, b_igate, b_fgate, ret_gn, m_gn, w_ret_up, w_m_up, w_out,
                      norm_ffn2, w1_ffn2, w3_ffn2, w2_ffn2, norm_final)
    mod = _ada_call(cvec, w_ada[l], b_ada[l][None, :]).reshape(ctx_row + 8, N_MOD, D_MODEL)

    n0 = state_mlstm_n[:, l].astype(F32).reshape(bs, N_PAIR, DH_M)
    m0 = jnp.broadcast_to(state_mlstm_m[:, l].astype(F32).reshape(bs, N_PAIR, 1), (bs, N_PAIR, CHUNK))
    init = (state_ret.astype(F32), state_mlstm_C.astype(F32), n0, m0)

    xp1 = _ffn_call(xp, mod, prompt_row, p["norm_ffn1"], p["w1_ffn1"], p["w3_ffn1"], p["w2_ffn1"])
    xs1 = _ffn_call(xs, mod, sample_row, p["norm_ffn1"], p["w1_ffn1"], p["w3_ffn1"], p["w2_ffn1"])

    rg_p, hm_p, s_fin, c_fin, n_fin, m_fin = _mixer_core_call(xp1, mod, prompt_row, p, lp, PROMPT_GROUP_ROWS, None, None)
    rg_s, hm_s = _mixer_core_call(xs1, mod, sample_row, p, ls, ls, rope, init)

    yp = _mixer_out_call(xp1, rg_p, hm_p, mod, prompt_row, p)
    ys = _mixer_out_call(xs1, rg_s, hm_s, mod, sample_row, p)

    dt = x_prompt.dtype
    new_c = c_fin
    new_n = n_fin.reshape(bp, 1, N_DIR, H_M, DH_M)
    m_diag = jnp.stack([m_fin[:, d, d * H_M:(d + 1) * H_M, 0] for d in range(N_DIR)], axis=1)
    new_m = m_diag.reshape(bp, 1, N_DIR, H_M)
    return (yp.reshape(bp, lp, D_MODEL).astype(dt), ys.reshape(bs, ls, D_MODEL).astype(dt),
            s_fin.astype(dt), new_c.astype(dt), new_n.astype(dt), new_m.astype(dt))
```

```python
import functools

import jax
import jax.numpy as jnp
from jax import lax
from jax.experimental import pallas as pl
from jax.experimental.pallas import tpu as pltpu

F32 = jnp.float32
BF16 = jnp.bfloat16

D_MODEL = 1024
D_FF = 2816
CHUNK = 128
N_DIR = 2
H_RET = 4
DK_RET = 128
DV_RET = 256
H_M = 4
DH_M = 128
RET_QK = H_RET * DK_RET
RET_V = H_RET * DV_RET
M_W = H_M * DH_M
GRID_W = 64
ROPE_BASE = 10000.0
EPS = 1e-6
GN_EPS = 1e-5
N_MOD = 9
N_PAIR = N_DIR * H_M
GATE_LANES = 128
SUBLANES = 8

MAX_GROUP_ROWS = 1024
PROMPT_GROUP_ROWS = 512
ROW_BLOCK = 256
CHUNK_UNROLL = 2
FFN_TILE = 512
VMEM_LIMIT = 60 * 1024 * 1024


def _sigmoid(x):
    return 1.0 / (1.0 + jnp.exp(-x))


def _silu(x):
    return x * _sigmoid(x)


def _log_sigmoid(x):
    return -(jnp.maximum(-x, 0.0) + jnp.log1p(jnp.exp(-jnp.abs(x))))


def _norm_mod(x, gain, shift, scale):
    ms = jnp.mean(x * x, axis=-1, keepdims=True)
    y = x * lax.rsqrt(ms + EPS) * gain
    return y * (1.0 + scale) + shift


def _dot(a, b):
    return jnp.dot(a, b, preferred_element_type=F32)


def _dot_nt(a, b):
    return lax.dot_general(a, b, (((1,), (1,)), ((), ())), preferred_element_type=F32)


def _head_norm(o, gain):
    mu = jnp.mean(o, axis=-1, keepdims=True)
    oc = o - mu
    var = jnp.mean(oc * oc, axis=-1, keepdims=True)
    return oc * lax.rsqrt(var + GN_EPS) * gain


def _ada_kernel(c_ref, w_ref, b_ref, o_ref):
    s = _silu(c_ref[...]).astype(BF16)
    o_ref[...] = _dot(s, w_ref[...].astype(BF16)) + b_ref[...]


def _ada_call(cvec, w_ada, b_ada):
    rows = cvec.shape[0]
    n_out = w_ada.shape[1]
    tile = D_MODEL
    return pl.pallas_call(
        _ada_kernel,
        out_shape=jax.ShapeDtypeStruct((rows, n_out), F32),
        grid=(n_out // tile,),
        in_specs=[
            pl.BlockSpec((rows, D_MODEL), lambda j: (0, 0)),
            pl.BlockSpec((D_MODEL, tile), lambda j: (0, j)),
            pl.BlockSpec((1, tile), lambda j: (0, j)),
        ],
        out_specs=pl.BlockSpec((rows, tile), lambda j: (0, j)),
        compiler_params=pltpu.CompilerParams(
            dimension_semantics=("arbitrary",), vmem_limit_bytes=VMEM_LIMIT),
        name="ada",
    )(cvec, w_ada, b_ada)


def _swiglu_half_step(x, mod, k0, gain, w1_ref, w3_ref, w2_ref):
    u = _norm_mod(x, gain, mod[k0:k0 + 1], mod[k0 + 1:k0 + 2]).astype(BF16)
    h1 = _dot(u, w1_ref[...])
    h3 = _dot(u, w3_ref[...])
    h = (_silu(h1) * h3).astype(BF16)
    y = _dot(h, w2_ref[...])
    return x + 0.5 * mod[k0 + 2:k0 + 3] * y


def _ffn_kernel(x_ref, mod_ref, g_ref, w1_ref, w3_ref, w2_ref, o_ref):
    o_ref[...] = _swiglu_half_step(x_ref[...], mod_ref[0], 0, g_ref[...], w1_ref, w3_ref, w2_ref)


def _const_spec(shape):
    nd = len(shape)
    return pl.BlockSpec(shape, lambda i: (0,) * nd)


def _ffn_call(x, mod, mod_row, gain, w1, w3, w2):
    t = x.shape[0]
    tm = FFN_TILE
    return pl.pallas_call(
        _ffn_kernel,
        out_shape=jax.ShapeDtypeStruct((t, D_MODEL), F32),
        grid=(t // tm,),
        in_specs=[
            pl.BlockSpec((tm, D_MODEL), lambda i: (i, 0)),
            pl.BlockSpec((1, N_MOD, D_MODEL), lambda i: (mod_row(i * tm), 0, 0)),
            _const_spec((1, D_MODEL)),
            _const_spec((D_MODEL, D_FF)),
            _const_spec((D_MODEL, D_FF)),
            _const_spec((D_FF, D_MODEL)),
        ],
        out_specs=pl.BlockSpec((tm, D_MODEL), lambda i: (i, 0)),
        compiler_params=pltpu.CompilerParams(
            dimension_semantics=("arbitrary",), vmem_limit_bytes=VMEM_LIMIT),
        name="ffn",
    )(x, mod, gain, w1, w3, w2)


def _mixer_core_kernel(*refs, group_rows, seq_len, use_rope, has_init, emit_state):
    it = iter(refs)
    x_ref, mod_ref, nmix_ref = next(it), next(it), next(it)
    wrqk_ref, wrv_ref, wrg_ref = next(it), next(it), next(it)
    wmqk_ref, wmv_ref, wmo_ref, wgt_ref = next(it), next(it), next(it), next(it)
    convw_ref, convb_ref, gbias_ref, decay_ref = next(it), next(it), next(it), next(it)
    rgn_ref, mgn_ref = next(it), next(it)
    if use_rope:
        cos_ref, sin_ref = next(it), next(it)
    if has_init:
        s0_ref, c0_ref, n0_ref, m0_ref = next(it), next(it), next(it), next(it)
    rg_out, hm_out = next(it), next(it)
    if emit_state:
        sfin_ref, cfin_ref, nfin_ref, mfin_ref = next(it), next(it), next(it), next(it)
    u_s, q_s, k_s, v_s, vt_s, kv_s, wide_s, igb_s, dec_s, stat_s, mprev_s = (next(it) for _ in range(11))

    rows_total = group_rows
    n_chunks = rows_total // CHUNK
    cps = seq_len // CHUNK
    n_seq = rows_total // seq_len
    n_rb = rows_total // ROW_BLOCK
    mod = mod_ref[0]

    def rb_rows(rb):
        return slice(rb * ROW_BLOCK, (rb + 1) * ROW_BLOCK)

    def head_cols(h, width):
        return slice(h * width, (h + 1) * width)

    for rb in range(n_rb):
        rows = rb_rows(rb)
        ub = _norm_mod(x_ref[rows, :], nmix_ref[...], mod[3:4], mod[4:5]).astype(BF16)
        u_s[rows, :] = ub
        qk = _dot(ub, wrqk_ref[...])
        q = qk[:, :RET_QK]
        k = qk[:, RET_QK:] * (DK_RET ** -0.5)
        for h in range(H_RET):
            cols = head_cols(h, DK_RET)
            qh, kh = q[:, cols], k[:, cols]
            if use_rope:
                cs, sn = cos_ref[rows, :], sin_ref[rows, :]
                qh = qh * cs + pltpu.roll(qh, DK_RET // 2, 1) * sn
                kh = kh * cs + pltpu.roll(kh, DK_RET // 2, 1) * sn
            q_s[rows, cols] = qh
            k_s[rows, cols] = kh
        v_s[rows, :] = _dot(ub, wrv_ref[...]).astype(BF16)

    pos_i = lax.broadcasted_iota(jnp.int32, (CHUNK, CHUNK), 0).astype(F32)
    pos_j = lax.broadcasted_iota(jnp.int32, (CHUNK, CHUNK), 1).astype(F32)
    lg_rows = _log_sigmoid(decay_ref[...])
    for h in range(H_RET):
        lgf = lg_rows[h:h + 1, :CHUNK]
        lgb = lg_rows[H_RET + h:H_RET + h + 1, :CHUNK]
        dec_s[h, 0] = jnp.where(pos_i > pos_j, jnp.exp((pos_i - pos_j) * lgf),
                                jnp.where(pos_i < pos_j, jnp.exp((pos_j - pos_i) * lgb), 2.0))
        dec_s[h, 1] = jnp.exp((CHUNK - 1.0 - pos_j) * lgf)
        dec_s[h, 2] = jnp.exp(pos_j * lgb)
        dec_s[h, 3] = jnp.exp((pos_i + 1.0) * lgf)
        dec_s[h, 4] = jnp.exp((CHUNK - pos_i) * lgb)
    g_chunk = jnp.exp(CHUNK * lg_rows)

    def ret_pass1(n, carry):
        rows = pl.ds(pl.multiple_of(n * CHUNK, CHUNK), CHUNK)
        for h in range(H_RET):
            k_t = k_s[rows, head_cols(h, DK_RET)].T
            kcat = jnp.concatenate([k_t * dec_s[h, 1], k_t * dec_s[h, 2]], axis=0).astype(BF16)
            kv = _dot(kcat, v_s[rows, head_cols(h, DV_RET)])
            kv_s[n, 2 * h] = kv[:DK_RET]
            kv_s[n, 2 * h + 1] = kv[DK_RET:]
        return carry

    lax.fori_loop(0, n_chunks, ret_pass1, 0, unroll=CHUNK_UNROLL)

    for g in range(n_seq):
        for h in range(H_RET):
            for d in range(N_DIR):
                slot = 2 * h + d
                decay = g_chunk[d * H_RET + h:d * H_RET + h + 1, :]
                state = s0_ref[0, 0, d, h] if has_init else jnp.zeros((DK_RET, DV_RET), F32)
                order = range(g * cps, (g + 1) * cps)
                for n in (order if d == 0 else reversed(order)):
                    local = kv_s[n, slot]
                    kv_s[n, slot] = state
                    state = decay * state + local
                if emit_state:
                    sfin_ref[g, 0, d, h] = state

    def ret_pass2(n, carry):
        rows = pl.ds(pl.multiple_of(n * CHUNK, CHUNK), CHUNK)
        for h in range(H_RET):
            qh = q_s[rows, head_cols(h, DK_RET)]
            kh = k_s[rows, head_cols(h, DK_RET)]
            vh = v_s[rows, head_cols(h, DV_RET)]
            scores = _dot_nt(qh.astype(BF16), kh.astype(BF16))
            sm = (scores * dec_s[h, 0]).astype(BF16)
            qcat = jnp.concatenate([qh * dec_s[h, 3], qh * dec_s[h, 4]], axis=1).astype(BF16)
            scat = jnp.concatenate([kv_s[n, 2 * h], kv_s[n, 2 * h + 1]], axis=0).astype(BF16)
            o = _dot(sm, vh) + _dot(qcat, scat)
            wide_s[rows, head_cols(h, DV_RET)] = _head_norm(o, rgn_ref[:, head_cols(h, DV_RET)])
        return carry

    lax.fori_loop(0, n_chunks, ret_pass2, 0, unroll=CHUNK_UNROLL)

    for rb in range(n_rb):
        rows = rb_rows(rb)
        rg = _dot(u_s[rows, :], wrg_ref[...])
        rg_out[rows, :] = (wide_s[rows, :] * _silu(rg)).astype(BF16)

    chunks_per_rb = ROW_BLOCK // CHUNK
    for rb in range(n_rb):
        rows = rb_rows(rb)
        ub = u_s[rows, :]
        wide_s[rows, :] = _dot(ub, wmqk_ref[...])
        mv = _dot(ub, wmv_ref[...])
        for ch in range(chunks_per_rb):
            for h in range(H_M):
                vt_s[rb * chunks_per_rb + ch, h] = mv[ch * CHUNK:(ch + 1) * CHUNK, head_cols(h, DH_M)].T
        gt = _dot(ub, wgt_ref[...]) + gbias_ref[...]
        glane = lax.broadcasted_iota(jnp.int32, (ROW_BLOCK, GATE_LANES), 1)
        igb_s[rows, :] = jnp.where(glane < N_PAIR, gt, _log_sigmoid(gt))

    row_id = lax.broadcasted_iota(jnp.int32, (rows_total, CHUNK), 0)
    seq_pos = row_id & (seq_len - 1)
    for cb in range(2 * M_W // CHUNK):
        cols = slice(cb * CHUNK, (cb + 1) * CHUNK)
        xc = wide_s[:, cols]
        prev = jnp.where(seq_pos == 0, 0.0, pltpu.roll(xc, 1, 0))
        nxt = jnp.where(seq_pos == seq_len - 1, 0.0, pltpu.roll(xc, rows_total - 1, 0))
        y = convb_ref[:, cols] + convw_ref[0:1, cols] * prev
        y = y + convw_ref[1:2, cols] * xc
        y = y + convw_ref[2:3, cols] * nxt
        y = _silu(y)
        if cb < M_W // CHUNK:
            q_s[:, cols] = y
        else:
            k_s[:, slice((cb - M_W // CHUNK) * CHUNK, (cb - M_W // CHUNK + 1) * CHUNK)] = y * (DH_M ** -0.5)

    z = igb_s[...]
    cpos = row_id & (CHUNK - 1)
    glane = lax.broadcasted_iota(jnp.int32, (rows_total, GATE_LANES), 1)
    pre, suf = z, z
    step = 1
    while step < CHUNK:
        pre = pre + jnp.where(cpos >= step, pltpu.roll(pre, step, 0), 0.0)
        suf = suf + jnp.where(cpos < CHUNK - step, pltpu.roll(suf, rows_total - step, 0), 0.0)
        step *= 2
    cum = jnp.where(glane < N_PAIR + H_M, pre, suf)
    i_minus_b = z - pltpu.roll(cum, GATE_LANES - N_PAIR, 1)
    igb_s[...] = jnp.where(glane < N_PAIR, z,
                           jnp.where(glane < 2 * N_PAIR, cum, pltpu.roll(i_minus_b, 2 * N_PAIR, 1)))

    def m_pass1(n, carry):
        rows = pl.ds(pl.multiple_of(n * CHUNK, CHUNK), CHUNK)
        zt = igb_s[rows, :].T
        ig_t = zt[0:N_PAIR, :]
        b_t = zt[N_PAIR:2 * N_PAIR, :]
        pair = lax.broadcasted_iota(jnp.int32, (N_PAIR, 1), 0)
        b_last = jnp.where(pair < H_M, b_t[:, CHUNK - 1:CHUNK], b_t[:, 0:1])
        g_t = (b_last - b_t) + ig_t
        m_loc = jnp.max(g_t, axis=-1, keepdims=True)
        w_t = jnp.exp(g_t - m_loc)
        stat_s[n, 0] = jnp.broadcast_to(b_last, (N_PAIR, CHUNK))
        stat_s[n, 1] = jnp.broadcast_to(m_loc, (N_PAIR, CHUNK))
        for h in range(H_M):
            kh = k_s[rows, head_cols(h, DH_M)].astype(BF16)
            v_t = vt_s[n, h]
            for d in range(N_DIR):
                c = d * H_M + h
                w_row = w_t[c:c + 1, :]
                lhs = jnp.concatenate([v_t * w_row, jnp.broadcast_to(w_row, (SUBLANES, CHUNK))], axis=0)
                local = _dot(lhs.astype(BF16), kh)
                kv_s[n, c, :, :DH_M] = local[:DH_M]
                kv_s[n, c, 0:SUBLANES, DH_M:] = local[DH_M:]
        return carry

    lax.fori_loop(0, n_chunks, m_pass1, 0, unroll=CHUNK_UNROLL)

    for g in range(n_seq):
        for d in range(N_DIR):
            order = list(range(g * cps, (g + 1) * cps))
            if d == 1:
                order.reverse()
            m_prev = m0_ref[0] if has_init else jnp.zeros((N_PAIR, CHUNK), F32)
            a1s, a2s = [], []
            for n in order:
                b_last, m_loc = stat_s[n, 0], stat_s[n, 1]
                m_new = jnp.maximum(b_last + m_prev, m_loc)
                a1s.append(jnp.exp(b_last + m_prev - m_new))
                a2s.append(jnp.exp(m_loc - m_new))
                mprev_s[n, d] = m_prev
                m_prev = m_new
            if emit_state:
                mfin_ref[g, d] = m_prev
            for h in range(H_M):
                c = d * H_M + h
                if has_init:
                    mem = c0_ref[0, 0, d, h]
                    nrm = jnp.broadcast_to(n0_ref[0][c:c + 1, :], (SUBLANES, DH_M))
                else:
                    mem = jnp.zeros((DH_M, DH_M), F32)
                    nrm = jnp.zeros((SUBLANES, DH_M), F32)
                for idx, n in enumerate(order):
                    a1 = a1s[idx][c:c + 1, :]
                    a2 = a2s[idx][c:c + 1, :]
                    local_mem = kv_s[n, c, :, :DH_M]
                    local_nrm = kv_s[n, c, 0:SUBLANES, DH_M:]
                    kv_s[n, c, :, :DH_M] = mem
                    kv_s[n, c, 0:SUBLANES, DH_M:] = nrm
                    mem = a1 * mem + a2 * local_mem
                    nrm = a1 * nrm + a2 * local_nrm
                if emit_state:
                    cfin_ref[g, 0, d, h] = mem
                    nfin_ref[g, c:c + 1, :] = nrm[0:1]

    tri_j = lax.broadcasted_iota(jnp.int32, (CHUNK, CHUNK), 0)
    tri_i = lax.broadcasted_iota(jnp.int32, (CHUNK, CHUNK), 1)

    def m_pass2(n, carry):
        rows = pl.ds(pl.multiple_of(n * CHUNK, CHUNK), CHUNK)
        zc = igb_s[rows, :]
        zt = zc.T
        for h in range(H_M):
            qh = q_s[rows, head_cols(h, DH_M)].astype(BF16)
            kh = k_s[rows, head_cols(h, DH_M)].astype(BF16)
            v_t = vt_s[n, h].astype(BF16)
            scores_t = _dot_nt(kh, qh)
            h_sum_t = jnp.zeros((DH_M, CHUNK), F32)
            for d in range(N_DIR):
                c = d * H_M + h
                a_col = zc[:, 2 * N_PAIR + c:2 * N_PAIR + c + 1]
                b_row = zt[N_PAIR + c:N_PAIR + c + 1, :]
                mask = (tri_j <= tri_i) if d == 0 else (tri_j >= tri_i)
                dlog = jnp.where(mask, b_row + a_col, -jnp.inf)
                lin = b_row + mprev_s[n, d][c:c + 1, :]
                m_i = jnp.maximum(jnp.max(dlog, axis=0, keepdims=True), lin)
                s_t = scores_t * jnp.exp(dlog - m_i)
                w_int = jnp.exp(lin - m_i)
                mem = jnp.concatenate([kv_s[n, c, :, :DH_M], kv_s[n, c, 0:SUBLANES, DH_M:]], axis=0)
                cross = _dot_nt(mem.astype(BF16), qh)
                num = _dot(v_t, s_t.astype(BF16)) + cross[:DH_M] * w_int
                den = jnp.sum(s_t, axis=0, keepdims=True) + cross[DH_M:DH_M + 1] * w_int
                denom = jnp.maximum(jnp.abs(den), jnp.exp(-m_i))
                h_sum_t = h_sum_t + num * (1.0 / denom)
            wide_s[rows, head_cols(h, DH_M)] = h_sum_t.T
        return carry

    lax.fori_loop(0, n_chunks, m_pass2, 0, unroll=CHUNK_UNROLL)

    for rb in range(n_rb):
        rows = rb_rows(rb)
        mo = _dot(u_s[rows, :], wmo_ref[...])
        hm = _sigmoid(mo) * wide_s[rows, :M_W]
        for h in range(H_M):
            cols = head_cols(h, DH_M)
            hm_out[rows, cols] = _head_norm(hm[:, cols], mgn_ref[:, cols]).astype(BF16)


def _mixer_core_call(x, mod, mod_row, p, seq_len, group_rows, rope, init):
    t = x.shape[0]
    n_groups = t // group_rows
    n_chunks = group_rows // CHUNK
    seq_per_group = group_rows // seq_len
    use_rope = rope is not None
    has_init = init is not None
    emit_state = not has_init

    inputs = [x, mod, p["norm_mix"], p["w_rqk"], p["w_rv"], p["w_rg"], p["w_mqk"], p["w_mv"], p["w_mo"],
              p["w_gates"], p["conv_w"], p["conv_b"], p["gate_bias"], p["decay_rows"], p["ret_gn"], p["m_gn"]]
    in_specs = [
        pl.BlockSpec((group_rows, D_MODEL), lambda i: (i, 0)),
        pl.BlockSpec((1, N_MOD, D_MODEL), lambda i: (mod_row(i * group_rows), 0, 0)),
    ] + [_const_spec(a.shape) for a in inputs[2:]]
    if use_rope:
        inputs += [rope[0], rope[1]]
        in_specs += [_const_spec(rope[0].shape), _const_spec(rope[1].shape)]
    if has_init:
        s0, c0, n0, m0 = init
        inputs += [s0, c0, n0, m0]
        in_specs += [
            pl.BlockSpec((1, 1, N_DIR, H_RET, DK_RET, DV_RET), lambda i: (i, 0, 0, 0, 0, 0)),
            pl.BlockSpec((1, 1, N_DIR, H_M, DH_M, DH_M), lambda i: (i, 0, 0, 0, 0, 0)),
            pl.BlockSpec((1, N_PAIR, DH_M), lambda i: (i, 0, 0)),
            pl.BlockSpec((1, N_PAIR, CHUNK), lambda i: (i, 0, 0)),
        ]

    out_shape = [jax.ShapeDtypeStruct((t, RET_V), BF16), jax.ShapeDtypeStruct((t, M_W), BF16)]
    out_specs = [pl.BlockSpec((group_rows, RET_V), lambda i: (i, 0)),
                 pl.BlockSpec((group_rows, M_W), lambda i: (i, 0))]
    if emit_state:
        n_seq = t // seq_len
        out_shape += [
            jax.ShapeDtypeStruct((n_seq, 1, N_DIR, H_RET, DK_RET, DV_RET), F32),
            jax.ShapeDtypeStruct((n_seq, 1, N_DIR, H_M, DH_M, DH_M), F32),
            jax.ShapeDtypeStruct((n_seq, N_PAIR, DH_M), F32),
            jax.ShapeDtypeStruct((n_seq, N_DIR, N_PAIR, CHUNK), F32),
        ]
        out_specs += [
            pl.BlockSpec((seq_per_group, 1, N_DIR, H_RET, DK_RET, DV_RET), lambda i: (i, 0, 0, 0, 0, 0)),
            pl.BlockSpec((seq_per_group, 1, N_DIR, H_M, DH_M, DH_M), lambda i: (i, 0, 0, 0, 0, 0)),
            pl.BlockSpec((seq_per_group, N_PAIR, DH_M), lambda i: (i, 0, 0)),
            pl.BlockSpec((seq_per_group, N_DIR, N_PAIR, CHUNK), lambda i: (i, 0, 0, 0)),
        ]

    scratch = [
        pltpu.VMEM((group_rows, D_MODEL), BF16),
        pltpu.VMEM((group_rows, RET_QK), F32),
        pltpu.VMEM((group_rows, RET_QK), F32),
        pltpu.VMEM((group_rows, RET_V), BF16),
        pltpu.VMEM((n_chunks, H_M, DH_M, CHUNK), F32),
        pltpu.VMEM((n_chunks, N_PAIR, DK_RET, DV_RET), F32),
        pltpu.VMEM((group_rows, RET_V), F32),
        pltpu.VMEM((group_rows, GATE_LANES), F32),
        pltpu.VMEM((H_RET, 5, CHUNK, CHUNK), F32),
        pltpu.VMEM((n_chunks, 2, N_PAIR, CHUNK), F32),
        pltpu.VMEM((n_chunks, N_DIR, N_PAIR, CHUNK), F32),
    ]
    kern = functools.partial(_mixer_core_kernel, group_rows=group_rows, seq_len=seq_len, use_rope=use_rope,
                             has_init=has_init, emit_state=emit_state)
    return pl.pallas_call(
        kern,
        out_shape=out_shape,
        grid=(n_groups,),
        in_specs=in_specs,
        out_specs=out_specs,
        scratch_shapes=scratch,
        compiler_params=pltpu.CompilerParams(
            dimension_semantics=("arbitrary",), vmem_limit_bytes=VMEM_LIMIT),
        name="mixer_core",
    )(*inputs)


def _mixer_out_kernel(x_ref, rg_ref, hm_ref, mod_ref, nmix_ref, wbg_ref, wru_ref, wmu_ref, wout_ref,
                      nffn_ref, w1_ref, w3_ref, w2_ref, nfin_ref, o_ref):
    mod = mod_ref[0]
    x = x_ref[...]
    u = _norm_mod(x, nmix_ref[...], mod[3:4], mod[4:5]).astype(BF16)
    gates = _sigmoid(_dot(u, wbg_ref[...]))
    ret_branch = _dot(rg_ref[...], wru_ref[...])
    m_branch = _dot(hm_ref[...], wmu_ref[...])
    merged = (gates[:, :D_MODEL] * ret_branch + gates[:, D_MODEL:] * m_branch).astype(BF16)
    x = x + mod[5:6] * _dot(merged, wout_ref[...])
    x = _swiglu_half_step(x, mod, 6, nffn_ref[...], w1_ref, w3_ref, w2_ref)
    ms = jnp.mean(x * x, axis=-1, keepdims=True)
    o_ref[...] = x * lax.rsqrt(ms + EPS) * nfin_ref[...]


def _mixer_out_call(x, rgated, hmn, mod, mod_row, p):
    t = x.shape[0]
    tm = FFN_TILE
    consts = [p["norm_mix"], p["w_bg"], p["w_ret_up"], p["w_m_up"], p["w_out"],
              p["norm_ffn2"], p["w1_ffn2"], p["w3_ffn2"], p["w2_ffn2"], p["norm_final"]]
    return pl.pallas_call(
        _mixer_out_kernel,
        out_shape=jax.ShapeDtypeStruct((t, D_MODEL), F32),
        grid=(t // tm,),
        in_specs=[
            pl.BlockSpec((tm, D_MODEL), lambda i: (i, 0)),
            pl.BlockSpec((tm, RET_V), lambda i: (i, 0)),
            pl.BlockSpec((tm, M_W), lambda i: (i, 0)),
            pl.BlockSpec((1, N_MOD, D_MODEL), lambda i: (mod_row(i * tm), 0, 0)),
        ] + [_const_spec(a.shape) for a in consts],
        out_specs=pl.BlockSpec((tm, D_MODEL), lambda i: (i, 0)),
        compiler_params=pltpu.CompilerParams(
            dimension_semantics=("arbitrary",), vmem_limit_bytes=VMEM_LIMIT),
        name="mixer_out",
    )(x, rgated, hmn, mod, *consts)


def _rope_tables(seq_len):
    rows = seq_len // GRID_W
    r = jnp.repeat(jnp.arange(rows, dtype=F32), GRID_W)
    col = (jnp.arange(seq_len) % GRID_W).astype(F32)
    n_f = DK_RET // 4
    freqs = ROPE_BASE ** (-jnp.arange(n_f, dtype=F32) / n_f)
    ang = jnp.concatenate([r[:, None] * freqs, col[:, None] * freqs], axis=-1)
    cos, sin = jnp.cos(ang), jnp.sin(ang)
    return jnp.concatenate([cos, cos], axis=-1), jnp.concatenate([-sin, sin], axis=-1)


def _layer_params(l, norm_ffn1, w1_ffn1, w3_ffn1, w2_ffn1, norm_mix, w_in, conv_w, conv_b,
                  ret_decay_logit, b_igate, b_fgate, ret_gn, m_gn, w_ret_up, w_m_up, w_out,
                  norm_ffn2, w1_ffn2, w3_ffn2, w2_ffn2, norm_final):
    w = w_in[l]
    o = 0
    cols = {}
    for name, width in (("w_rqk", 2 * RET_QK), ("w_rv", RET_V), ("w_rg", RET_V), ("w_mqk", 2 * M_W),
                        ("w_mv", M_W), ("w_mo", M_W), ("w_gates", 2 * N_PAIR), ("w_bg", N_DIR * D_MODEL)):
        cols[name] = w[:, o:o + width].astype(BF16)
        o += width
    cols["w_gates"] = jnp.pad(cols["w_gates"], ((0, 0), (0, GATE_LANES - 2 * N_PAIR)))
    gate_bias = jnp.pad(jnp.concatenate([b_igate[l], b_fgate[l]]), (0, GATE_LANES - 2 * N_PAIR))
    p = dict(cols)
    p.update(
        norm_ffn1=norm_ffn1[l][None, :], norm_mix=norm_mix[l][None, :], norm_ffn2=norm_ffn2[l][None, :],
        norm_final=norm_final[None, :],
        w1_ffn1=w1_ffn1[l].astype(BF16), w3_ffn1=w3_ffn1[l].astype(BF16), w2_ffn1=w2_ffn1[l].astype(BF16),
        w1_ffn2=w1_ffn2[l].astype(BF16), w3_ffn2=w3_ffn2[l].astype(BF16), w2_ffn2=w2_ffn2[l].astype(BF16),
        w_ret_up=w_ret_up[l].astype(BF16), w_m_up=w_m_up[l].astype(BF16), w_out=w_out[l].astype(BF16),
        conv_w=conv_w[l], conv_b=conv_b[l][None, :],
        gate_bias=gate_bias[None, :],
        decay_rows=jnp.broadcast_to(ret_decay_logit[l].reshape(N_DIR * H_RET, 1), (N_DIR * H_RET, DV_RET)),
        ret_gn=ret_gn[l][None, :], m_gn=m_gn[l][None, :],
    )
    return p


def kernel(x_prompt, x_sample, c, state_ret, state_mlstm_C, state_mlstm_n, state_mlstm_m, c_ctx, w_ada, b_ada, norm_ffn1, w1_ffn1, w3_ffn1, w2_ffn1, norm_mix, w_in, conv_w, conv_b, ret_decay_logit, b_igate, b_fgate, ret_gn, m_gn, w_ret_up, w_m_up, w_out, norm_ffn2, w1_ffn2, w3_ffn2, w2_ffn2, norm_final):
    bp, lp, _ = x_prompt.shape
    bs, ls, _ = x_sample.shape
    depth = w_ada.shape[0]
    assert depth == 1 and lp % CHUNK == 0 and PROMPT_GROUP_ROWS % lp == 0 and ls <= MAX_GROUP_ROWS
    assert lp & (lp - 1) == 0 and ls & (ls - 1) == 0

    ctx_row = 8 * ((bs + 7) // 8)
    cvec = jnp.zeros((ctx_row + 8, D_MODEL), F32).at[:bs].set(c).at[ctx_row].set(c_ctx)
    rope = _rope_tables(ls)

    xp = x_prompt.reshape(bp * lp, D_MODEL)
    xs = x_sample.reshape(bs * ls, D_MODEL)
    prompt_row = lambda r: ctx_row
    sample_row = lambda r: r // ls

    l = 0
    p = _layer_params(l, norm_ffn1, w1_ffn1, w3_ffn1, w2_ffn1, norm_mix, w_in, conv_w, conv_b,
                      ret_decay_logit, b_igate, b_fgate, ret_gn, m_gn, w_ret_up, w_m_up, w_out,
                      norm_ffn2, w1_ffn2, w3_ffn2, w2_ffn2, norm_final)
    mod = _ada_call(cvec, w_ada[l], b_ada[l][None, :]).reshape(ctx_row + 8, N_MOD, D_MODEL)

    n0 = state_mlstm_n[:, l].astype(F32).reshape(bs, N_PAIR, DH_M)
    m0 = jnp.broadcast_to(state_mlstm_m[:, l].astype(F32).reshape(bs, N_PAIR, 1), (bs, N_PAIR, CHUNK))
    init = (state_ret.astype(F32), state_mlstm_C.astype(F32), n0, m0)

    xp1 = _ffn_call(xp, mod, prompt_row, p["norm_ffn1"], p["w1_ffn1"], p["w3_ffn1"], p["w2_ffn1"])
    xs1 = _ffn_call(xs, mod, sample_row, p["norm_ffn1"], p["w1_ffn1"], p["w3_ffn1"], p["w2_ffn1"])

    rg_p, hm_p, s_fin, c_fin, n_fin, m_fin = _mixer_core_call(xp1, mod, prompt_row, p, lp, PROMPT_GROUP_ROWS, None, None)
    rg_s, hm_s = _mixer_core_call(xs1, mod, sample_row, p, ls, ls, rope, init)

    yp = _mixer_out_call(xp1, rg_p, hm_p, mod, prompt_row, p)
    ys = _mixer_out_call(xs1, rg_s, hm_s, mod, sample_row, p)

    dt = x_prompt.dtype
    new_c = c_fin
    new_n = n_fin.reshape(bp, 1, N_DIR, H_M, DH_M)
    m_diag = jnp.stack([m_fin[:, d, d * H_M:(d + 1) * H_M, 0] for d in range(N_DIR)], axis=1)
    new_m = m_diag.reshape(bp, 1, N_DIR, H_M)
    return (yp.reshape(bp, lp, D_MODEL).astype(dt), ys.reshape(bs, ls, D_MODEL).astype(dt),
            s_fin.astype(dt), new_c.astype(dt), new_n.astype(dt), new_m.astype(dt))
```

```python
import functools

import jax
import jax.numpy as jnp
from jax import lax
from jax.experimental import pallas as pl
from jax.experimental.pallas import tpu as pltpu

F32 = jnp.float32
BF16 = jnp.bfloat16

D_MODEL = 1024
D_FF = 2816
CHUNK = 128
N_DIR = 2
H_RET = 4
DK_RET = 128
DV_RET = 256
H_M = 4
DH_M = 128
RET_QK = H_RET * DK_RET
RET_V = H_RET * DV_RET
M_W = H_M * DH_M
GRID_W = 64
ROPE_BASE = 10000.0
EPS = 1e-6
GN_EPS = 1e-5
N_MOD = 9
N_PAIR = N_DIR * H_M
GATE_LANES = 128
NORM_ROWS = 16

MAX_GROUP_ROWS = 1024
PROMPT_GROUP_ROWS = 512
ROW_BLOCK = 256
CHUNK_UNROLL = 2
FFN_TILE = 512
VMEM_LIMIT = 60 * 1024 * 1024


def _sigmoid(x):
    return 1.0 / (1.0 + jnp.exp(-x))


def _silu(x):
    return x * _sigmoid(x)


def _log_sigmoid(x):
    return -(jnp.maximum(-x, 0.0) + jnp.log1p(jnp.exp(-jnp.abs(x))))


def _norm_mod(x, gain, shift, scale):
    ms = jnp.mean(x * x, axis=-1, keepdims=True)
    y = x * lax.rsqrt(ms + EPS) * gain
    return y * (1.0 + scale) + shift


def _dot(a, b):
    return jnp.dot(a, b, preferred_element_type=F32)


def _dot_nt(a, b):
    return lax.dot_general(a, b, (((1,), (1,)), ((), ())), preferred_element_type=F32)


def _head_norm(o, gain):
    mu = jnp.mean(o, axis=-1, keepdims=True)
    oc = o - mu
    var = jnp.mean(oc * oc, axis=-1, keepdims=True)
    return oc * lax.rsqrt(var + GN_EPS) * gain


def _ada_kernel(c_ref, w_ref, b_ref, o_ref):
    s = _silu(c_ref[...]).astype(BF16)
    o_ref[...] = _dot(s, w_ref[...].astype(BF16)) + b_ref[...]


def _ada_call(cvec, w_ada, b_ada):
    rows = cvec.shape[0]
    n_out = w_ada.shape[1]
    tile = D_MODEL
    return pl.pallas_call(
        _ada_kernel,
        out_shape=jax.ShapeDtypeStruct((rows, n_out), F32),
        grid=(n_out // tile,),
        in_specs=[
            pl.BlockSpec((rows, D_MODEL), lambda j: (0, 0)),
            pl.BlockSpec((D_MODEL, tile), lambda j: (0, j)),
            pl.BlockSpec((1, tile), lambda j: (0, j)),
        ],
        out_specs=pl.BlockSpec((rows, tile), lambda j: (0, j)),
        compiler_params=pltpu.CompilerParams(
            dimension_semantics=("arbitrary",), vmem_limit_bytes=VMEM_LIMIT),
        name="ada",
    )(cvec, w_ada, b_ada)


def _swiglu_half_step(x, mod, k0, gain, w1_ref, w3_ref, w2_ref):
    u = _norm_mod(x, gain, mod[k0:k0 + 1], mod[k0 + 1:k0 + 2]).astype(BF16)
    h1 = _dot(u, w1_ref[...])
    h3 = _dot(u, w3_ref[...])
    h = (_silu(h1) * h3).astype(BF16)
    y = _dot(h, w2_ref[...])
    return x + 0.5 * mod[k0 + 2:k0 + 3] * y


def _ffn_kernel(x_ref, mod_ref, g_ref, w1_ref, w3_ref, w2_ref, o_ref):
    o_ref[...] = _swiglu_half_step(x_ref[...], mod_ref[0], 0, g_ref[...], w1_ref, w3_ref, w2_ref)


def _const_spec(shape):
    nd = len(shape)
    return pl.BlockSpec(shape, lambda i: (0,) * nd)


def _ffn_call(x, mod, mod_row, gain, w1, w3, w2):
    t = x.shape[0]
    tm = FFN_TILE
    return pl.pallas_call(
        _ffn_kernel,
        out_shape=jax.ShapeDtypeStruct((t, D_MODEL), F32),
        grid=(t // tm,),
        in_specs=[
            pl.BlockSpec((tm, D_MODEL), lambda i: (i, 0)),
            pl.BlockSpec((1, N_MOD, D_MODEL), lambda i: (mod_row(i * tm), 0, 0)),
            _const_spec((1, D_MODEL)),
            _const_spec((D_MODEL, D_FF)),
            _const_spec((D_MODEL, D_FF)),
            _const_spec((D_FF, D_MODEL)),
        ],
        out_specs=pl.BlockSpec((tm, D_MODEL), lambda i: (i, 0)),
        compiler_params=pltpu.CompilerParams(
            dimension_semantics=("arbitrary",), vmem_limit_bytes=VMEM_LIMIT),
        name="ffn",
    )(x, mod, gain, w1, w3, w2)


def _mixer_core_kernel(*refs, group_rows, seq_len, use_rope, has_init, emit_state):
    it = iter(refs)
    x_ref, mod_ref, nmix_ref = next(it), next(it), next(it)
    wrqk_ref, wrv_ref, wrg_ref = next(it), next(it), next(it)
    wmqk_ref, wmv_ref, wmo_ref, wgt_ref = next(it), next(it), next(it), next(it)
    convw_ref, convb_ref, gbias_ref, decay_ref = next(it), next(it), next(it), next(it)
    rgn_ref, mgn_ref = next(it), next(it)
    if use_rope:
        cos_ref, sin_ref = next(it), next(it)
    if has_init:
        s0_ref, c0_ref, n0_ref, m0_ref = next(it), next(it), next(it), next(it)
    rg_out, hm_out = next(it), next(it)
    if emit_state:
        sfin_ref, cfin_ref, nfin_ref, mfin_ref = next(it), next(it), next(it), next(it)
    (u_s, q_s, k_s, v_s, t_s, qt_s, kv_s, wide_s, igb_s, zt_s, wt_s, dec_s, stat_s,
     mprev_s) = (next(it) for _ in range(14))

    rows_total = group_rows
    n_chunks = rows_total // CHUNK
    cps = seq_len // CHUNK
    n_seq = rows_total // seq_len
    n_rb = rows_total // ROW_BLOCK
    mod = mod_ref[0]

    def rb_rows(rb):
        return slice(rb * ROW_BLOCK, (rb + 1) * ROW_BLOCK)

    def head_cols(h, width):
        return slice(h * width, (h + 1) * width)

    chunks_per_rb = ROW_BLOCK // CHUNK
    for rb in range(n_rb):
        rows = rb_rows(rb)
        ub = _norm_mod(x_ref[rows, :], nmix_ref[...], mod[3:4], mod[4:5]).astype(BF16)
        u_s[rows, :] = ub
        qk = _dot(ub, wrqk_ref[...])
        q = qk[:, :RET_QK]
        k = qk[:, RET_QK:] * (DK_RET ** -0.5)
        for h in range(H_RET):
            cols = head_cols(h, DK_RET)
            qh, kh = q[:, cols], k[:, cols]
            if use_rope:
                cs, sn = cos_ref[rows, :], sin_ref[rows, :]
                qh = qh * cs + pltpu.roll(qh, DK_RET // 2, 1) * sn
                kh = kh * cs + pltpu.roll(kh, DK_RET // 2, 1) * sn
            q_s[rows, cols] = qh
            for ch in range(chunks_per_rb):
                t_s[rb * chunks_per_rb + ch, h] = kh[ch * CHUNK:(ch + 1) * CHUNK, :].T
        v_s[rows, :] = _dot(ub, wrv_ref[...]).astype(BF16)

    pos_i = lax.broadcasted_iota(jnp.int32, (CHUNK, CHUNK), 0).astype(F32)
    pos_j = lax.broadcasted_iota(jnp.int32, (CHUNK, CHUNK), 1).astype(F32)
    lg_rows = _log_sigmoid(decay_ref[...])
    for h in range(H_RET):
        lgf = lg_rows[h:h + 1, :CHUNK]
        lgb = lg_rows[H_RET + h:H_RET + h + 1, :CHUNK]
        dec_s[h, 0] = jnp.where(pos_i > pos_j, jnp.exp((pos_i - pos_j) * lgf),
                                jnp.where(pos_i < pos_j, jnp.exp((pos_j - pos_i) * lgb), 2.0))
        dec_s[h, 1] = jnp.exp((CHUNK - 1.0 - pos_j) * lgf)
        dec_s[h, 2] = jnp.exp(pos_j * lgb)
        dec_s[h, 3] = jnp.exp((pos_i + 1.0) * lgf)
        dec_s[h, 4] = jnp.exp((CHUNK - pos_i) * lgb)
    g_chunk = jnp.exp(CHUNK * lg_rows)

    def ret_pass1(n, carry):
        rows = pl.ds(pl.multiple_of(n * CHUNK, CHUNK), CHUNK)
        for h in range(H_RET):
            k_t = t_s[n, h]
            kcat = jnp.concatenate([k_t * dec_s[h, 1], k_t * dec_s[h, 2]], axis=0).astype(BF16)
            kv = _dot(kcat, v_s[rows, head_cols(h, DV_RET)])
            kv_s[n, 2 * h] = kv[:DK_RET]
            kv_s[n, 2 * h + 1] = kv[DK_RET:]
        return carry

    lax.fori_loop(0, n_chunks, ret_pass1, 0, unroll=CHUNK_UNROLL)

    for g in range(n_seq):
        for h in range(H_RET):
            for d in range(N_DIR):
                slot = 2 * h + d
                decay = g_chunk[d * H_RET + h:d * H_RET + h + 1, :]
                state = s0_ref[0, 0, d, h] if has_init else jnp.zeros((DK_RET, DV_RET), F32)
                order = range(g * cps, (g + 1) * cps)
                for n in (order if d == 0 else reversed(order)):
                    local = kv_s[n, slot]
                    kv_s[n, slot] = state
                    state = decay * state + local
                if emit_state:
                    sfin_ref[g, 0, d, h] = state

    def ret_pass2(n, carry):
        rows = pl.ds(pl.multiple_of(n * CHUNK, CHUNK), CHUNK)
        for h in range(H_RET):
            qh = q_s[rows, head_cols(h, DK_RET)]
            vh = v_s[rows, head_cols(h, DV_RET)]
            scores = _dot(qh.astype(BF16), t_s[n, h].astype(BF16))
            sm = (scores * dec_s[h, 0]).astype(BF16)
            qcat = jnp.concatenate([qh * dec_s[h, 3], qh * dec_s[h, 4]], axis=1).astype(BF16)
            scat = jnp.concatenate([kv_s[n, 2 * h], kv_s[n, 2 * h + 1]], axis=0).astype(BF16)
            o = _dot(sm, vh) + _dot(qcat, scat)
            wide_s[rows, head_cols(h, DV_RET)] = _head_norm(o, rgn_ref[:, head_cols(h, DV_RET)])
        return carry

    lax.fori_loop(0, n_chunks, ret_pass2, 0, unroll=CHUNK_UNROLL)

    for rb in range(n_rb):
        rows = rb_rows(rb)
        rg = _dot(u_s[rows, :], wrg_ref[...])
        rg_out[rows, :] = (wide_s[rows, :] * _silu(rg)).astype(BF16)

    for rb in range(n_rb):
        rows = rb_rows(rb)
        ub = u_s[rows, :]
        wide_s[rows, :] = _dot(ub, wmqk_ref[...])
        mv = _dot(ub, wmv_ref[...])
        for ch in range(chunks_per_rb):
            for h in range(H_M):
                t_s[rb * chunks_per_rb + ch, h] = mv[ch * CHUNK:(ch + 1) * CHUNK, head_cols(h, DH_M)].T
        gt = _dot(ub, wgt_ref[...]) + gbias_ref[...]
        glane = lax.broadcasted_iota(jnp.int32, (ROW_BLOCK, GATE_LANES), 1)
        igb_s[rows, :] = jnp.where(glane < N_PAIR, gt, _log_sigmoid(gt))

    row_id = lax.broadcasted_iota(jnp.int32, (rows_total, CHUNK), 0)
    seq_pos = row_id & (seq_len - 1)
    for cb in range(2 * M_W // CHUNK):
        cols = slice(cb * CHUNK, (cb + 1) * CHUNK)
        xc = wide_s[:, cols]
        prev = jnp.where(seq_pos == 0, 0.0, pltpu.roll(xc, 1, 0))
        nxt = jnp.where(seq_pos == seq_len - 1, 0.0, pltpu.roll(xc, rows_total - 1, 0))
        y = convb_ref[:, cols] + convw_ref[0:1, cols] * prev
        y = y + convw_ref[1:2, cols] * xc
        y = y + convw_ref[2:3, cols] * nxt
        y = _silu(y)
        if cb < H_M:
            for n in range(n_chunks):
                qt_s[n, cb] = y[n * CHUNK:(n + 1) * CHUNK, :].T.astype(BF16)
        else:
            k_s[:, head_cols(cb - H_M, DH_M)] = y * (DH_M ** -0.5)

    z = igb_s[...]
    cpos = row_id & (CHUNK - 1)
    glane = lax.broadcasted_iota(jnp.int32, (rows_total, GATE_LANES), 1)
    pre, suf = z, z
    step = 1
    while step < CHUNK:
        pre = pre + jnp.where(cpos >= step, pltpu.roll(pre, step, 0), 0.0)
        suf = suf + jnp.where(cpos < CHUNK - step, pltpu.roll(suf, rows_total - step, 0), 0.0)
        step *= 2
    cum = jnp.where(glane < N_PAIR + H_M, pre, suf)
    i_minus_b = z - pltpu.roll(cum, GATE_LANES - N_PAIR, 1)
    igb = jnp.where(glane < N_PAIR, z,
                    jnp.where(glane < 2 * N_PAIR, cum, pltpu.roll(i_minus_b, 2 * N_PAIR, 1)))
    igb_s[...] = igb

    for n in range(n_chunks):
        zt_s[n] = igb[n * CHUNK:(n + 1) * CHUNK, :].T[0:3 * N_PAIR, :]
    ig_all = zt_s[:, 0:N_PAIR, :]
    b_all = zt_s[:, N_PAIR:2 * N_PAIR, :]
    pair = lax.broadcasted_iota(jnp.int32, (n_chunks, N_PAIR, 1), 1)
    b_last = jnp.where(pair < H_M, b_all[:, :, CHUNK - 1:CHUNK], b_all[:, :, 0:1])
    g_all = (b_last - b_all) + ig_all
    m_loc = jnp.max(g_all, axis=-1, keepdims=True)
    wt_s[...] = jnp.exp(g_all - m_loc)
    stat_s[:, 0] = jnp.broadcast_to(b_last, (n_chunks, N_PAIR, CHUNK))
    stat_s[:, 1] = jnp.broadcast_to(m_loc, (n_chunks, N_PAIR, CHUNK))

    def m_pass1(n, carry):
        rows = pl.ds(pl.multiple_of(n * CHUNK, CHUNK), CHUNK)
        w_t = wt_s[n]
        for h in range(H_M):
            kh = k_s[rows, head_cols(h, DH_M)].astype(BF16)
            v_t = t_s[n, h]
            parts = []
            for d in range(N_DIR):
                w_row = w_t[d * H_M + h:d * H_M + h + 1, :]
                parts += [v_t * w_row, jnp.broadcast_to(w_row, (NORM_ROWS, CHUNK))]
            local = _dot(jnp.concatenate(parts, axis=0).astype(BF16), kh)
            for d in range(N_DIR):
                c = d * H_M + h
                base = d * (DH_M + NORM_ROWS)
                kv_s[n, c, :, :DH_M] = local[base:base + DH_M]
                kv_s[n, c, 0:NORM_ROWS, DH_M:] = local[base + DH_M:base + DH_M + NORM_ROWS]
        return carry

    lax.fori_loop(0, n_chunks, m_pass1, 0, unroll=CHUNK_UNROLL)

    for g in range(n_seq):
        for d in range(N_DIR):
            order = list(range(g * cps, (g + 1) * cps))
            if d == 1:
                order.reverse()
            m_prev = m0_ref[0] if has_init else jnp.zeros((N_PAIR, CHUNK), F32)
            a1s, a2s = [], []
            for n in order:
                b_last_n, m_loc_n = stat_s[n, 0], stat_s[n, 1]
                m_new = jnp.maximum(b_last_n + m_prev, m_loc_n)
                a1s.append(jnp.exp(b_last_n + m_prev - m_new))
                a2s.append(jnp.exp(m_loc_n - m_new))
                mprev_s[n, d] = m_prev
                m_prev = m_new
            if emit_state:
                mfin_ref[g, d] = m_prev
            for h in range(H_M):
                c = d * H_M + h
                if has_init:
                    mem = c0_ref[0, 0, d, h]
                    nrm = jnp.broadcast_to(n0_ref[0][c:c + 1, :], (NORM_ROWS, DH_M))
                else:
                    mem = jnp.zeros((DH_M, DH_M), F32)
                    nrm = jnp.zeros((NORM_ROWS, DH_M), F32)
                for idx, n in enumerate(order):
                    a1 = a1s[idx][c:c + 1, :]
                    a2 = a2s[idx][c:c + 1, :]
                    local_mem = kv_s[n, c, :, :DH_M]
                    local_nrm = kv_s[n, c, 0:NORM_ROWS, DH_M:]
                    kv_s[n, c, :, :DH_M] = mem
                    kv_s[n, c, 0:NORM_ROWS, DH_M:] = nrm
                    mem = a1 * mem + a2 * local_mem
                    nrm = a1 * nrm + a2 * local_nrm
                if emit_state:
                    cfin_ref[g, 0, d, h] = mem
                    nfin_ref[g, c:c + 1, :] = nrm[0:1]

    tri_j = lax.broadcasted_iota(jnp.int32, (CHUNK, CHUNK), 0)
    tri_i = lax.broadcasted_iota(jnp.int32, (CHUNK, CHUNK), 1)
    mem_rows = DH_M + NORM_ROWS

    def m_pass2(n, carry):
        rows = pl.ds(pl.multiple_of(n * CHUNK, CHUNK), CHUNK)
        zc = igb_s[rows, :]
        zt = zt_s[n]
        for h in range(H_M):
            kh = k_s[rows, head_cols(h, DH_M)].astype(BF16)
            mems = [jnp.concatenate([kv_s[n, d * H_M + h, :, :DH_M], kv_s[n, d * H_M + h, 0:NORM_ROWS, DH_M:]],
                                    axis=0).astype(BF16) for d in range(N_DIR)]
            big = _dot(jnp.concatenate([kh] + mems, axis=0), qt_s[n, h])
            scores_t = big[:CHUNK]
            weights, crosses, w_ints, m_is = [], [], [], []
            for d in range(N_DIR):
                c = d * H_M + h
                a_col = zc[:, 2 * N_PAIR + c:2 * N_PAIR + c + 1]
                b_row = zt[N_PAIR + c:N_PAIR + c + 1, :]
                mask = (tri_j <= tri_i) if d == 0 else (tri_j >= tri_i)
                dlog = jnp.where(mask, b_row + a_col, -jnp.inf)
                lin = b_row + mprev_s[n, d][c:c + 1, :]
                m_i = jnp.maximum(jnp.max(dlog, axis=0, keepdims=True), lin)
                weights.append(scores_t * jnp.exp(dlog - m_i))
                crosses.append(big[CHUNK + d * mem_rows:CHUNK + (d + 1) * mem_rows])
                w_ints.append(jnp.exp(lin - m_i))
                m_is.append(m_i)
            intra = _dot(t_s[n, h].astype(BF16), jnp.concatenate(weights, axis=1).astype(BF16))
            h_sum_t = jnp.zeros((DH_M, CHUNK), F32)
            for d in range(N_DIR):
                cross, w_int = crosses[d], w_ints[d]
                num = intra[:, d * CHUNK:(d + 1) * CHUNK] + cross[:DH_M] * w_int
                den = jnp.sum(weights[d], axis=0, keepdims=True) + cross[DH_M:DH_M + 1] * w_int
                denom = jnp.maximum(jnp.abs(den), jnp.exp(-m_is[d]))
                h_sum_t = h_sum_t + num * (1.0 / denom)
            wide_s[rows, head_cols(h, DH_M)] = h_sum_t.T
        return carry

    lax.fori_loop(0, n_chunks, m_pass2, 0, unroll=CHUNK_UNROLL)

    for rb in range(n_rb):
        rows = rb_rows(rb)
        mo = _dot(u_s[rows, :], wmo_ref[...])
        hm = _sigmoid(mo) * wide_s[rows, :M_W]
        for h in range(H_M):
            cols = head_cols(h, DH_M)
            hm_out[rows, cols] = _head_norm(hm[:, cols], mgn_ref[:, cols]).astype(BF16)


def _mixer_core_call(x, mod, mod_row, p, seq_len, group_rows, rope, init):
    t = x.shape[0]
    n_groups = t // group_rows
    n_chunks = group_rows // CHUNK
    seq_per_group = group_rows // seq_len
    use_rope = rope is not None
    has_init = init is not None
    emit_state = not has_init

    inputs = [x, mod, p["norm_mix"], p["w_rqk"], p["w_rv"], p["w_rg"], p["w_mqk"], p["w_mv"], p["w_mo"],
              p["w_gates"], p["conv_w"], p["conv_b"], p["gate_bias"], p["decay_rows"], p["ret_gn"], p["m_gn"]]
    in_specs = [
        pl.BlockSpec((group_rows, D_MODEL), lambda i: (i, 0)),
        pl.BlockSpec((1, N_MOD, D_MODEL), lambda i: (mod_row(i * group_rows), 0, 0)),
    ] + [_const_spec(a.shape) for a in inputs[2:]]
    if use_rope:
        inputs += [rope[0], rope[1]]
        in_specs += [_const_spec(rope[0].shape), _const_spec(rope[1].shape)]
    if has_init:
        s0, c0, n0, m0 = init
        inputs += [s0, c0, n0, m0]
        in_specs += [
            pl.BlockSpec((1, 1, N_DIR, H_RET, DK_RET, DV_RET), lambda i: (i, 0, 0, 0, 0, 0)),
            pl.BlockSpec((1, 1, N_DIR, H_M, DH_M, DH_M), lambda i: (i, 0, 0, 0, 0, 0)),
            pl.BlockSpec((1, N_PAIR, DH_M), lambda i: (i, 0, 0)),
            pl.BlockSpec((1, N_PAIR, CHUNK), lambda i: (i, 0, 0)),
        ]

    out_shape = [jax.ShapeDtypeStruct((t, RET_V), BF16), jax.ShapeDtypeStruct((t, M_W), BF16)]
    out_specs = [pl.BlockSpec((group_rows, RET_V), lambda i: (i, 0)),
                 pl.BlockSpec((group_rows, M_W), lambda i: (i, 0))]
    if emit_state:
        n_seq = t // seq_len
        out_shape += [
            jax.ShapeDtypeStruct((n_seq, 1, N_DIR, H_RET, DK_RET, DV_RET), F32),
            jax.ShapeDtypeStruct((n_seq, 1, N_DIR, H_M, DH_M, DH_M), F32),
            jax.ShapeDtypeStruct((n_seq, N_PAIR, DH_M), F32),
            jax.ShapeDtypeStruct((n_seq, N_DIR, N_PAIR, CHUNK), F32),
        ]
        out_specs += [
            pl.BlockSpec((seq_per_group, 1, N_DIR, H_RET, DK_RET, DV_RET), lambda i: (i, 0, 0, 0, 0, 0)),
            pl.BlockSpec((seq_per_group, 1, N_DIR, H_M, DH_M, DH_M), lambda i: (i, 0, 0, 0, 0, 0)),
            pl.BlockSpec((seq_per_group, N_PAIR, DH_M), lambda i: (i, 0, 0)),
            pl.BlockSpec((seq_per_group, N_DIR, N_PAIR, CHUNK), lambda i: (i, 0, 0, 0)),
        ]

    scratch = [
        pltpu.VMEM((group_rows, D_MODEL), BF16),
        pltpu.VMEM((group_rows, RET_QK), F32),
        pltpu.VMEM((group_rows, RET_QK), F32),
        pltpu.VMEM((group_rows, RET_V), BF16),
        pltpu.VMEM((n_chunks, H_M, DH_M, CHUNK), F32),
        pltpu.VMEM((n_chunks, H_M, DH_M, CHUNK), BF16),
        pltpu.VMEM((n_chunks, N_PAIR, DK_RET, DV_RET), F32),
        pltpu.VMEM((group_rows, RET_V), F32),
        pltpu.VMEM((group_rows, GATE_LANES), F32),
        pltpu.VMEM((n_chunks, 3 * N_PAIR, CHUNK), F32),
        pltpu.VMEM((n_chunks, N_PAIR, CHUNK), F32),
        pltpu.VMEM((H_RET, 5, CHUNK, CHUNK), F32),
        pltpu.VMEM((n_chunks, 2, N_PAIR, CHUNK), F32),
        pltpu.VMEM((n_chunks, N_DIR, N_PAIR, CHUNK), F32),
    ]
    kern = functools.partial(_mixer_core_kernel, group_rows=group_rows, seq_len=seq_len, use_rope=use_rope,
                             has_init=has_init, emit_state=emit_state)
    return pl.pallas_call(
        kern,
        out_shape=out_shape,
        grid=(n_groups,),
        in_specs=in_specs,
        out_specs=out_specs,
        scratch_shapes=scratch,
        compiler_params=pltpu.CompilerParams(
            dimension_semantics=("arbitrary",), vmem_limit_bytes=VMEM_LIMIT),
        name="mixer_core",
    )(*inputs)


def _mixer_out_kernel(x_ref, rg_ref, hm_ref, mod_ref, nmix_ref, wbg_ref, wru_ref, wmu_ref, wout_ref,
                      nffn_ref, w1_ref, w3_ref, w2_ref, nfin_ref, o_ref):
    mod = mod_ref[0]
    x = x_ref[...]
    u = _norm_mod(x, nmix_ref[...], mod[3:4], mod[4:5]).astype(BF16)
    gates = _sigmoid(_dot(u, wbg_ref[...]))
    ret_branch = _dot(rg_ref[...], wru_ref[...])
    m_branch = _dot(hm_ref[...], wmu_ref[...])
    merged = (gates[:, :D_MODEL] * ret_branch + gates[:, D_MODEL:] * m_branch).astype(BF16)
    x = x + mod[5:6] * _dot(merged, wout_ref[...])
    x = _swiglu_half_step(x, mod, 6, nffn_ref[...], w1_ref, w3_ref, w2_ref)
    ms = jnp.mean(x * x, axis=-1, keepdims=True)
    o_ref[...] = x * lax.rsqrt(ms + EPS) * nfin_ref[...]


def _mixer_out_call(x, rgated, hmn, mod, mod_row, p):
    t = x.shape[0]
    tm = FFN_TILE
    consts = [p["norm_mix"], p["w_bg"], p["w_ret_up"], p["w_m_up"], p["w_out"],
              p["norm_ffn2"], p["w1_ffn2"], p["w3_ffn2"], p["w2_ffn2"], p["norm_final"]]
    return pl.pallas_call(
        _mixer_out_kernel,
        out_shape=jax.ShapeDtypeStruct((t, D_MODEL), F32),
        grid=(t // tm,),
        in_specs=[
            pl.BlockSpec((tm, D_MODEL), lambda i: (i, 0)),
            pl.BlockSpec((tm, RET_V), lambda i: (i, 0)),
            pl.BlockSpec((tm, M_W), lambda i: (i, 0)),
            pl.BlockSpec((1, N_MOD, D_MODEL), lambda i: (mod_row(i * tm), 0, 0)),
        ] + [_const_spec(a.shape) for a in consts],
        out_specs=pl.BlockSpec((tm, D_MODEL), lambda i: (i, 0)),
        compiler_params=pltpu.CompilerParams(
            dimension_semantics=("arbitrary",), vmem_limit_bytes=VMEM_LIMIT),
        name="mixer_out",
    )(x, rgated, hmn, mod, *consts)


def _rope_tables(seq_len):
    rows = seq_len // GRID_W
    r = jnp.repeat(jnp.arange(rows, dtype=F32), GRID_W)
    col = (jnp.arange(seq_len) % GRID_W).astype(F32)
    n_f = DK_RET // 4
    freqs = ROPE_BASE ** (-jnp.arange(n_f, dtype=F32) / n_f)
    ang = jnp.concatenate([r[:, None] * freqs, col[:, None] * freqs], axis=-1)
    cos, sin = jnp.cos(ang), jnp.sin(ang)
    return jnp.concatenate([cos, cos], axis=-1), jnp.concatenate([-sin, sin], axis=-1)


def _layer_params(l, norm_ffn1, w1_ffn1, w3_ffn1, w2_ffn1, norm_mix, w_in, conv_w, conv_b,
                  ret_decay_logit, b_igate, b_fgate, ret_gn, m_gn, w_ret_up, w_m_up, w_out,
                  norm_ffn2, w1_ffn2, w3_ffn2, w2_ffn2, norm_final):
    w = w_in[l]
    o = 0
    cols = {}
    for name, width in (("w_rqk", 2 * RET_QK), ("w_rv", RET_V), ("w_rg", RET_V), ("w_mqk", 2 * M_W),
                        ("w_mv", M_W), ("w_mo", M_W), ("w_gates", 2 * N_PAIR), ("w_bg", N_DIR * D_MODEL)):
        cols[name] = w[:, o:o + width].astype(BF16)
        o += width
    cols["w_gates"] = jnp.pad(cols["w_gates"], ((0, 0), (0, GATE_LANES - 2 * N_PAIR)))
    gate_bias = jnp.pad(jnp.concatenate([b_igate[l], b_fgate[l]]), (0, GATE_LANES - 2 * N_PAIR))
    p = dict(cols)
    p.update(
        norm_ffn1=norm_ffn1[l][None, :], norm_mix=norm_mix[l][None, :], norm_ffn2=norm_ffn2[l][None, :],
        norm_final=norm_final[None, :],
        w1_ffn1=w1_ffn1[l].astype(BF16), w3_ffn1=w3_ffn1[l].astype(BF16), w2_ffn1=w2_ffn1[l].astype(BF16),
        w1_ffn2=w1_ffn2[l].astype(BF16), w3_ffn2=w3_ffn2[l].astype(BF16), w2_ffn2=w2_ffn2[l].astype(BF16),
        w_ret_up=w_ret_up[l].astype(BF16), w_m_up=w_m_up[l].astype(BF16), w_out=w_out[l].astype(BF16),
        conv_w=conv_w[l], conv_b=conv_b[l][None, :],
        gate_bias=gate_bias[None, :],
        decay_rows=jnp.broadcast_to(ret_decay_logit[l].reshape(N_DIR * H_RET, 1), (N_DIR * H_RET, DV_RET)),
        ret_gn=ret_gn[l][None, :], m_gn=m_gn[l][None, :],
    )
    return p


def kernel(x_prompt, x_sample, c, state_ret, state_mlstm_C, state_mlstm_n, state_mlstm_m, c_ctx, w_ada, b_ada, norm_ffn1, w1_ffn1, w3_ffn1, w2_ffn1, norm_mix, w_in, conv_w, conv_b, ret_decay_logit, b_igate, b_fgate, ret_gn, m_gn, w_ret_up, w_m_up, w_out, norm_ffn2, w1_ffn2, w3_ffn2, w2_ffn2, norm_final):
    bp, lp, _ = x_prompt.shape
    bs, ls, _ = x_sample.shape
    depth = w_ada.shape[0]
    assert depth == 1 and lp % CHUNK == 0 and PROMPT_GROUP_ROWS % lp == 0 and ls <= MAX_GROUP_ROWS
    assert lp & (lp - 1) == 0 and ls & (ls - 1) == 0

    ctx_row = 8 * ((bs + 7) // 8)
    cvec = jnp.zeros((ctx_row + 8, D_MODEL), F32).at[:bs].set(c).at[ctx_row].set(c_ctx)
    rope = _rope_tables(ls)

    xp = x_prompt.reshape(bp * lp, D_MODEL)
    xs = x_sample.reshape(bs * ls, D_MODEL)
    prompt_row = lambda r: ctx_row
    sample_row = lambda r: r // ls

    l = 0
    p = _layer_params(l, norm_ffn1, w1_ffn1, w3_ffn1, w2_ffn1, norm_mix, w_in, conv_w, conv_b,
                      ret_decay_logit, b_igate, b_fgate, ret_gn, m_gn, w_ret_up, w_m_up, w_out,
                      norm_ffn2, w1_ffn2, w3_ffn2, w2_ffn2, norm_final)
    mod = _ada_call(cvec, w_ada[l], b_ada[l][None, :]).reshape(ctx_row + 8, N_MOD, D_MODEL)

    n0 = state_mlstm_n[:, l].astype(F32).reshape(bs, N_PAIR, DH_M)
    m0 = jnp.broadcast_to(state_mlstm_m[:, l].astype(F32).reshape(bs, N_PAIR, 1), (bs, N_PAIR, CHUNK))
    init = (state_ret.astype(F32), state_mlstm_C.astype(F32), n0, m0)

    xp1 = _ffn_call(xp, mod, prompt_row, p["norm_ffn1"], p["w1_ffn1"], p["w3_ffn1"], p["w2_ffn1"])
    xs1 = _ffn_call(xs, mod, sample_row, p["norm_ffn1"], p["w1_ffn1"], p["w3_ffn1"], p["w2_ffn1"])

    rg_p, hm_p, s_fin, c_fin, n_fin, m_fin = _mixer_core_call(xp1, mod, prompt_row, p, lp, PROMPT_GROUP_ROWS, None, None)
    rg_s, hm_s = _mixer_core_call(xs1, mod, sample_row, p, ls, ls, rope, init)

    yp = _mixer_out_call(xp1, rg_p, hm_p, mod, prompt_row, p)
    ys = _mixer_out_call(xs1, rg_s, hm_s, mod, sample_row, p)

    dt = x_prompt.dtype
    new_c = c_fin
    new_n = n_fin.reshape(bp, 1, N_DIR, H_M, DH_M)
    m_diag = jnp.stack([m_fin[:, d, d * H_M:(d + 1) * H_M, 0] for d in range(N_DIR)], axis=1)
    new_m = m_diag.reshape(bp, 1, N_DIR, H_M)
    return (yp.reshape(bp, lp, D_MODEL).astype(dt), ys.reshape(bs, ls, D_MODEL).astype(dt),
            s_fin.astype(dt), new_c.astype(dt), new_n.astype(dt), new_m.astype(dt))
```

```python
import functools

import jax
import jax.numpy as jnp
from jax import lax
from jax.experimental import pallas as pl
from jax.experimental.pallas import tpu as pltpu

F32 = jnp.float32
BF16 = jnp.bfloat16

D_MODEL = 1024
D_FF = 2816
CHUNK = 128
N_DIR = 2
H_RET = 4
DK_RET = 128
DV_RET = 256
H_M = 4
DH_M = 128
RET_QK = H_RET * DK_RET
RET_V = H_RET * DV_RET
M_W = H_M * DH_M
GRID_W = 64
ROPE_BASE = 10000.0
EPS = 1e-6
GN_EPS = 1e-5
N_MOD = 9
N_PAIR = N_DIR * H_M
GATE_LANES = 128
NORM_ROWS = 16

MAX_GROUP_ROWS = 1024
PROMPT_GROUP_ROWS = 512
ROW_BLOCK = 256
CHUNK_UNROLL = 2
FFN_TILE = 512
VMEM_LIMIT = 60 * 1024 * 1024


def _sigmoid(x):
    return 1.0 / (1.0 + jnp.exp(-x))


def _silu(x):
    return x * _sigmoid(x)


def _log_sigmoid(x):
    return -(jnp.maximum(-x, 0.0) + jnp.log1p(jnp.exp(-jnp.abs(x))))


def _norm_mod(x, gain, shift, scale):
    ms = jnp.mean(x * x, axis=-1, keepdims=True)
    y = x * lax.rsqrt(ms + EPS) * gain
    return y * (1.0 + scale) + shift


def _dot(a, b):
    return jnp.dot(a, b, preferred_element_type=F32)


def _dot_nt(a, b):
    return lax.dot_general(a, b, (((1,), (1,)), ((), ())), preferred_element_type=F32)


def _head_norm(o, gain):
    mu = jnp.mean(o, axis=-1, keepdims=True)
    oc = o - mu
    var = jnp.mean(oc * oc, axis=-1, keepdims=True)
    return oc * lax.rsqrt(var + GN_EPS) * gain


def _ada_kernel(c_ref, w_ref, b_ref, o_ref):
    s = _silu(c_ref[...]).astype(BF16)
    o_ref[...] = _dot(s, w_ref[...].astype(BF16)) + b_ref[...]


def _ada_call(cvec, w_ada, b_ada):
    rows = cvec.shape[0]
    n_out = w_ada.shape[1]
    tile = D_MODEL
    return pl.pallas_call(
        _ada_kernel,
        out_shape=jax.ShapeDtypeStruct((rows, n_out), F32),
        grid=(n_out // tile,),
        in_specs=[
            pl.BlockSpec((rows, D_MODEL), lambda j: (0, 0)),
            pl.BlockSpec((D_MODEL, tile), lambda j: (0, j)),
            pl.BlockSpec((1, tile), lambda j: (0, j)),
        ],
        out_specs=pl.BlockSpec((rows, tile), lambda j: (0, j)),
        compiler_params=pltpu.CompilerParams(
            dimension_semantics=("arbitrary",), vmem_limit_bytes=VMEM_LIMIT),
        name="ada",
    )(cvec, w_ada, b_ada)


def _swiglu_half_step(x, mod, k0, gain, w1_ref, w3_ref, w2_ref):
    u = _norm_mod(x, gain, mod[k0:k0 + 1], mod[k0 + 1:k0 + 2]).astype(BF16)
    h1 = _dot(u, w1_ref[...])
    h3 = _dot(u, w3_ref[...])
    h = (_silu(h1) * h3).astype(BF16)
    y = _dot(h, w2_ref[...])
    return x + 0.5 * mod[k0 + 2:k0 + 3] * y


def _ffn_kernel(x_ref, mod_ref, g_ref, w1_ref, w3_ref, w2_ref, o_ref):
    o_ref[...] = _swiglu_half_step(x_ref[...], mod_ref[0], 0, g_ref[...], w1_ref, w3_ref, w2_ref)


def _const_spec(shape):
    nd = len(shape)
    return pl.BlockSpec(shape, lambda i: (0,) * nd)


def _ffn_call(x, mod, mod_row, gain, w1, w3, w2):
    t = x.shape[0]
    tm = FFN_TILE
    return pl.pallas_call(
        _ffn_kernel,
        out_shape=jax.ShapeDtypeStruct((t, D_MODEL), F32),
        grid=(t // tm,),
        in_specs=[
            pl.BlockSpec((tm, D_MODEL), lambda i: (i, 0)),
            pl.BlockSpec((1, N_MOD, D_MODEL), lambda i: (mod_row(i * tm), 0, 0)),
            _const_spec((1, D_MODEL)),
            _const_spec((D_MODEL, D_FF)),
            _const_spec((D_MODEL, D_FF)),
            _const_spec((D_FF, D_MODEL)),
        ],
        out_specs=pl.BlockSpec((tm, D_MODEL), lambda i: (i, 0)),
        compiler_params=pltpu.CompilerParams(
            dimension_semantics=("arbitrary",), vmem_limit_bytes=VMEM_LIMIT),
        name="ffn",
    )(x, mod, gain, w1, w3, w2)


def _mixer_core_kernel(*refs, group_rows, seq_len, use_rope, has_init, emit_state):
    it = iter(refs)
    x_ref, mod_ref, nmix_ref = next(it), next(it), next(it)
    wrqk_ref, wrv_ref, wrg_ref = next(it), next(it), next(it)
    wmqk_ref, wmv_ref, wmo_ref, wgt_ref = next(it), next(it), next(it), next(it)
    convw_ref, convb_ref, gbias_ref, decay_ref = next(it), next(it), next(it), next(it)
    rgn_ref, mgn_ref = next(it), next(it)
    if use_rope:
        cos_ref, sin_ref = next(it), next(it)
    if has_init:
        s0_ref, c0_ref, n0_ref, m0_ref = next(it), next(it), next(it), next(it)
    rg_out, hm_out = next(it), next(it)
    if emit_state:
        sfin_ref, cfin_ref, nfin_ref, mfin_ref = next(it), next(it), next(it), next(it)
    (u_s, q_s, k_s, v_s, t_s, qt_s, kv_s, wide_s, igb_s, zt_s, wt_s, dec_s, stat_s,
     mprev_s) = (next(it) for _ in range(14))

    rows_total = group_rows
    n_chunks = rows_total // CHUNK
    cps = seq_len // CHUNK
    n_seq = rows_total // seq_len
    n_rb = rows_total // ROW_BLOCK
    mod = mod_ref[0]

    def rb_rows(rb):
        return slice(rb * ROW_BLOCK, (rb + 1) * ROW_BLOCK)

    def head_cols(h, width):
        return slice(h * width, (h + 1) * width)

    chunks_per_rb = ROW_BLOCK // CHUNK
    for rb in range(n_rb):
        rows = rb_rows(rb)
        ub = _norm_mod(x_ref[rows, :], nmix_ref[...], mod[3:4], mod[4:5]).astype(BF16)
        u_s[rows, :] = ub
        qk = _dot(ub, wrqk_ref[...])
        q = qk[:, :RET_QK]
        k = qk[:, RET_QK:] * (DK_RET ** -0.5)
        for h in range(H_RET):
            cols = head_cols(h, DK_RET)
            qh, kh = q[:, cols], k[:, cols]
            if use_rope:
                cs, sn = cos_ref[rows, :], sin_ref[rows, :]
                qh = qh * cs + pltpu.roll(qh, DK_RET // 2, 1) * sn
                kh = kh * cs + pltpu.roll(kh, DK_RET // 2, 1) * sn
            q_s[rows, cols] = qh
            for ch in range(chunks_per_rb):
                t_s[rb * chunks_per_rb + ch, h] = kh[ch * CHUNK:(ch + 1) * CHUNK, :].T
        v_s[rows, :] = _dot(ub, wrv_ref[...]).astype(BF16)

    pos_i = lax.broadcasted_iota(jnp.int32, (CHUNK, CHUNK), 0).astype(F32)
    pos_j = lax.broadcasted_iota(jnp.int32, (CHUNK, CHUNK), 1).astype(F32)
    lg_rows = _log_sigmoid(decay_ref[...])
    for h in range(H_RET):
        lgf = lg_rows[h:h + 1, :CHUNK]
        lgb = lg_rows[H_RET + h:H_RET + h + 1, :CHUNK]
        dec_s[h, 0] = jnp.where(pos_i > pos_j, jnp.exp((pos_i - pos_j) * lgf),
                                jnp.where(pos_i < pos_j, jnp.exp((pos_j - pos_i) * lgb), 2.0))
        dec_s[h, 1] = jnp.exp((CHUNK - 1.0 - pos_j) * lgf)
        dec_s[h, 2] = jnp.exp(pos_j * lgb)
        dec_s[h, 3] = jnp.exp((pos_i + 1.0) * lgf)
        dec_s[h, 4] = jnp.exp((CHUNK - pos_i) * lgb)
    g_chunk = jnp.exp(CHUNK * lg_rows)

    def ret_pass1(n, carry):
        rows = pl.ds(pl.multiple_of(n * CHUNK, CHUNK), CHUNK)
        for h in range(H_RET):
            k_t = t_s[n, h]
            kcat = jnp.concatenate([k_t * dec_s[h, 1], k_t * dec_s[h, 2]], axis=0).astype(BF16)
            kv = _dot(kcat, v_s[rows, head_cols(h, DV_RET)])
            kv_s[n, 2 * h] = kv[:DK_RET]
            kv_s[n, 2 * h + 1] = kv[DK_RET:]
        return carry

    lax.fori_loop(0, n_chunks, ret_pass1, 0, unroll=CHUNK_UNROLL)

    for g in range(n_seq):
        for h in range(H_RET):
            for d in range(N_DIR):
                slot = 2 * h + d
                decay = g_chunk[d * H_RET + h:d * H_RET + h + 1, :]
                state = s0_ref[0, 0, d, h] if has_init else jnp.zeros((DK_RET, DV_RET), F32)
                order = range(g * cps, (g + 1) * cps)
                for n in (order if d == 0 else reversed(order)):
                    local = kv_s[n, slot]
                    kv_s[n, slot] = state
                    state = decay * state + local
                if emit_state:
                    sfin_ref[g, 0, d, h] = state

    def ret_pass2(n, carry):
        rows = pl.ds(pl.multiple_of(n * CHUNK, CHUNK), CHUNK)
        all_scores = [_dot(q_s[rows, head_cols(h, DK_RET)].astype(BF16), t_s[n, h].astype(BF16))
                      for h in range(H_RET)]
        for h in range(H_RET):
            qh = q_s[rows, head_cols(h, DK_RET)]
            vh = v_s[rows, head_cols(h, DV_RET)]
            sm = (all_scores[h] * dec_s[h, 0]).astype(BF16)
            qcat = jnp.concatenate([qh * dec_s[h, 3], qh * dec_s[h, 4]], axis=1).astype(BF16)
            scat = jnp.concatenate([kv_s[n, 2 * h], kv_s[n, 2 * h + 1]], axis=0).astype(BF16)
            o = _dot(sm, vh) + _dot(qcat, scat)
            wide_s[rows, head_cols(h, DV_RET)] = _head_norm(o, rgn_ref[:, head_cols(h, DV_RET)])
        return carry

    lax.fori_loop(0, n_chunks, ret_pass2, 0, unroll=CHUNK_UNROLL)

    for rb in range(n_rb):
        rows = rb_rows(rb)
        rg = _dot(u_s[rows, :], wrg_ref[...])
        rg_out[rows, :] = (wide_s[rows, :] * _silu(rg)).astype(BF16)

    for rb in range(n_rb):
        rows = rb_rows(rb)
        ub = u_s[rows, :]
        wide_s[rows, :] = _dot(ub, wmqk_ref[...])
        mv = _dot(ub, wmv_ref[...])
        for ch in range(chunks_per_rb):
            for h in range(H_M):
                t_s[rb * chunks_per_rb + ch, h] = mv[ch * CHUNK:(ch + 1) * CHUNK, head_cols(h, DH_M)].T
        gt = _dot(ub, wgt_ref[...]) + gbias_ref[...]
        glane = lax.broadcasted_iota(jnp.int32, (ROW_BLOCK, GATE_LANES), 1)
        igb_s[rows, :] = jnp.where(glane < N_PAIR, gt, _log_sigmoid(gt))

    row_id = lax.broadcasted_iota(jnp.int32, (rows_total, CHUNK), 0)
    seq_pos = row_id & (seq_len - 1)
    for cb in range(2 * M_W // CHUNK):
        cols = slice(cb * CHUNK, (cb + 1) * CHUNK)
        xc = wide_s[:, cols]
        prev = jnp.where(seq_pos == 0, 0.0, pltpu.roll(xc, 1, 0))
        nxt = jnp.where(seq_pos == seq_len - 1, 0.0, pltpu.roll(xc, rows_total - 1, 0))
        y = convb_ref[:, cols] + convw_ref[0:1, cols] * prev
        y = y + convw_ref[1:2, cols] * xc
        y = y + convw_ref[2:3, cols] * nxt
        y = _silu(y)
        if cb < H_M:
            for n in range(n_chunks):
                qt_s[n, cb] = y[n * CHUNK:(n + 1) * CHUNK, :].T.astype(BF16)
        else:
            k_s[:, head_cols(cb - H_M, DH_M)] = y * (DH_M ** -0.5)

    z = igb_s[...]
    cpos = row_id & (CHUNK - 1)
    glane = lax.broadcasted_iota(jnp.int32, (rows_total, GATE_LANES), 1)
    pre, suf = z, z
    step = 1
    while step < CHUNK:
        pre = pre + jnp.where(cpos >= step, pltpu.roll(pre, step, 0), 0.0)
        suf = suf + jnp.where(cpos < CHUNK - step, pltpu.roll(suf, rows_total - step, 0), 0.0)
        step *= 2
    cum = jnp.where(glane < N_PAIR + H_M, pre, suf)
    i_minus_b = z - pltpu.roll(cum, GATE_LANES - N_PAIR, 1)
    igb = jnp.where(glane < N_PAIR, z,
                    jnp.where(glane < 2 * N_PAIR, cum, pltpu.roll(i_minus_b, 2 * N_PAIR, 1)))
    igb_s[...] = igb

    for n in range(n_chunks):
        zt_s[n] = igb[n * CHUNK:(n + 1) * CHUNK, :].T[0:3 * N_PAIR, :]
    ig_all = zt_s[:, 0:N_PAIR, :]
    b_all = zt_s[:, N_PAIR:2 * N_PAIR, :]
    pair = lax.broadcasted_iota(jnp.int32, (n_chunks, N_PAIR, 1), 1)
    b_last = jnp.where(pair < H_M, b_all[:, :, CHUNK - 1:CHUNK], b_all[:, :, 0:1])
    g_all = (b_last - b_all) + ig_all
    m_loc = jnp.max(g_all, axis=-1, keepdims=True)
    wt_s[...] = jnp.exp(g_all - m_loc)
    stat_s[:, 0] = jnp.broadcast_to(b_last, (n_chunks, N_PAIR, CHUNK))
    stat_s[:, 1] = jnp.broadcast_to(m_loc, (n_chunks, N_PAIR, CHUNK))

    def m_pass1(n, carry):
        rows = pl.ds(pl.multiple_of(n * CHUNK, CHUNK), CHUNK)
        w_t = wt_s[n]
        for h in range(H_M):
            kh = k_s[rows, head_cols(h, DH_M)].astype(BF16)
            v_t = t_s[n, h]
            parts = []
            for d in range(N_DIR):
                w_row = w_t[d * H_M + h:d * H_M + h + 1, :]
                parts += [v_t * w_row, jnp.broadcast_to(w_row, (NORM_ROWS, CHUNK))]
            local = _dot(jnp.concatenate(parts, axis=0).astype(BF16), kh)
            for d in range(N_DIR):
                c = d * H_M + h
                base = d * (DH_M + NORM_ROWS)
                kv_s[n, c, :, :DH_M] = local[base:base + DH_M]
                kv_s[n, c, 0:NORM_ROWS, DH_M:] = local[base + DH_M:base + DH_M + NORM_ROWS]
        return carry

    lax.fori_loop(0, n_chunks, m_pass1, 0, unroll=CHUNK_UNROLL)

    for g in range(n_seq):
        for d in range(N_DIR):
            order = list(range(g * cps, (g + 1) * cps))
            if d == 1:
                order.reverse()
            m_prev = m0_ref[0] if has_init else jnp.zeros((N_PAIR, CHUNK), F32)
            a1s, a2s = [], []
            for n in order:
                b_last_n, m_loc_n = stat_s[n, 0], stat_s[n, 1]
                m_new = jnp.maximum(b_last_n + m_prev, m_loc_n)
                a1s.append(jnp.exp(b_last_n + m_prev - m_new))
                a2s.append(jnp.exp(m_loc_n - m_new))
                mprev_s[n, d] = m_prev
                m_prev = m_new
            if emit_state:
                mfin_ref[g, d] = m_prev
            for h in range(H_M):
                c = d * H_M + h
                if has_init:
                    mem = c0_ref[0, 0, d, h]
                    nrm = jnp.broadcast_to(n0_ref[0][c:c + 1, :], (NORM_ROWS, DH_M))
                else:
                    mem = jnp.zeros((DH_M, DH_M), F32)
                    nrm = jnp.zeros((NORM_ROWS, DH_M), F32)
                for idx, n in enumerate(order):
                    a1 = a1s[idx][c:c + 1, :]
                    a2 = a2s[idx][c:c + 1, :]
                    local_mem = kv_s[n, c, :, :DH_M]
                    local_nrm = kv_s[n, c, 0:NORM_ROWS, DH_M:]
                    kv_s[n, c, :, :DH_M] = mem
                    kv_s[n, c, 0:NORM_ROWS, DH_M:] = nrm
                    mem = a1 * mem + a2 * local_mem
                    nrm = a1 * nrm + a2 * local_nrm
                if emit_state:
                    cfin_ref[g, 0, d, h] = mem
                    nfin_ref[g, c:c + 1, :] = nrm[0:1]

    tri_j = lax.broadcasted_iota(jnp.int32, (CHUNK, CHUNK), 0)
    tri_i = lax.broadcasted_iota(jnp.int32, (CHUNK, CHUNK), 1)
    mem_rows = DH_M + NORM_ROWS

    def m_pass2(n, carry):
        rows = pl.ds(pl.multiple_of(n * CHUNK, CHUNK), CHUNK)
        zc = igb_s[rows, :]
        zt = zt_s[n]
        bigs = []
        for h in range(H_M):
            kh = k_s[rows, head_cols(h, DH_M)].astype(BF16)
            mems = [jnp.concatenate([kv_s[n, d * H_M + h, :, :DH_M], kv_s[n, d * H_M + h, 0:NORM_ROWS, DH_M:]],
                                    axis=0).astype(BF16) for d in range(N_DIR)]
            bigs.append(_dot(jnp.concatenate([kh] + mems, axis=0), qt_s[n, h]))
        for h in range(H_M):
            big = bigs[h]
            scores_t = big[:CHUNK]
            weights, crosses, w_ints, m_is = [], [], [], []
            for d in range(N_DIR):
                c = d * H_M + h
                a_col = zc[:, 2 * N_PAIR + c:2 * N_PAIR + c + 1]
                b_row = zt[N_PAIR + c:N_PAIR + c + 1, :]
                mask = (tri_j <= tri_i) if d == 0 else (tri_j >= tri_i)
                dlog = jnp.where(mask, b_row + a_col, -jnp.inf)
                lin = b_row + mprev_s[n, d][c:c + 1, :]
                m_i = jnp.maximum(jnp.max(dlog, axis=0, keepdims=True), lin)
                weights.append(scores_t * jnp.exp(dlog - m_i))
                crosses.append(big[CHUNK + d * mem_rows:CHUNK + (d + 1) * mem_rows])
                w_ints.append(jnp.exp(lin - m_i))
                m_is.append(m_i)
            intra = _dot(t_s[n, h].astype(BF16), jnp.concatenate(weights, axis=1).astype(BF16))
            h_sum_t = jnp.zeros((DH_M, CHUNK), F32)
            for d in range(N_DIR):
                cross, w_int = crosses[d], w_ints[d]
                num = intra[:, d * CHUNK:(d + 1) * CHUNK] + cross[:DH_M] * w_int
                den = jnp.sum(weights[d], axis=0, keepdims=True) + cross[DH_M:DH_M + 1] * w_int
                denom = jnp.maximum(jnp.abs(den), jnp.exp(-m_is[d]))
                h_sum_t = h_sum_t + num * (1.0 / denom)
            wide_s[rows, head_cols(h, DH_M)] = h_sum_t.T
        return carry

    lax.fori_loop(0, n_chunks, m_pass2, 0, unroll=CHUNK_UNROLL)

    for rb in range(n_rb):
        rows = rb_rows(rb)
        mo = _dot(u_s[rows, :], wmo_ref[...])
        hm = _sigmoid(mo) * wide_s[rows, :M_W]
        for h in range(H_M):
            cols = head_cols(h, DH_M)
            hm_out[rows, cols] = _head_norm(hm[:, cols], mgn_ref[:, cols]).astype(BF16)


def _mixer_core_call(x, mod, mod_row, p, seq_len, group_rows, rope, init):
    t = x.shape[0]
    n_groups = t // group_rows
    n_chunks = group_rows // CHUNK
    seq_per_group = group_rows // seq_len
    use_rope = rope is not None
    has_init = init is not None
    emit_state = not has_init

    inputs = [x, mod, p["norm_mix"], p["w_rqk"], p["w_rv"], p["w_rg"], p["w_mqk"], p["w_mv"], p["w_mo"],
              p["w_gates"], p["conv_w"], p["conv_b"], p["gate_bias"], p["decay_rows"], p["ret_gn"], p["m_gn"]]
    in_specs = [
        pl.BlockSpec((group_rows, D_MODEL), lambda i: (i, 0)),
        pl.BlockSpec((1, N_MOD, D_MODEL), lambda i: (mod_row(i * group_rows), 0, 0)),
    ] + [_const_spec(a.shape) for a in inputs[2:]]
    if use_rope:
        inputs += [rope[0], rope[1]]
        in_specs += [_const_spec(rope[0].shape), _const_spec(rope[1].shape)]
    if has_init:
        s0, c0, n0, m0 = init
        inputs += [s0, c0, n0, m0]
        in_specs += [
            pl.BlockSpec((1, 1, N_DIR, H_RET, DK_RET, DV_RET), lambda i: (i, 0, 0, 0, 0, 0)),
            pl.BlockSpec((1, 1, N_DIR, H_M, DH_M, DH_M), lambda i: (i, 0, 0, 0, 0, 0)),
            pl.BlockSpec((1, N_PAIR, DH_M), lambda i: (i, 0, 0)),
            pl.BlockSpec((1, N_PAIR, CHUNK), lambda i: (i, 0, 0)),
        ]

    out_shape = [jax.ShapeDtypeStruct((t, RET_V), BF16), jax.ShapeDtypeStruct((t, M_W), BF16)]
    out_specs = [pl.BlockSpec((group_rows, RET_V), lambda i: (i, 0)),
                 pl.BlockSpec((group_rows, M_W), lambda i: (i, 0))]
    if emit_state:
        n_seq = t // seq_len
        out_shape += [
            jax.ShapeDtypeStruct((n_seq, 1, N_DIR, H_RET, DK_RET, DV_RET), F32),
            jax.ShapeDtypeStruct((n_seq, 1, N_DIR, H_M, DH_M, DH_M), F32),
            jax.ShapeDtypeStruct((n_seq, N_PAIR, DH_M), F32),
            jax.ShapeDtypeStruct((n_seq, N_DIR, N_PAIR, CHUNK), F32),
        ]
        out_specs += [
            pl.BlockSpec((seq_per_group, 1, N_DIR, H_RET, DK_RET, DV_RET), lambda i: (i, 0, 0, 0, 0, 0)),
            pl.BlockSpec((seq_per_group, 1, N_DIR, H_M, DH_M, DH_M), lambda i: (i, 0, 0, 0, 0, 0)),
            pl.BlockSpec((seq_per_group, N_PAIR, DH_M), lambda i: (i, 0, 0)),
            pl.BlockSpec((seq_per_group, N_DIR, N_PAIR, CHUNK), lambda i: (i, 0, 0, 0)),
        ]

    scratch = [
        pltpu.VMEM((group_rows, D_MODEL), BF16),
        pltpu.VMEM((group_rows, RET_QK), F32),
        pltpu.VMEM((group_rows, RET_QK), F32),
        pltpu.VMEM((group_rows, RET_V), BF16),
        pltpu.VMEM((n_chunks, H_M, DH_M, CHUNK), F32),
        pltpu.VMEM((n_chunks, H_M, DH_M, CHUNK), BF16),
        pltpu.VMEM((n_chunks, N_PAIR, DK_RET, DV_RET), F32),
        pltpu.VMEM((group_rows, RET_V), F32),
        pltpu.VMEM((group_rows, GATE_LANES), F32),
        pltpu.VMEM((n_chunks, 3 * N_PAIR, CHUNK), F32),
        pltpu.VMEM((n_chunks, N_PAIR, CHUNK), F32),
        pltpu.VMEM((H_RET, 5, CHUNK, CHUNK), F32),
        pltpu.VMEM((n_chunks, 2, N_PAIR, CHUNK), F32),
        pltpu.VMEM((n_chunks, N_DIR, N_PAIR, CHUNK), F32),
    ]
    kern = functools.partial(_mixer_core_kernel, group_rows=group_rows, seq_len=seq_len, use_rope=use_rope,
                             has_init=has_init, emit_state=emit_state)
    return pl.pallas_call(
        kern,
        out_shape=out_shape,
        grid=(n_groups,),
        in_specs=in_specs,
        out_specs=out_specs,
        scratch_shapes=scratch,
        compiler_params=pltpu.CompilerParams(
            dimension_semantics=("arbitrary",), vmem_limit_bytes=VMEM_LIMIT),
        name="mixer_core",
    )(*inputs)


def _mixer_out_kernel(x_ref, rg_ref, hm_ref, mod_ref, nmix_ref, wbg_ref, wru_ref, wmu_ref, wout_ref,
                      nffn_ref, w1_ref, w3_ref, w2_ref, nfin_ref, o_ref):
    mod = mod_ref[0]
    x = x_ref[...]
    u = _norm_mod(x, nmix_ref[...], mod[3:4], mod[4:5]).astype(BF16)
    gates = _sigmoid(_dot(u, wbg_ref[...]))
    ret_branch = _dot(rg_ref[...], wru_ref[...])
    m_branch = _dot(hm_ref[...], wmu_ref[...])
    merged = (gates[:, :D_MODEL] * ret_branch + gates[:, D_MODEL:] * m_branch).astype(BF16)
    x = x + mod[5:6] * _dot(merged, wout_ref[...])
    x = _swiglu_half_step(x, mod, 6, nffn_ref[...], w1_ref, w3_ref, w2_ref)
    ms = jnp.mean(x * x, axis=-1, keepdims=True)
    o_ref[...] = x * lax.rsqrt(ms + EPS) * nfin_ref[...]


def _mixer_out_call(x, rgated, hmn, mod, mod_row, p):
    t = x.shape[0]
    tm = FFN_TILE
    consts = [p["norm_mix"], p["w_bg"], p["w_ret_up"], p["w_m_up"], p["w_out"],
              p["norm_ffn2"], p["w1_ffn2"], p["w3_ffn2"], p["w2_ffn2"], p["norm_final"]]
    return pl.pallas_call(
        _mixer_out_kernel,
        out_shape=jax.ShapeDtypeStruct((t, D_MODEL), F32),
        grid=(t // tm,),
        in_specs=[
            pl.BlockSpec((tm, D_MODEL), lambda i: (i, 0)),
            pl.BlockSpec((tm, RET_V), lambda i: (i, 0)),
            pl.BlockSpec((tm, M_W), lambda i: (i, 0)),
            pl.BlockSpec((1, N_MOD, D_MODEL), lambda i: (mod_row(i * tm), 0, 0)),
        ] + [_const_spec(a.shape) for a in consts],
        out_specs=pl.BlockSpec((tm, D_MODEL), lambda i: (i, 0)),
        compiler_params=pltpu.CompilerParams(
            dimension_semantics=("arbitrary",), vmem_limit_bytes=VMEM_LIMIT),
        name="mixer_out",
    )(x, rgated, hmn, mod, *consts)


def _rope_tables(seq_len):
    rows = seq_len // GRID_W
    r = jnp.repeat(jnp.arange(rows, dtype=F32), GRID_W)
    col = (jnp.arange(seq_len) % GRID_W).astype(F32)
    n_f = DK_RET // 4
    freqs = ROPE_BASE ** (-jnp.arange(n_f, dtype=F32) / n_f)
    ang = jnp.concatenate([r[:, None] * freqs, col[:, None] * freqs], axis=-1)
    cos, sin = jnp.cos(ang), jnp.sin(ang)
    return jnp.concatenate([cos, cos], axis=-1), jnp.concatenate([-sin, sin], axis=-1)


def _layer_params(l, norm_ffn1, w1_ffn1, w3_ffn1, w2_ffn1, norm_mix, w_in, conv_w, conv_b,
                  ret_decay_logit, b_igate, b_fgate, ret_gn, m_gn, w_ret_up, w_m_up, w_out,
                  norm_ffn2, w1_ffn2, w3_ffn2, w2_ffn2, norm_final):
    w = w_in[l]
    o = 0
    cols = {}
    for name, width in (("w_rqk", 2 * RET_QK), ("w_rv", RET_V), ("w_rg", RET_V), ("w_mqk", 2 * M_W),
                        ("w_mv", M_W), ("w_mo", M_W), ("w_gates", 2 * N_PAIR), ("w_bg", N_DIR * D_MODEL)):
        cols[name] = w[:, o:o + width].astype(BF16)
        o += width
    cols["w_gates"] = jnp.pad(cols["w_gates"], ((0, 0), (0, GATE_LANES - 2 * N_PAIR)))
    gate_bias = jnp.pad(jnp.concatenate([b_igate[l], b_fgate[l]]), (0, GATE_LANES - 2 * N_PAIR))
    p = dict(cols)
    p.update(
        norm_ffn1=norm_ffn1[l][None, :], norm_mix=norm_mix[l][None, :], norm_ffn2=norm_ffn2[l][None, :],
        norm_final=norm_final[None, :],
        w1_ffn1=w1_ffn1[l].astype(BF16), w3_ffn1=w3_ffn1[l].astype(BF16), w2_ffn1=w2_ffn1[l].astype(BF16),
        w1_ffn2=w1_ffn2[l].astype(BF16), w3_ffn2=w3_ffn2[l].astype(BF16), w2_ffn2=w2_ffn2[l].astype(BF16),
        w_ret_up=w_ret_up[l].astype(BF16), w_m_up=w_m_up[l].astype(BF16), w_out=w_out[l].astype(BF16),
        conv_w=conv_w[l], conv_b=conv_b[l][None, :],
        gate_bias=gate_bias[None, :],
        decay_rows=jnp.broadcast_to(ret_decay_logit[l].reshape(N_DIR * H_RET, 1), (N_DIR * H_RET, DV_RET)),
        ret_gn=ret_gn[l][None, :], m_gn=m_gn[l][None, :],
    )
    return p


def kernel(x_prompt, x_sample, c, state_ret, state_mlstm_C, state_mlstm_n, state_mlstm_m, c_ctx, w_ada, b_ada, norm_ffn1, w1_ffn1, w3_ffn1, w2_ffn1, norm_mix, w_in, conv_w, conv_b, ret_decay_logit, b_igate, b_fgate, ret_gn, m_gn, w_ret_up, w_m_up, w_out, norm_ffn2, w1_ffn2, w3_ffn2, w2_ffn2, norm_final):
    bp, lp, _ = x_prompt.shape
    bs, ls, _ = x_sample.shape
    depth = w_ada.shape[0]
    assert depth == 1 and lp % CHUNK == 0 and PROMPT_GROUP_ROWS % lp == 0 and ls <= MAX_GROUP_ROWS
    assert lp & (lp - 1) == 0 and ls & (ls - 1) == 0

    ctx_row = 8 * ((bs + 7) // 8)
    cvec = jnp.zeros((ctx_row + 8, D_MODEL), F32).at[:bs].set(c).at[ctx_row].set(c_ctx)
    rope = _rope_tables(ls)

    xp = x_prompt.reshape(bp * lp, D_MODEL)
    xs = x_sample.reshape(bs * ls, D_MODEL)
    prompt_row = lambda r: ctx_row
    sample_row = lambda r: r // ls

    l = 0
    p = _layer_params(l, norm_ffn1, w1_ffn1, w3_ffn1, w2_ffn1, norm_mix, w_in, conv_w, conv_b,
                      ret_decay_logit, b_igate, b_fgate, ret_gn, m_gn, w_ret_up, w_m_up, w_out,
                      norm_ffn2, w1_ffn2, w3_ffn2, w2_ffn2, norm_final)
    mod = _ada_call(cvec, w_ada[l], b_ada[l][None, :]).reshape(ctx_row + 8, N_MOD, D_MODEL)

    n0 = state_mlstm_n[:, l].astype(F32).reshape(bs, N_PAIR, DH_M)
    m0 = jnp.broadcast_to(state_mlstm_m[:, l].astype(F32).reshape(bs, N_PAIR, 1), (bs, N_PAIR, CHUNK))
    init = (state_ret.astype(F32), state_mlstm_C.astype(F32), n0, m0)

    xp1 = _ffn_call(xp, mod, prompt_row, p["norm_ffn1"], p["w1_ffn1"], p["w3_ffn1"], p["w2_ffn1"])
    xs1 = _ffn_call(xs, mod, sample_row, p["norm_ffn1"], p["w1_ffn1"], p["w3_ffn1"], p["w2_ffn1"])

    rg_p, hm_p, s_fin, c_fin, n_fin, m_fin = _mixer_core_call(xp1, mod, prompt_row, p, lp, PROMPT_GROUP_ROWS, None, None)
    rg_s, hm_s = _mixer_core_call(xs1, mod, sample_row, p, ls, ls, rope, init)

    yp = _mixer_out_call(xp1, rg_p, hm_p, mod, prompt_row, p)
    ys = _mixer_out_call(xs1, rg_s, hm_s, mod, sample_row, p)

    dt = x_prompt.dtype
    new_c = c_fin
    new_n = n_fin.reshape(bp, 1, N_DIR, H_M, DH_M)
    m_diag = jnp.stack([m_fin[:, d, d * H_M:(d + 1) * H_M, 0] for d in range(N_DIR)], axis=1)
    new_m = m_diag.reshape(bp, 1, N_DIR, H_M)
    return (yp.reshape(bp, lp, D_MODEL).astype(dt), ys.reshape(bs, ls, D_MODEL).astype(dt),
            s_fin.astype(dt), new_c.astype(dt), new_n.astype(dt), new_m.astype(dt))
```

```python
import functools

import jax
import jax.numpy as jnp
from jax import lax
from jax.experimental import pallas as pl
from jax.experimental.pallas import tpu as pltpu

F32 = jnp.float32
BF16 = jnp.bfloat16

D_MODEL = 1024
D_FF = 2816
CHUNK = 128
N_DIR = 2
H_RET = 4
DK_RET = 128
DV_RET = 256
H_M = 4
DH_M = 128
RET_QK = H_RET * DK_RET
RET_V = H_RET * DV_RET
M_W = H_M * DH_M
D_IN = 2 * RET_QK + 2 * RET_V + 4 * M_W + 2 * N_DIR * H_M + N_DIR * D_MODEL
GRID_W = 64
ROPE_BASE = 10000.0
EPS = 1e-6
GN_EPS = 1e-5
N_MOD = 9
N_PAIR = N_DIR * H_M
GATE_LANES = 128
BF16_SUBLANES = 16
NORM_ROWS = BF16_SUBLANES

MAX_GROUP_ROWS = 1024
PROMPT_GROUP_ROWS = 512
ROW_BLOCK = 256
CHUNK_UNROLL = 2
FFN_TILE = 512
VMEM_LIMIT = 60 * 1024 * 1024


def _sigmoid(x):
    return 1.0 / (1.0 + jnp.exp(-x))


def _silu(x):
    return x * _sigmoid(x)


def _log_sigmoid(x):
    return -(jnp.maximum(-x, 0.0) + jnp.log1p(jnp.exp(-jnp.abs(x))))


def _norm_mod(x, gain, shift, scale):
    ms = jnp.mean(x * x, axis=-1, keepdims=True)
    y = x * lax.rsqrt(ms + EPS) * gain
    return y * (1.0 + scale) + shift


def _dot(a, b):
    return jnp.dot(a, b, preferred_element_type=F32)


def _dot_nt(a, b):
    return lax.dot_general(a, b, (((1,), (1,)), ((), ())), preferred_element_type=F32)


def _head_norm(o, gain):
    mu = jnp.mean(o, axis=-1, keepdims=True)
    oc = o - mu
    var = jnp.mean(oc * oc, axis=-1, keepdims=True)
    return oc * lax.rsqrt(var + GN_EPS) * gain


def _ada_kernel(c_ref, w_ref, b_ref, o_ref):
    s = _silu(c_ref[...]).astype(BF16)
    o_ref[...] = _dot(s, w_ref[...].astype(BF16)) + b_ref[...]


def _ada_call(cvec, w_ada, b_ada):
    rows = cvec.shape[0]
    n_out = w_ada.shape[1]
    tile = D_MODEL
    return pl.pallas_call(
        _ada_kernel,
        out_shape=jax.ShapeDtypeStruct((rows, n_out), F32),
        grid=(n_out // tile,),
        in_specs=[
            pl.BlockSpec((rows, D_MODEL), lambda j: (0, 0)),
            pl.BlockSpec((D_MODEL, tile), lambda j: (0, j)),
            pl.BlockSpec((1, tile), lambda j: (0, j)),
        ],
        out_specs=pl.BlockSpec((rows, tile), lambda j: (0, j)),
        compiler_params=pltpu.CompilerParams(
            dimension_semantics=("arbitrary",), vmem_limit_bytes=VMEM_LIMIT),
        name="ada",
    )(cvec, w_ada, b_ada)


def _swiglu_half_step(x, mod, k0, gain, w1_ref, w3_ref, w2_ref):
    u = _norm_mod(x, gain, mod[k0:k0 + 1], mod[k0 + 1:k0 + 2]).astype(BF16)
    h1 = _dot(u, w1_ref[...])
    h3 = _dot(u, w3_ref[...])
    h = (_silu(h1) * h3).astype(BF16)
    y = _dot(h, w2_ref[...])
    return x + 0.5 * mod[k0 + 2:k0 + 3] * y


def _ffn_kernel(*refs, cast_ranges):
    n_in = 6 + len(cast_ranges)
    x_ref, mod_ref, g_ref, w1_ref, w3_ref, w2_ref = refs[:6]
    o_ref = refs[n_in]
    o_ref[...] = _swiglu_half_step(x_ref[...], mod_ref[0], 0, g_ref[...], w1_ref, w3_ref, w2_ref)
    cast_out = iter(refs[n_in + 1:])
    for src_ref, ranges in zip(refs[6:n_in], cast_ranges):
        block = src_ref[...]
        for start, width in ranges:
            next(cast_out)[...] = block[:, start:start + width].astype(BF16)


def _const_spec(shape):
    nd = len(shape)
    return pl.BlockSpec(shape, lambda i: (0,) * nd)


def _ffn_call(x, mod, mod_row, gain, w1, w3, w2, casts=()):
    t = x.shape[0]
    tm = FFN_TILE
    steps = t // tm
    in_specs = [
        pl.BlockSpec((tm, D_MODEL), lambda i: (i, 0)),
        pl.BlockSpec((1, N_MOD, D_MODEL), lambda i: (mod_row(i * tm), 0, 0)),
        _const_spec((1, D_MODEL)),
        _const_spec((D_MODEL, D_FF)),
        _const_spec((D_MODEL, D_FF)),
        _const_spec((D_FF, D_MODEL)),
    ]
    out_shape = [jax.ShapeDtypeStruct((t, D_MODEL), F32)]
    out_specs = [pl.BlockSpec((tm, D_MODEL), lambda i: (i, 0))]
    for src, ranges in casts:
        rows, cols = src.shape
        assert rows % (steps * BF16_SUBLANES) == 0
        in_specs.append(pl.BlockSpec((rows // steps, cols), lambda i: (i, 0)))
        for _, width in ranges:
            out_shape.append(jax.ShapeDtypeStruct((rows, width), BF16))
            out_specs.append(pl.BlockSpec((rows // steps, width), lambda i: (i, 0)))
    outs = pl.pallas_call(
        functools.partial(_ffn_kernel, cast_ranges=tuple(r for _, r in casts)),
        out_shape=out_shape,
        grid=(steps,),
        in_specs=in_specs,
        out_specs=out_specs,
        compiler_params=pltpu.CompilerParams(
            dimension_semantics=("arbitrary",), vmem_limit_bytes=VMEM_LIMIT),
        name="ffn",
    )(x, mod, gain, w1, w3, w2, *[src for src, _ in casts])
    return outs[0], outs[1:]


def _mixer_core_kernel(*refs, group_rows, seq_len, use_rope, has_init, emit_state):
    it = iter(refs)
    x_ref, mod_ref, nmix_ref = next(it), next(it), next(it)
    wrqk_ref, wrv_ref, wrg_ref = next(it), next(it), next(it)
    wmqk_ref, wmv_ref, wmo_ref, wgt_ref = next(it), next(it), next(it), next(it)
    convw_ref, convb_ref, gbias_ref, decay_ref = next(it), next(it), next(it), next(it)
    rgn_ref, mgn_ref = next(it), next(it)
    if use_rope:
        cos_ref, sin_ref = next(it), next(it)
    if has_init:
        s0_ref, c0_ref, n0_ref, m0_ref = next(it), next(it), next(it), next(it)
    rg_out, hm_out = next(it), next(it)
    if emit_state:
        sfin_ref, cfin_ref, nfin_ref, mfin_ref = next(it), next(it), next(it), next(it)
    (u_s, q_s, k_s, v_s, t_s, qt_s, kv_s, wide_s, igb_s, zt_s, wt_s, dec_s, stat_s,
     mprev_s) = (next(it) for _ in range(14))

    rows_total = group_rows
    n_chunks = rows_total // CHUNK
    cps = seq_len // CHUNK
    n_seq = rows_total // seq_len
    n_rb = rows_total // ROW_BLOCK
    mod = mod_ref[0]

    def rb_rows(rb):
        return slice(rb * ROW_BLOCK, (rb + 1) * ROW_BLOCK)

    def head_cols(h, width):
        return slice(h * width, (h + 1) * width)

    chunks_per_rb = ROW_BLOCK // CHUNK
    for rb in range(n_rb):
        rows = rb_rows(rb)
        ub = _norm_mod(x_ref[rows, :], nmix_ref[...], mod[3:4], mod[4:5]).astype(BF16)
        u_s[rows, :] = ub
        qk = _dot(ub, wrqk_ref[...])
        q = qk[:, :RET_QK]
        k = qk[:, RET_QK:] * (DK_RET ** -0.5)
        for h in range(H_RET):
            cols = head_cols(h, DK_RET)
            qh, kh = q[:, cols], k[:, cols]
            if use_rope:
                cs, sn = cos_ref[rows, :], sin_ref[rows, :]
                qh = qh * cs + pltpu.roll(qh, DK_RET // 2, 1) * sn
                kh = kh * cs + pltpu.roll(kh, DK_RET // 2, 1) * sn
            q_s[rows, cols] = qh
            for ch in range(chunks_per_rb):
                t_s[rb * chunks_per_rb + ch, h] = kh[ch * CHUNK:(ch + 1) * CHUNK, :].T
        v_s[rows, :] = _dot(ub, wrv_ref[...]).astype(BF16)

    pos_i = lax.broadcasted_iota(jnp.int32, (CHUNK, CHUNK), 0).astype(F32)
    pos_j = lax.broadcasted_iota(jnp.int32, (CHUNK, CHUNK), 1).astype(F32)
    lg_rows = _log_sigmoid(decay_ref[...])
    for h in range(H_RET):
        lgf = lg_rows[h:h + 1, :CHUNK]
        lgb = lg_rows[H_RET + h:H_RET + h + 1, :CHUNK]
        dec_s[h, 0] = jnp.where(pos_i > pos_j, jnp.exp((pos_i - pos_j) * lgf),
                                jnp.where(pos_i < pos_j, jnp.exp((pos_j - pos_i) * lgb), 2.0))
        dec_s[h, 1] = jnp.exp((CHUNK - 1.0 - pos_j) * lgf)
        dec_s[h, 2] = jnp.exp(pos_j * lgb)
        dec_s[h, 3] = jnp.exp((pos_i + 1.0) * lgf)
        dec_s[h, 4] = jnp.exp((CHUNK - pos_i) * lgb)
    g_chunk = jnp.exp(CHUNK * lg_rows)

    def ret_pass1(n, carry):
        rows = pl.ds(pl.multiple_of(n * CHUNK, CHUNK), CHUNK)
        for h in range(H_RET):
            k_t = t_s[n, h]
            kcat = jnp.concatenate([k_t * dec_s[h, 1], k_t * dec_s[h, 2]], axis=0).astype(BF16)
            kv = _dot(kcat, v_s[rows, head_cols(h, DV_RET)])
            kv_s[n, 2 * h] = kv[:DK_RET]
            kv_s[n, 2 * h + 1] = kv[DK_RET:]
        return carry

    lax.fori_loop(0, n_chunks, ret_pass1, 0, unroll=CHUNK_UNROLL)

    for g in range(n_seq):
        for h in range(H_RET):
            for d in range(N_DIR):
                slot = 2 * h + d
                decay = g_chunk[d * H_RET + h:d * H_RET + h + 1, :]
                state = s0_ref[0, 0, d, h] if has_init else jnp.zeros((DK_RET, DV_RET), F32)
                order = range(g * cps, (g + 1) * cps)
                for n in (order if d == 0 else reversed(order)):
                    local = kv_s[n, slot]
                    kv_s[n, slot] = state
                    state = decay * state + local
                if emit_state:
                    sfin_ref[g, 0, d, h] = state

    def ret_pass2(n, carry):
        rows = pl.ds(pl.multiple_of(n * CHUNK, CHUNK), CHUNK)
        all_scores = [_dot(q_s[rows, head_cols(h, DK_RET)].astype(BF16), t_s[n, h].astype(BF16))
                      for h in range(H_RET)]
        for h in range(H_RET):
            qh = q_s[rows, head_cols(h, DK_RET)]
            vh = v_s[rows, head_cols(h, DV_RET)]
            sm = (all_scores[h] * dec_s[h, 0]).astype(BF16)
            qcat = jnp.concatenate([qh * dec_s[h, 3], qh * dec_s[h, 4]], axis=1).astype(BF16)
            scat = jnp.concatenate([kv_s[n, 2 * h], kv_s[n, 2 * h + 1]], axis=0).astype(BF16)
            o = _dot(sm, vh) + _dot(qcat, scat)
            wide_s[rows, head_cols(h, DV_RET)] = _head_norm(o, rgn_ref[:, head_cols(h, DV_RET)])
        return carry

    lax.fori_loop(0, n_chunks, ret_pass2, 0, unroll=CHUNK_UNROLL)

    for rb in range(n_rb):
        rows = rb_rows(rb)
        rg = _dot(u_s[rows, :], wrg_ref[...])
        rg_out[rows, :] = (wide_s[rows, :] * _silu(rg)).astype(BF16)

    for rb in range(n_rb):
        rows = rb_rows(rb)
        ub = u_s[rows, :]
        wide_s[rows, :] = _dot(ub, wmqk_ref[...])
        mv = _dot(ub, wmv_ref[...])
        for ch in range(chunks_per_rb):
            for h in range(H_M):
                t_s[rb * chunks_per_rb + ch, h] = mv[ch * CHUNK:(ch + 1) * CHUNK, head_cols(h, DH_M)].T
        glane = lax.broadcasted_iota(jnp.int32, (ROW_BLOCK, GATE_LANES), 1)
        gt = jnp.where(glane < 2 * N_PAIR, _dot(ub, wgt_ref[...]) + gbias_ref[...], 0.0)
        igb_s[rows, :] = jnp.where(glane < N_PAIR, gt, _log_sigmoid(gt))

    row_id = lax.broadcasted_iota(jnp.int32, (rows_total, CHUNK), 0)
    seq_pos = row_id & (seq_len - 1)
    for cb in range(2 * M_W // CHUNK):
        cols = slice(cb * CHUNK, (cb + 1) * CHUNK)
        xc = wide_s[:, cols]
        prev = jnp.where(seq_pos == 0, 0.0, pltpu.roll(xc, 1, 0))
        nxt = jnp.where(seq_pos == seq_len - 1, 0.0, pltpu.roll(xc, rows_total - 1, 0))
        y = convb_ref[:, cols] + convw_ref[0:1, cols] * prev
        y = y + convw_ref[1:2, cols] * xc
        y = y + convw_ref[2:3, cols] * nxt
        y = _silu(y)
        if cb < H_M:
            for n in range(n_chunks):
                qt_s[n, cb] = y[n * CHUNK:(n + 1) * CHUNK, :].T.astype(BF16)
        else:
            k_s[:, head_cols(cb - H_M, DH_M)] = y * (DH_M ** -0.5)

    z = igb_s[...]
    cpos = row_id & (CHUNK - 1)
    glane = lax.broadcasted_iota(jnp.int32, (rows_total, GATE_LANES), 1)
    pre, suf = z, z
    step = 1
    while step < CHUNK:
        pre = pre + jnp.where(cpos >= step, pltpu.roll(pre, step, 0), 0.0)
        suf = suf + jnp.where(cpos < CHUNK - step, pltpu.roll(suf, rows_total - step, 0), 0.0)
        step *= 2
    cum = jnp.where(glane < N_PAIR + H_M, pre, suf)
    i_minus_b = z - pltpu.roll(cum, GATE_LANES - N_PAIR, 1)
    igb = jnp.where(glane < N_PAIR, z,
                    jnp.where(glane < 2 * N_PAIR, cum, pltpu.roll(i_minus_b, 2 * N_PAIR, 1)))
    igb_s[...] = igb

    for n in range(n_chunks):
        zt_s[n] = igb[n * CHUNK:(n + 1) * CHUNK, :].T[0:3 * N_PAIR, :]
    ig_all = zt_s[:, 0:N_PAIR, :]
    b_all = zt_s[:, N_PAIR:2 * N_PAIR, :]
    pair = lax.broadcasted_iota(jnp.int32, (n_chunks, N_PAIR, 1), 1)
    b_last = jnp.where(pair < H_M, b_all[:, :, CHUNK - 1:CHUNK], b_all[:, :, 0:1])
    g_all = (b_last - b_all) + ig_all
    m_loc = jnp.max(g_all, axis=-1, keepdims=True)
    wt_s[...] = jnp.exp(g_all - m_loc)
    stat_s[:, 0] = jnp.broadcast_to(b_last, (n_chunks, N_PAIR, CHUNK))
    stat_s[:, 1] = jnp.broadcast_to(m_loc, (n_chunks, N_PAIR, CHUNK))

    def m_pass1(n, carry):
        rows = pl.ds(pl.multiple_of(n * CHUNK, CHUNK), CHUNK)
        w_t = wt_s[n]
        for h in range(H_M):
            kh = k_s[rows, head_cols(h, DH_M)].astype(BF16)
            v_t = t_s[n, h]
            parts = []
            for d in range(N_DIR):
                w_row = w_t[d * H_M + h:d * H_M + h + 1, :]
                parts += [v_t * w_row, jnp.broadcast_to(w_row, (NORM_ROWS, CHUNK))]
            local = _dot(jnp.concatenate(parts, axis=0).astype(BF16), kh)
            for d in range(N_DIR):
                c = d * H_M + h
                base = d * (DH_M + NORM_ROWS)
                kv_s[n, c, :, :DH_M] = local[base:base + DH_M]
                kv_s[n, c, 0:NORM_ROWS, DH_M:] = local[base + DH_M:base + DH_M + NORM_ROWS]
        return carry

    lax.fori_loop(0, n_chunks, m_pass1, 0, unroll=CHUNK_UNROLL)

    for g in range(n_seq):
        for d in range(N_DIR):
            order = list(range(g * cps, (g + 1) * cps))
            if d == 1:
                order.reverse()
            m_prev = m0_ref[0] if has_init else jnp.zeros((N_PAIR, CHUNK), F32)
            a1s, a2s = [], []
            for n in order:
                b_last_n, m_loc_n = stat_s[n, 0], stat_s[n, 1]
                m_new = jnp.maximum(b_last_n + m_prev, m_loc_n)
                a1s.append(jnp.exp(b_last_n + m_prev - m_new))
                a2s.append(jnp.exp(m_loc_n - m_new))
                mprev_s[n, d] = m_prev
                m_prev = m_new
            if emit_state:
                mfin_ref[g, d] = m_prev
            for h in range(H_M):
                c = d * H_M + h
                if has_init:
                    mem = c0_ref[0, 0, d, h]
                    nrm = jnp.broadcast_to(n0_ref[0][c:c + 1, :], (NORM_ROWS, DH_M))
                else:
                    mem = jnp.zeros((DH_M, DH_M), F32)
                    nrm = jnp.zeros((NORM_ROWS, DH_M), F32)
                for idx, n in enumerate(order):
                    a1 = a1s[idx][c:c + 1, :]
                    a2 = a2s[idx][c:c + 1, :]
                    local_mem = kv_s[n, c, :, :DH_M]
                    local_nrm = kv_s[n, c, 0:NORM_ROWS, DH_M:]
                    kv_s[n, c, :, :DH_M] = mem
                    kv_s[n, c, 0:NORM_ROWS, DH_M:] = nrm
                    mem = a1 * mem + a2 * local_mem
                    nrm = a1 * nrm + a2 * local_nrm
                if emit_state:
                    cfin_ref[g, 0, d, h] = mem
                    nfin_ref[g, c:c + 1, :] = nrm[0:1]

    tri_j = lax.broadcasted_iota(jnp.int32, (CHUNK, CHUNK), 0)
    tri_i = lax.broadcasted_iota(jnp.int32, (CHUNK, CHUNK), 1)
    mem_rows = DH_M + NORM_ROWS

    def m_pass2(n, carry):
        rows = pl.ds(pl.multiple_of(n * CHUNK, CHUNK), CHUNK)
        zc = igb_s[rows, :]
        zt = zt_s[n]
        bigs = []
        for h in range(H_M):
            kh = k_s[rows, head_cols(h, DH_M)].astype(BF16)
            mems = [jnp.concatenate([kv_s[n, d * H_M + h, :, :DH_M], kv_s[n, d * H_M + h, 0:NORM_ROWS, DH_M:]],
                                    axis=0).astype(BF16) for d in range(N_DIR)]
            bigs.append(_dot(jnp.concatenate([kh] + mems, axis=0), qt_s[n, h]))
        for h in range(H_M):
            big = bigs[h]
            scores_t = big[:CHUNK]
            weights, crosses, w_ints, m_is = [], [], [], []
            for d in range(N_DIR):
                c = d * H_M + h
                a_col = zc[:, 2 * N_PAIR + c:2 * N_PAIR + c + 1]
                b_row = zt[N_PAIR + c:N_PAIR + c + 1, :]
                mask = (tri_j <= tri_i) if d == 0 else (tri_j >= tri_i)
                dlog = jnp.where(mask, b_row + a_col, -jnp.inf)
                lin = b_row + mprev_s[n, d][c:c + 1, :]
                m_i = jnp.maximum(jnp.max(dlog, axis=0, keepdims=True), lin)
                weights.append(scores_t * jnp.exp(dlog - m_i))
                crosses.append(big[CHUNK + d * mem_rows:CHUNK + (d + 1) * mem_rows])
                w_ints.append(jnp.exp(lin - m_i))
                m_is.append(m_i)
            intra = _dot(t_s[n, h].astype(BF16), jnp.concatenate(weights, axis=1).astype(BF16))
            h_sum_t = jnp.zeros((DH_M, CHUNK), F32)
            for d in range(N_DIR):
                cross, w_int = crosses[d], w_ints[d]
                num = intra[:, d * CHUNK:(d + 1) * CHUNK] + cross[:DH_M] * w_int
                den = jnp.sum(weights[d], axis=0, keepdims=True) + cross[DH_M:DH_M + 1] * w_int
                denom = jnp.maximum(jnp.abs(den), jnp.exp(-m_is[d]))
                h_sum_t = h_sum_t + num * (1.0 / denom)
            wide_s[rows, head_cols(h, DH_M)] = h_sum_t.T
        return carry

    lax.fori_loop(0, n_chunks, m_pass2, 0, unroll=CHUNK_UNROLL)

    for rb in range(n_rb):
        rows = rb_rows(rb)
        mo = _dot(u_s[rows, :], wmo_ref[...])
        hm = _sigmoid(mo) * wide_s[rows, :M_W]
        for h in range(H_M):
            cols = head_cols(h, DH_M)
            hm_out[rows, cols] = _head_norm(hm[:, cols], mgn_ref[:, cols]).astype(BF16)


def _mixer_core_call(x, mod, mod_row, p, seq_len, group_rows, rope, init):
    t = x.shape[0]
    n_groups = t // group_rows
    n_chunks = group_rows // CHUNK
    seq_per_group = group_rows // seq_len
    use_rope = rope is not None
    has_init = init is not None
    emit_state = not has_init

    w_in = p["w_in"]
    groups = []
    start = 0
    for width in (2 * RET_QK, RET_V, RET_V, 2 * M_W, M_W, M_W, GATE_LANES):
        assert start % width == 0
        groups.append(pl.BlockSpec((D_MODEL, width), functools.partial(lambda i, c: (0, c), c=start // width)))
        start += width
    small = [p["conv_w"], p["conv_b"], p["gate_bias"], p["decay_rows"], p["ret_gn"], p["m_gn"]]
    inputs = [x, mod, p["norm_mix"]] + [w_in] * len(groups) + small
    in_specs = [
        pl.BlockSpec((group_rows, D_MODEL), lambda i: (i, 0)),
        pl.BlockSpec((1, N_MOD, D_MODEL), lambda i: (mod_row(i * group_rows), 0, 0)),
        _const_spec(p["norm_mix"].shape),
    ] + groups + [_const_spec(a.shape) for a in small]
    if use_rope:
        inputs += [rope[0], rope[1]]
        in_specs += [_const_spec(rope[0].shape), _const_spec(rope[1].shape)]
    if has_init:
        s0, c0, n0, m0 = init
        inputs += [s0, c0, n0, m0]
        in_specs += [
            pl.BlockSpec((1, 1, N_DIR, H_RET, DK_RET, DV_RET), lambda i: (i, 0, 0, 0, 0, 0)),
            pl.BlockSpec((1, 1, N_DIR, H_M, DH_M, DH_M), lambda i: (i, 0, 0, 0, 0, 0)),
            pl.BlockSpec((1, N_PAIR, DH_M), lambda i: (i, 0, 0)),
            pl.BlockSpec((1, N_PAIR, CHUNK), lambda i: (i, 0, 0)),
        ]

    out_shape = [jax.ShapeDtypeStruct((t, RET_V), BF16), jax.ShapeDtypeStruct((t, M_W), BF16)]
    out_specs = [pl.BlockSpec((group_rows, RET_V), lambda i: (i, 0)),
                 pl.BlockSpec((group_rows, M_W), lambda i: (i, 0))]
    if emit_state:
        n_seq = t // seq_len
        out_shape += [
            jax.ShapeDtypeStruct((n_seq, 1, N_DIR, H_RET, DK_RET, DV_RET), F32),
            jax.ShapeDtypeStruct((n_seq, 1, N_DIR, H_M, DH_M, DH_M), F32),
            jax.ShapeDtypeStruct((n_seq, N_PAIR, DH_M), F32),
            jax.ShapeDtypeStruct((n_seq, N_DIR, N_PAIR, CHUNK), F32),
        ]
        out_specs += [
            pl.BlockSpec((seq_per_group, 1, N_DIR, H_RET, DK_RET, DV_RET), lambda i: (i, 0, 0, 0, 0, 0)),
            pl.BlockSpec((seq_per_group, 1, N_DIR, H_M, DH_M, DH_M), lambda i: (i, 0, 0, 0, 0, 0)),
            pl.BlockSpec((seq_per_group, N_PAIR, DH_M), lambda i: (i, 0, 0)),
            pl.BlockSpec((seq_per_group, N_DIR, N_PAIR, CHUNK), lambda i: (i, 0, 0, 0)),
        ]

    scratch = [
        pltpu.VMEM((group_rows, D_MODEL), BF16),
        pltpu.VMEM((group_rows, RET_QK), F32),
        pltpu.VMEM((group_rows, RET_QK), F32),
        pltpu.VMEM((group_rows, RET_V), BF16),
        pltpu.VMEM((n_chunks, H_M, DH_M, CHUNK), F32),
        pltpu.VMEM((n_chunks, H_M, DH_M, CHUNK), BF16),
        pltpu.VMEM((n_chunks, N_PAIR, DK_RET, DV_RET), F32),
        pltpu.VMEM((group_rows, RET_V), F32),
        pltpu.VMEM((group_rows, GATE_LANES), F32),
        pltpu.VMEM((n_chunks, 3 * N_PAIR, CHUNK), F32),
        pltpu.VMEM((n_chunks, N_PAIR, CHUNK), F32),
        pltpu.VMEM((H_RET, 5, CHUNK, CHUNK), F32),
        pltpu.VMEM((n_chunks, 2, N_PAIR, CHUNK), F32),
        pltpu.VMEM((n_chunks, N_DIR, N_PAIR, CHUNK), F32),
    ]
    kern = functools.partial(_mixer_core_kernel, group_rows=group_rows, seq_len=seq_len, use_rope=use_rope,
                             has_init=has_init, emit_state=emit_state)
    return pl.pallas_call(
        kern,
        out_shape=out_shape,
        grid=(n_groups,),
        in_specs=in_specs,
        out_specs=out_specs,
        scratch_shapes=scratch,
        compiler_params=pltpu.CompilerParams(
            dimension_semantics=("arbitrary",), vmem_limit_bytes=VMEM_LIMIT),
        name="mixer_core",
    )(*inputs)


def _mixer_out_kernel(x_ref, rg_ref, hm_ref, mod_ref, nmix_ref, wbg_ref, wru_ref, wmu_ref, wout_ref,
                      nffn_ref, w1_ref, w3_ref, w2_ref, nfin_ref, o_ref):
    mod = mod_ref[0]
    x = x_ref[...]
    u = _norm_mod(x, nmix_ref[...], mod[3:4], mod[4:5]).astype(BF16)
    gates = _sigmoid(_dot(u, wbg_ref[...]))
    ret_branch = _dot(rg_ref[...], wru_ref[...])
    m_branch = _dot(hm_ref[...], wmu_ref[...])
    merged = (gates[:, :D_MODEL] * ret_branch + gates[:, D_MODEL:] * m_branch).astype(BF16)
    x = x + mod[5:6] * _dot(merged, wout_ref[...])
    x = _swiglu_half_step(x, mod, 6, nffn_ref[...], w1_ref, w3_ref, w2_ref)
    ms = jnp.mean(x * x, axis=-1, keepdims=True)
    o_ref[...] = x * lax.rsqrt(ms + EPS) * nfin_ref[...]


def _mixer_out_call(x, rgated, hmn, mod, mod_row, p):
    t = x.shape[0]
    tm = FFN_TILE
    consts = [p["norm_mix"], p["w_bg"], p["w_ret_up"], p["w_m_up"], p["w_out"],
              p["norm_ffn2"], p["w1_ffn2"], p["w3_ffn2"], p["w2_ffn2"], p["norm_final"]]
    return pl.pallas_call(
        _mixer_out_kernel,
        out_shape=jax.ShapeDtypeStruct((t, D_MODEL), F32),
        grid=(t // tm,),
        in_specs=[
            pl.BlockSpec((tm, D_MODEL), lambda i: (i, 0)),
            pl.BlockSpec((tm, RET_V), lambda i: (i, 0)),
            pl.BlockSpec((tm, M_W), lambda i: (i, 0)),
            pl.BlockSpec((1, N_MOD, D_MODEL), lambda i: (mod_row(i * tm), 0, 0)),
        ] + [_const_spec(a.shape) for a in consts],
        out_specs=pl.BlockSpec((tm, D_MODEL), lambda i: (i, 0)),
        compiler_params=pltpu.CompilerParams(
            dimension_semantics=("arbitrary",), vmem_limit_bytes=VMEM_LIMIT),
        name="mixer_out",
    )(x, rgated, hmn, mod, *consts)


def _rope_tables(seq_len):
    rows = seq_len // GRID_W
    r = jnp.repeat(jnp.arange(rows, dtype=F32), GRID_W)
    col = (jnp.arange(seq_len) % GRID_W).astype(F32)
    n_f = DK_RET // 4
    freqs = ROPE_BASE ** (-jnp.arange(n_f, dtype=F32) / n_f)
    ang = jnp.concatenate([r[:, None] * freqs, col[:, None] * freqs], axis=-1)
    cos, sin = jnp.cos(ang), jnp.sin(ang)
    return jnp.concatenate([cos, cos], axis=-1), jnp.concatenate([-sin, sin], axis=-1)


def _layer_params(l, norm_ffn1, w1_ffn1, w3_ffn1, w2_ffn1, norm_mix, conv_w, conv_b,
                  ret_decay_logit, b_igate, b_fgate, ret_gn, m_gn, norm_ffn2, norm_final):
    gate_bias = jnp.pad(jnp.concatenate([b_igate[l], b_fgate[l]]), (0, GATE_LANES - 2 * N_PAIR))
    p = dict(
        norm_ffn1=norm_ffn1[l][None, :], norm_mix=norm_mix[l][None, :], norm_ffn2=norm_ffn2[l][None, :],
        norm_final=norm_final[None, :],
        w1_ffn1=w1_ffn1[l].astype(BF16), w3_ffn1=w3_ffn1[l].astype(BF16), w2_ffn1=w2_ffn1[l].astype(BF16),
        conv_w=conv_w[l], conv_b=conv_b[l][None, :],
        gate_bias=gate_bias[None, :],
        decay_rows=jnp.broadcast_to(ret_decay_logit[l].reshape(N_DIR * H_RET, 1), (N_DIR * H_RET, DV_RET)),
        ret_gn=ret_gn[l][None, :], m_gn=m_gn[l][None, :],
    )
    return p


def kernel(x_prompt, x_sample, c, state_ret, state_mlstm_C, state_mlstm_n, state_mlstm_m, c_ctx, w_ada, b_ada, norm_ffn1, w1_ffn1, w3_ffn1, w2_ffn1, norm_mix, w_in, conv_w, conv_b, ret_decay_logit, b_igate, b_fgate, ret_gn, m_gn, w_ret_up, w_m_up, w_out, norm_ffn2, w1_ffn2, w3_ffn2, w2_ffn2, norm_final):
    bp, lp, _ = x_prompt.shape
    bs, ls, _ = x_sample.shape
    depth = w_ada.shape[0]
    assert depth == 1 and lp % CHUNK == 0 and PROMPT_GROUP_ROWS % lp == 0 and ls <= MAX_GROUP_ROWS
    assert lp & (lp - 1) == 0 and ls & (ls - 1) == 0

    ctx_row = 8 * ((bs + 7) // 8)
    cvec = jnp.zeros((ctx_row + 8, D_MODEL), F32).at[:bs].set(c).at[ctx_row].set(c_ctx)
    rope = _rope_tables(ls)

    xp = x_prompt.reshape(bp * lp, D_MODEL)
    xs = x_sample.reshape(bs * ls, D_MODEL)
    prompt_row = lambda r: ctx_row
    sample_row = lambda r: r // ls

    l = 0
    p = _layer_params(l, norm_ffn1, w1_ffn1, w3_ffn1, w2_ffn1, norm_mix, conv_w, conv_b,
                      ret_decay_logit, b_igate, b_fgate, ret_gn, m_gn, norm_ffn2, norm_final)
    mod = _ada_call(cvec, w_ada[l], b_ada[l][None, :]).reshape(ctx_row + 8, N_MOD, D_MODEL)

    n0 = state_mlstm_n[:, l].astype(F32).reshape(bs, N_PAIR, DH_M)
    m0 = jnp.broadcast_to(state_mlstm_m[:, l].astype(F32).reshape(bs, N_PAIR, 1), (bs, N_PAIR, CHUNK))
    init = (state_ret.astype(F32), state_mlstm_C.astype(F32), n0, m0)

    whole = lambda a: (a, ((0, a.shape[1]),))
    bg_start = D_IN - N_DIR * D_MODEL
    xp1, (p["w1_ffn2"], p["w3_ffn2"], p["w2_ffn2"]) = _ffn_call(
        xp, mod, prompt_row, p["norm_ffn1"], p["w1_ffn1"], p["w3_ffn1"], p["w2_ffn1"],
        casts=(whole(w1_ffn2[l]), whole(w3_ffn2[l]), whole(w2_ffn2[l])))
    xs1, (p["w_in"], p["w_bg"], p["w_ret_up"], p["w_m_up"], p["w_out"]) = _ffn_call(
        xs, mod, sample_row, p["norm_ffn1"], p["w1_ffn1"], p["w3_ffn1"], p["w2_ffn1"],
        casts=((w_in[l], ((0, D_IN), (bg_start, N_DIR * D_MODEL))),
               whole(w_ret_up[l]), whole(w_m_up[l]), whole(w_out[l])))

    rg_p, hm_p, s_fin, c_fin, n_fin, m_fin = _mixer_core_call(xp1, mod, prompt_row, p, lp, PROMPT_GROUP_ROWS, None, None)
    rg_s, hm_s = _mixer_core_call(xs1, mod, sample_row, p, ls, ls, rope, init)

    yp = _mixer_out_call(xp1, rg_p, hm_p, mod, prompt_row, p)
    ys = _mixer_out_call(xs1, rg_s, hm_s, mod, sample_row, p)

    dt = x_prompt.dtype
    new_c = c_fin
    new_n = n_fin.reshape(bp, 1, N_DIR, H_M, DH_M)
    m_diag = jnp.stack([m_fin[:, d, d * H_M:(d + 1) * H_M, 0] for d in range(N_DIR)], axis=1)
    new_m = m_diag.reshape(bp, 1, N_DIR, H_M)
    return (yp.reshape(bp, lp, D_MODEL).astype(dt), ys.reshape(bs, ls, D_MODEL).astype(dt),
            s_fin.astype(dt), new_c.astype(dt), new_n.astype(dt), new_m.astype(dt))
```

```python
import functools

import jax
import jax.numpy as jnp
from jax import lax
from jax.experimental import pallas as pl
from jax.experimental.pallas import tpu as pltpu

F32 = jnp.float32
BF16 = jnp.bfloat16

D_MODEL = 1024
D_FF = 2816
CHUNK = 128
N_DIR = 2
H_RET = 4
DK_RET = 128
DV_RET = 256
H_M = 4
DH_M = 128
RET_QK = H_RET * DK_RET
RET_V = H_RET * DV_RET
M_W = H_M * DH_M
D_IN = 2 * RET_QK + 2 * RET_V + 4 * M_W + 2 * N_DIR * H_M + N_DIR * D_MODEL
GRID_W = 64
ROPE_BASE = 10000.0
EPS = 1e-6
GN_EPS = 1e-5
N_MOD = 9
N_PAIR = N_DIR * H_M
GATE_LANES = 128
LANES = 128
BF16_SUBLANES = 16
NORM_ROWS = BF16_SUBLANES

MAX_GROUP_ROWS = 1024
PROMPT_GROUP_ROWS = 512
ROW_BLOCK = 256
CHUNK_UNROLL = 2
FFN_TILE = 512
VMEM_LIMIT = 60 * 1024 * 1024


def _sigmoid(x):
    return 1.0 / (1.0 + jnp.exp(-x))


def _silu(x):
    return x * _sigmoid(x)


def _log_sigmoid(x):
    return -(jnp.maximum(-x, 0.0) + jnp.log1p(jnp.exp(-jnp.abs(x))))


def _norm_mod(x, gain, shift, scale):
    ms = jnp.mean(x * x, axis=-1, keepdims=True)
    y = x * lax.rsqrt(ms + EPS) * gain
    return y * (1.0 + scale) + shift


def _dot(a, b):
    return jnp.dot(a, b, preferred_element_type=F32)


def _dot_nt(a, b):
    return lax.dot_general(a, b, (((1,), (1,)), ((), ())), preferred_element_type=F32)


def _head_norm(o, gain):
    mu = jnp.mean(o, axis=-1, keepdims=True)
    oc = o - mu
    var = jnp.mean(oc * oc, axis=-1, keepdims=True)
    return oc * lax.rsqrt(var + GN_EPS) * gain


def _ada_kernel(c_ref, w_ref, b_ref, o_ref):
    s = _silu(c_ref[...]).astype(BF16)
    o_ref[...] = _dot(s, w_ref[...].astype(BF16)) + b_ref[...]


def _ada_call(cvec, w_ada, b_ada):
    rows = cvec.shape[0]
    n_out = w_ada.shape[1]
    tile = D_MODEL
    return pl.pallas_call(
        _ada_kernel,
        out_shape=jax.ShapeDtypeStruct((rows, n_out), F32),
        grid=(n_out // tile,),
        in_specs=[
            pl.BlockSpec((rows, D_MODEL), lambda j: (0, 0)),
            pl.BlockSpec((D_MODEL, tile), lambda j: (0, j)),
            pl.BlockSpec((1, tile), lambda j: (0, j)),
        ],
        out_specs=pl.BlockSpec((rows, tile), lambda j: (0, j)),
        compiler_params=pltpu.CompilerParams(
            dimension_semantics=("arbitrary",), vmem_limit_bytes=VMEM_LIMIT),
        name="ada",
    )(cvec, w_ada, b_ada)


def _swiglu_half_step(x, mod, k0, gain, w1_ref, w3_ref, w2_ref):
    u = _norm_mod(x, gain, mod[k0:k0 + 1], mod[k0 + 1:k0 + 2]).astype(BF16)
    h1 = _dot(u, w1_ref[...])
    h3 = _dot(u, w3_ref[...])
    h = (_silu(h1) * h3).astype(BF16)
    y = _dot(h, w2_ref[...])
    return x + 0.5 * mod[k0 + 2:k0 + 3] * y


def _ffn_kernel(*refs, cast_plan):
    n_in = 6 + len(cast_plan)
    x_ref, mod_ref, g_ref, w1_ref, w3_ref, w2_ref = refs[:6]
    o_ref = refs[n_in]
    o_ref[...] = _swiglu_half_step(x_ref[...], mod_ref[0], 0, g_ref[...], w1_ref, w3_ref, w2_ref)
    cast_out = iter(refs[n_in + 1:])
    for src_ref, (transposed, ranges) in zip(refs[6:n_in], cast_plan):
        block = src_ref[...].T if transposed else src_ref[...]
        for start, width in ranges:
            next(cast_out)[...] = block[:, start:start + width].astype(BF16)


def _const_spec(shape):
    nd = len(shape)
    return pl.BlockSpec(shape, lambda i: (0,) * nd)


def _ffn_call(x, mod, mod_row, gain, w1, w3, w2, casts=()):
    t = x.shape[0]
    tm = FFN_TILE
    steps = t // tm
    in_specs = [
        pl.BlockSpec((tm, D_MODEL), lambda i: (i, 0)),
        pl.BlockSpec((1, N_MOD, D_MODEL), lambda i: (mod_row(i * tm), 0, 0)),
        _const_spec((1, D_MODEL)),
        _const_spec((D_MODEL, D_FF)),
        _const_spec((D_MODEL, D_FF)),
        _const_spec((D_FF, D_MODEL)),
    ]
    out_shape = [jax.ShapeDtypeStruct((t, D_MODEL), F32)]
    out_specs = [pl.BlockSpec((tm, D_MODEL), lambda i: (i, 0))]
    for src, transposed, ranges in casts:
        rows, cols = src.shape[::-1] if transposed else src.shape
        if transposed:
            assert rows % (steps * LANES) == 0
            in_specs.append(pl.BlockSpec((cols, rows // steps), lambda i: (0, i)))
        else:
            assert rows % (steps * BF16_SUBLANES) == 0
            in_specs.append(pl.BlockSpec((rows // steps, cols), lambda i: (i, 0)))
        for _, width in ranges:
            out_shape.append(jax.ShapeDtypeStruct((rows, width), BF16))
            out_specs.append(pl.BlockSpec((rows // steps, width), lambda i: (i, 0)))
    outs = pl.pallas_call(
        functools.partial(_ffn_kernel, cast_plan=tuple((tr, r) for _, tr, r in casts)),
        out_shape=out_shape,
        grid=(steps,),
        in_specs=in_specs,
        out_specs=out_specs,
        compiler_params=pltpu.CompilerParams(
            dimension_semantics=("arbitrary",), vmem_limit_bytes=VMEM_LIMIT),
        name="ffn",
    )(x, mod, gain, w1, w3, w2, *[src for src, _, _ in casts])
    return outs[0], outs[1:]


def _mixer_core_kernel(*refs, group_rows, seq_len, use_rope, has_init, emit_state):
    it = iter(refs)
    x_ref, mod_ref, nmix_ref = next(it), next(it), next(it)
    wrqk_ref, wrv_ref, wrg_ref = next(it), next(it), next(it)
    wmqk_ref, wmv_ref, wmo_ref, wgt_ref = next(it), next(it), next(it), next(it)
    convw_ref, convb_ref, gbias_ref, decay_ref = next(it), next(it), next(it), next(it)
    rgn_ref, mgn_ref = next(it), next(it)
    if use_rope:
        cos_ref, sin_ref = next(it), next(it)
    if has_init:
        s0_ref, c0_ref, n0_ref, m0_ref = next(it), next(it), next(it), next(it)
    rg_out, hm_out = next(it), next(it)
    if emit_state:
        sfin_ref, cfin_ref, nfin_ref, mfin_ref = next(it), next(it), next(it), next(it)
    (u_s, q_s, k_s, v_s, t_s, qt_s, kv_s, wide_s, igb_s, zt_s, wt_s, dec_s, stat_s,
     mprev_s) = (next(it) for _ in range(14))

    rows_total = group_rows
    n_chunks = rows_total // CHUNK
    cps = seq_len // CHUNK
    n_seq = rows_total // seq_len
    n_rb = rows_total // ROW_BLOCK
    mod = mod_ref[0]

    def rb_rows(rb):
        return slice(rb * ROW_BLOCK, (rb + 1) * ROW_BLOCK)

    def head_cols(h, width):
        return slice(h * width, (h + 1) * width)

    chunks_per_rb = ROW_BLOCK // CHUNK
    for rb in range(n_rb):
        rows = rb_rows(rb)
        ub = _norm_mod(x_ref[rows, :], nmix_ref[...], mod[3:4], mod[4:5]).astype(BF16)
        u_s[rows, :] = ub
        qk = _dot(ub, wrqk_ref[...])
        q = qk[:, :RET_QK]
        k = qk[:, RET_QK:] * (DK_RET ** -0.5)
        for h in range(H_RET):
            cols = head_cols(h, DK_RET)
            qh, kh = q[:, cols], k[:, cols]
            if use_rope:
                cs, sn = cos_ref[rows, :], sin_ref[rows, :]
                qh = qh * cs + pltpu.roll(qh, DK_RET // 2, 1) * sn
                kh = kh * cs + pltpu.roll(kh, DK_RET // 2, 1) * sn
            q_s[rows, cols] = qh
            for ch in range(chunks_per_rb):
                t_s[rb * chunks_per_rb + ch, h] = kh[ch * CHUNK:(ch + 1) * CHUNK, :].T
        v_s[rows, :] = _dot(ub, wrv_ref[...]).astype(BF16)

    pos_i = lax.broadcasted_iota(jnp.int32, (CHUNK, CHUNK), 0).astype(F32)
    pos_j = lax.broadcasted_iota(jnp.int32, (CHUNK, CHUNK), 1).astype(F32)
    lg_rows = _log_sigmoid(decay_ref[...])
    for h in range(H_RET):
        lgf = lg_rows[h:h + 1, :CHUNK]
        lgb = lg_rows[H_RET + h:H_RET + h + 1, :CHUNK]
        dec_s[h, 0] = jnp.where(pos_i > pos_j, jnp.exp((pos_i - pos_j) * lgf),
                                jnp.where(pos_i < pos_j, jnp.exp((pos_j - pos_i) * lgb), 2.0))
        dec_s[h, 1] = jnp.exp((CHUNK - 1.0 - pos_j) * lgf)
        dec_s[h, 2] = jnp.exp(pos_j * lgb)
        dec_s[h, 3] = jnp.exp((pos_i + 1.0) * lgf)
        dec_s[h, 4] = jnp.exp((CHUNK - pos_i) * lgb)
    g_chunk = jnp.exp(CHUNK * lg_rows)

    def ret_pass1(n, carry):
        rows = pl.ds(pl.multiple_of(n * CHUNK, CHUNK), CHUNK)
        for h in range(H_RET):
            k_t = t_s[n, h]
            kcat = jnp.concatenate([k_t * dec_s[h, 1], k_t * dec_s[h, 2]], axis=0).astype(BF16)
            kv = _dot(kcat, v_s[rows, head_cols(h, DV_RET)])
            kv_s[n, 2 * h] = kv[:DK_RET]
            kv_s[n, 2 * h + 1] = kv[DK_RET:]
        return carry

    lax.fori_loop(0, n_chunks, ret_pass1, 0, unroll=CHUNK_UNROLL)

    for g in range(n_seq):
        for h in range(H_RET):
            for d in range(N_DIR):
                slot = 2 * h + d
                decay = g_chunk[d * H_RET + h:d * H_RET + h + 1, :]
                state = s0_ref[0, 0, d, h] if has_init else jnp.zeros((DK_RET, DV_RET), F32)
                order = range(g * cps, (g + 1) * cps)
                for n in (order if d == 0 else reversed(order)):
                    local = kv_s[n, slot]
                    kv_s[n, slot] = state
                    state = decay * state + local
                if emit_state:
                    sfin_ref[g, 0, d, h] = state

    def ret_pass2(n, carry):
        rows = pl.ds(pl.multiple_of(n * CHUNK, CHUNK), CHUNK)
        all_scores = [_dot(q_s[rows, head_cols(h, DK_RET)].astype(BF16), t_s[n, h].astype(BF16))
                      for h in range(H_RET)]
        for h in range(H_RET):
            qh = q_s[rows, head_cols(h, DK_RET)]
            vh = v_s[rows, head_cols(h, DV_RET)]
            sm = (all_scores[h] * dec_s[h, 0]).astype(BF16)
            qcat = jnp.concatenate([qh * dec_s[h, 3], qh * dec_s[h, 4]], axis=1).astype(BF16)
            scat = jnp.concatenate([kv_s[n, 2 * h], kv_s[n, 2 * h + 1]], axis=0).astype(BF16)
            o = _dot(sm, vh) + _dot(qcat, scat)
            wide_s[rows, head_cols(h, DV_RET)] = _head_norm(o, rgn_ref[:, head_cols(h, DV_RET)])
        return carry

    lax.fori_loop(0, n_chunks, ret_pass2, 0, unroll=CHUNK_UNROLL)

    for rb in range(n_rb):
        rows = rb_rows(rb)
        rg = _dot(u_s[rows, :], wrg_ref[...])
        rg_out[rows, :] = (wide_s[rows, :] * _silu(rg)).astype(BF16)

    for rb in range(n_rb):
        rows = rb_rows(rb)
        ub = u_s[rows, :]
        wide_s[rows, :] = _dot(ub, wmqk_ref[...])
        mv = _dot(ub, wmv_ref[...])
        for ch in range(chunks_per_rb):
            for h in range(H_M):
                t_s[rb * chunks_per_rb + ch, h] = mv[ch * CHUNK:(ch + 1) * CHUNK, head_cols(h, DH_M)].T
        glane = lax.broadcasted_iota(jnp.int32, (ROW_BLOCK, GATE_LANES), 1)
        gt = jnp.where(glane < 2 * N_PAIR, _dot(ub, wgt_ref[...]) + gbias_ref[...], 0.0)
        igb_s[rows, :] = jnp.where(glane < N_PAIR, gt, _log_sigmoid(gt))

    row_id = lax.broadcasted_iota(jnp.int32, (rows_total, CHUNK), 0)
    seq_pos = row_id & (seq_len - 1)
    for cb in range(2 * M_W // CHUNK):
        cols = slice(cb * CHUNK, (cb + 1) * CHUNK)
        xc = wide_s[:, cols]
        prev = jnp.where(seq_pos == 0, 0.0, pltpu.roll(xc, 1, 0))
        nxt = jnp.where(seq_pos == seq_len - 1, 0.0, pltpu.roll(xc, rows_total - 1, 0))
        y = convb_ref[:, cols] + convw_ref[0:1, cols] * prev
        y = y + convw_ref[1:2, cols] * xc
        y = y + convw_ref[2:3, cols] * nxt
        y = _silu(y)
        if cb < H_M:
            for n in range(n_chunks):
                qt_s[n, cb] = y[n * CHUNK:(n + 1) * CHUNK, :].T.astype(BF16)
        else:
            k_s[:, head_cols(cb - H_M, DH_M)] = y * (DH_M ** -0.5)

    z = igb_s[...]
    cpos = row_id & (CHUNK - 1)
    glane = lax.broadcasted_iota(jnp.int32, (rows_total, GATE_LANES), 1)
    pre, suf = z, z
    step = 1
    while step < CHUNK:
        pre = pre + jnp.where(cpos >= step, pltpu.roll(pre, step, 0), 0.0)
        suf = suf + jnp.where(cpos < CHUNK - step, pltpu.roll(suf, rows_total - step, 0), 0.0)
        step *= 2
    cum = jnp.where(glane < N_PAIR + H_M, pre, suf)
    i_minus_b = z - pltpu.roll(cum, GATE_LANES - N_PAIR, 1)
    igb = jnp.where(glane < N_PAIR, z,
                    jnp.where(glane < 2 * N_PAIR, cum, pltpu.roll(i_minus_b, 2 * N_PAIR, 1)))
    igb_s[...] = igb

    for n in range(n_chunks):
        zt_s[n] = igb[n * CHUNK:(n + 1) * CHUNK, :].T[0:3 * N_PAIR, :]
    ig_all = zt_s[:, 0:N_PAIR, :]
    b_all = zt_s[:, N_PAIR:2 * N_PAIR, :]
    pair = lax.broadcasted_iota(jnp.int32, (n_chunks, N_PAIR, 1), 1)
    b_last = jnp.where(pair < H_M, b_all[:, :, CHUNK - 1:CHUNK], b_all[:, :, 0:1])
    g_all = (b_last - b_all) + ig_all
    m_loc = jnp.max(g_all, axis=-1, keepdims=True)
    wt_s[...] = jnp.exp(g_all - m_loc)
    stat_s[:, 0] = jnp.broadcast_to(b_last, (n_chunks, N_PAIR, CHUNK))
    stat_s[:, 1] = jnp.broadcast_to(m_loc, (n_chunks, N_PAIR, CHUNK))

    def m_pass1(n, carry):
        rows = pl.ds(pl.multiple_of(n * CHUNK, CHUNK), CHUNK)
        w_t = wt_s[n]
        for h in range(H_M):
            kh = k_s[rows, head_cols(h, DH_M)].astype(BF16)
            v_t = t_s[n, h]
            parts = []
            for d in range(N_DIR):
                w_row = w_t[d * H_M + h:d * H_M + h + 1, :]
                parts += [v_t * w_row, jnp.broadcast_to(w_row, (NORM_ROWS, CHUNK))]
            local = _dot(jnp.concatenate(parts, axis=0).astype(BF16), kh)
            for d in range(N_DIR):
                c = d * H_M + h
                base = d * (DH_M + NORM_ROWS)
                kv_s[n, c, :, :DH_M] = local[base:base + DH_M]
                kv_s[n, c, 0:NORM_ROWS, DH_M:] = local[base + DH_M:base + DH_M + NORM_ROWS]
        return carry

    lax.fori_loop(0, n_chunks, m_pass1, 0, unroll=CHUNK_UNROLL)

    for g in range(n_seq):
        for d in range(N_DIR):
            order = list(range(g * cps, (g + 1) * cps))
            if d == 1:
                order.reverse()
            m_prev = m0_ref[0] if has_init else jnp.zeros((N_PAIR, CHUNK), F32)
            a1s, a2s = [], []
            for n in order:
                b_last_n, m_loc_n = stat_s[n, 0], stat_s[n, 1]
                m_new = jnp.maximum(b_last_n + m_prev, m_loc_n)
                a1s.append(jnp.exp(b_last_n + m_prev - m_new))
                a2s.append(jnp.exp(m_loc_n - m_new))
                mprev_s[n, d] = m_prev
                m_prev = m_new
            if emit_state:
                mfin_ref[g, d] = m_prev
            for h in range(H_M):
                c = d * H_M + h
                if has_init:
                    mem = c0_ref[0, 0, d, h]
                    nrm = jnp.broadcast_to(n0_ref[0][c:c + 1, :], (NORM_ROWS, DH_M))
                else:
                    mem = jnp.zeros((DH_M, DH_M), F32)
                    nrm = jnp.zeros((NORM_ROWS, DH_M), F32)
                for idx, n in enumerate(order):
                    a1 = a1s[idx][c:c + 1, :]
                    a2 = a2s[idx][c:c + 1, :]
                    local_mem = kv_s[n, c, :, :DH_M]
                    local_nrm = kv_s[n, c, 0:NORM_ROWS, DH_M:]
                    kv_s[n, c, :, :DH_M] = mem
                    kv_s[n, c, 0:NORM_ROWS, DH_M:] = nrm
                    mem = a1 * mem + a2 * local_mem
                    nrm = a1 * nrm + a2 * local_nrm
                if emit_state:
                    cfin_ref[g, 0, d, h] = mem
                    nfin_ref[g, c:c + 1, :] = nrm[0:1]

    tri_j = lax.broadcasted_iota(jnp.int32, (CHUNK, CHUNK), 0)
    tri_i = lax.broadcasted_iota(jnp.int32, (CHUNK, CHUNK), 1)
    mem_rows = DH_M + NORM_ROWS

    def m_pass2(n, carry):
        rows = pl.ds(pl.multiple_of(n * CHUNK, CHUNK), CHUNK)
        zc = igb_s[rows, :]
        zt = zt_s[n]
        bigs = []
        for h in range(H_M):
            kh = k_s[rows, head_cols(h, DH_M)].astype(BF16)
            mems = [jnp.concatenate([kv_s[n, d * H_M + h, :, :DH_M], kv_s[n, d * H_M + h, 0:NORM_ROWS, DH_M:]],
                                    axis=0).astype(BF16) for d in range(N_DIR)]
            bigs.append(_dot(jnp.concatenate([kh] + mems, axis=0), qt_s[n, h]))
        for h in range(H_M):
            big = bigs[h]
            scores_t = big[:CHUNK]
            weights, crosses, w_ints, m_is = [], [], [], []
            for d in range(N_DIR):
                c = d * H_M + h
                a_col = zc[:, 2 * N_PAIR + c:2 * N_PAIR + c + 1]
                b_row = zt[N_PAIR + c:N_PAIR + c + 1, :]
                mask = (tri_j <= tri_i) if d == 0 else (tri_j >= tri_i)
                dlog = jnp.where(mask, b_row + a_col, -jnp.inf)
                lin = b_row + mprev_s[n, d][c:c + 1, :]
                m_i = jnp.maximum(jnp.max(dlog, axis=0, keepdims=True), lin)
                weights.append(scores_t * jnp.exp(dlog - m_i))
                crosses.append(big[CHUNK + d * mem_rows:CHUNK + (d + 1) * mem_rows])
                w_ints.append(jnp.exp(lin - m_i))
                m_is.append(m_i)
            intra = _dot(t_s[n, h].astype(BF16), jnp.concatenate(weights, axis=1).astype(BF16))
            h_sum_t = jnp.zeros((DH_M, CHUNK), F32)
            for d in range(N_DIR):
                cross, w_int = crosses[d], w_ints[d]
                num = intra[:, d * CHUNK:(d + 1) * CHUNK] + cross[:DH_M] * w_int
                den = jnp.sum(weights[d], axis=0, keepdims=True) + cross[DH_M:DH_M + 1] * w_int
                denom = jnp.maximum(jnp.abs(den), jnp.exp(-m_is[d]))
                h_sum_t = h_sum_t + num * (1.0 / denom)
            wide_s[rows, head_cols(h, DH_M)] = h_sum_t.T
        return carry

    lax.fori_loop(0, n_chunks, m_pass2, 0, unroll=CHUNK_UNROLL)

    for rb in range(n_rb):
        rows = rb_rows(rb)
        mo = _dot(u_s[rows, :], wmo_ref[...])
        hm = _sigmoid(mo) * wide_s[rows, :M_W]
        for h in range(H_M):
            cols = head_cols(h, DH_M)
            hm_out[rows, cols] = _head_norm(hm[:, cols], mgn_ref[:, cols]).astype(BF16)


def _mixer_core_call(x, mod, mod_row, p, seq_len, group_rows, rope, init):
    t = x.shape[0]
    n_groups = t // group_rows
    n_chunks = group_rows // CHUNK
    seq_per_group = group_rows // seq_len
    use_rope = rope is not None
    has_init = init is not None
    emit_state = not has_init

    w_in = p["w_in"]
    groups = []
    start = 0
    for width in (2 * RET_QK, RET_V, RET_V, 2 * M_W, M_W, M_W, GATE_LANES):
        assert start % width == 0
        groups.append(pl.BlockSpec((D_MODEL, width), functools.partial(lambda i, c: (0, c), c=start // width)))
        start += width
    small = [p["conv_w"], p["conv_b"], p["gate_bias"], p["decay_rows"], p["ret_gn"], p["m_gn"]]
    inputs = [x, mod, p["norm_mix"]] + [w_in] * len(groups) + small
    in_specs = [
        pl.BlockSpec((group_rows, D_MODEL), lambda i: (i, 0)),
        pl.BlockSpec((1, N_MOD, D_MODEL), lambda i: (mod_row(i * group_rows), 0, 0)),
        _const_spec(p["norm_mix"].shape),
    ] + groups + [_const_spec(a.shape) for a in small]
    if use_rope:
        inputs += [rope[0], rope[1]]
        in_specs += [_const_spec(rope[0].shape), _const_spec(rope[1].shape)]
    if has_init:
        s0, c0, n0, m0 = init
        inputs += [s0, c0, n0, m0]
        in_specs += [
            pl.BlockSpec((1, 1, N_DIR, H_RET, DK_RET, DV_RET), lambda i: (i, 0, 0, 0, 0, 0)),
            pl.BlockSpec((1, 1, N_DIR, H_M, DH_M, DH_M), lambda i: (i, 0, 0, 0, 0, 0)),
            pl.BlockSpec((1, N_PAIR, DH_M), lambda i: (i, 0, 0)),
            pl.BlockSpec((1, N_PAIR, CHUNK), lambda i: (i, 0, 0)),
        ]

    out_shape = [jax.ShapeDtypeStruct((t, RET_V), BF16), jax.ShapeDtypeStruct((t, M_W), BF16)]
    out_specs = [pl.BlockSpec((group_rows, RET_V), lambda i: (i, 0)),
                 pl.BlockSpec((group_rows, M_W), lambda i: (i, 0))]
    if emit_state:
        n_seq = t // seq_len
        out_shape += [
            jax.ShapeDtypeStruct((n_seq, 1, N_DIR, H_RET, DK_RET, DV_RET), F32),
            jax.ShapeDtypeStruct((n_seq, 1, N_DIR, H_M, DH_M, DH_M), F32),
            jax.ShapeDtypeStruct((n_seq, N_PAIR, DH_M), F32),
            jax.ShapeDtypeStruct((n_seq, N_DIR, N_PAIR, CHUNK), F32),
        ]
        out_specs += [
            pl.BlockSpec((seq_per_group, 1, N_DIR, H_RET, DK_RET, DV_RET), lambda i: (i, 0, 0, 0, 0, 0)),
            pl.BlockSpec((seq_per_group, 1, N_DIR, H_M, DH_M, DH_M), lambda i: (i, 0, 0, 0, 0, 0)),
            pl.BlockSpec((seq_per_group, N_PAIR, DH_M), lambda i: (i, 0, 0)),
            pl.BlockSpec((seq_per_group, N_DIR, N_PAIR, CHUNK), lambda i: (i, 0, 0, 0)),
        ]

    scratch = [
        pltpu.VMEM((group_rows, D_MODEL), BF16),
        pltpu.VMEM((group_rows, RET_QK), F32),
        pltpu.VMEM((group_rows, RET_QK), F32),
        pltpu.VMEM((group_rows, RET_V), BF16),
        pltpu.VMEM((n_chunks, H_M, DH_M, CHUNK), F32),
        pltpu.VMEM((n_chunks, H_M, DH_M, CHUNK), BF16),
        pltpu.VMEM((n_chunks, N_PAIR, DK_RET, DV_RET), F32),
        pltpu.VMEM((group_rows, RET_V), F32),
        pltpu.VMEM((group_rows, GATE_LANES), F32),
        pltpu.VMEM((n_chunks, 3 * N_PAIR, CHUNK), F32),
        pltpu.VMEM((n_chunks, N_PAIR, CHUNK), F32),
        pltpu.VMEM((H_RET, 5, CHUNK, CHUNK), F32),
        pltpu.VMEM((n_chunks, 2, N_PAIR, CHUNK), F32),
        pltpu.VMEM((n_chunks, N_DIR, N_PAIR, CHUNK), F32),
    ]
    kern = functools.partial(_mixer_core_kernel, group_rows=group_rows, seq_len=seq_len, use_rope=use_rope,
                             has_init=has_init, emit_state=emit_state)
    return pl.pallas_call(
        kern,
        out_shape=out_shape,
        grid=(n_groups,),
        in_specs=in_specs,
        out_specs=out_specs,
        scratch_shapes=scratch,
        compiler_params=pltpu.CompilerParams(
            dimension_semantics=("arbitrary",), vmem_limit_bytes=VMEM_LIMIT),
        name="mixer_core",
    )(*inputs)


def _mixer_out_kernel(x_ref, rg_ref, hm_ref, mod_ref, nmix_ref, wbg_ref, wru_ref, wmu_ref, wout_ref,
                      nffn_ref, w1_ref, w3_ref, w2_ref, nfin_ref, o_ref):
    mod = mod_ref[0]
    x = x_ref[...]
    u = _norm_mod(x, nmix_ref[...], mod[3:4], mod[4:5]).astype(BF16)
    gates = _sigmoid(_dot(u, wbg_ref[...]))
    ret_branch = _dot(rg_ref[...], wru_ref[...])
    m_branch = _dot(hm_ref[...], wmu_ref[...])
    merged = (gates[:, :D_MODEL] * ret_branch + gates[:, D_MODEL:] * m_branch).astype(BF16)
    x = x + mod[5:6] * _dot(merged, wout_ref[...])
    x = _swiglu_half_step(x, mod, 6, nffn_ref[...], w1_ref, w3_ref, w2_ref)
    ms = jnp.mean(x * x, axis=-1, keepdims=True)
    o_ref[...] = x * lax.rsqrt(ms + EPS) * nfin_ref[...]


def _mixer_out_call(x, rgated, hmn, mod, mod_row, p):
    t = x.shape[0]
    tm = FFN_TILE
    consts = [p["norm_mix"], p["w_bg"], p["w_ret_up"], p["w_m_up"], p["w_out"],
              p["norm_ffn2"], p["w1_ffn2"], p["w3_ffn2"], p["w2_ffn2"], p["norm_final"]]
    return pl.pallas_call(
        _mixer_out_kernel,
        out_shape=jax.ShapeDtypeStruct((t, D_MODEL), F32),
        grid=(t // tm,),
        in_specs=[
            pl.BlockSpec((tm, D_MODEL), lambda i: (i, 0)),
            pl.BlockSpec((tm, RET_V), lambda i: (i, 0)),
            pl.BlockSpec((tm, M_W), lambda i: (i, 0)),
            pl.BlockSpec((1, N_MOD, D_MODEL), lambda i: (mod_row(i * tm), 0, 0)),
        ] + [_const_spec(a.shape) for a in consts],
        out_specs=pl.BlockSpec((tm, D_MODEL), lambda i: (i, 0)),
        compiler_params=pltpu.CompilerParams(
            dimension_semantics=("arbitrary",), vmem_limit_bytes=VMEM_LIMIT),
        name="mixer_out",
    )(x, rgated, hmn, mod, *consts)


def _rope_tables(seq_len):
    rows = seq_len // GRID_W
    r = jnp.repeat(jnp.arange(rows, dtype=F32), GRID_W)
    col = (jnp.arange(seq_len) % GRID_W).astype(F32)
    n_f = DK_RET // 4
    freqs = ROPE_BASE ** (-jnp.arange(n_f, dtype=F32) / n_f)
    ang = jnp.concatenate([r[:, None] * freqs, col[:, None] * freqs], axis=-1)
    cos, sin = jnp.cos(ang), jnp.sin(ang)
    return jnp.concatenate([cos, cos], axis=-1), jnp.concatenate([-sin, sin], axis=-1)


def _layer_params(l, norm_ffn1, w1_ffn1, w3_ffn1, w2_ffn1, norm_mix, conv_w, conv_b,
                  ret_decay_logit, b_igate, b_fgate, ret_gn, m_gn, norm_ffn2, norm_final):
    gate_bias = jnp.pad(jnp.concatenate([b_igate[l], b_fgate[l]]), (0, GATE_LANES - 2 * N_PAIR))
    p = dict(
        norm_ffn1=norm_ffn1[l][None, :], norm_mix=norm_mix[l][None, :], norm_ffn2=norm_ffn2[l][None, :],
        norm_final=norm_final[None, :],
        w1_ffn1=w1_ffn1[l].astype(BF16), w3_ffn1=w3_ffn1[l].astype(BF16), w2_ffn1=w2_ffn1[l].astype(BF16),
        conv_w=conv_w[l], conv_b=conv_b[l][None, :],
        gate_bias=gate_bias[None, :],
        decay_rows=jnp.broadcast_to(ret_decay_logit[l].reshape(N_DIR * H_RET, 1), (N_DIR * H_RET, DV_RET)),
        ret_gn=ret_gn[l][None, :], m_gn=m_gn[l][None, :],
    )
    return p


def kernel(x_prompt, x_sample, c, state_ret, state_mlstm_C, state_mlstm_n, state_mlstm_m, c_ctx, w_ada, b_ada, norm_ffn1, w1_ffn1, w3_ffn1, w2_ffn1, norm_mix, w_in, conv_w, conv_b, ret_decay_logit, b_igate, b_fgate, ret_gn, m_gn, w_ret_up, w_m_up, w_out, norm_ffn2, w1_ffn2, w3_ffn2, w2_ffn2, norm_final):
    bp, lp, _ = x_prompt.shape
    bs, ls, _ = x_sample.shape
    depth = w_ada.shape[0]
    assert depth == 1 and lp % CHUNK == 0 and PROMPT_GROUP_ROWS % lp == 0 and ls <= MAX_GROUP_ROWS
    assert lp & (lp - 1) == 0 and ls & (ls - 1) == 0

    ctx_row = 8 * ((bs + 7) // 8)
    cvec = jnp.zeros((ctx_row + 8, D_MODEL), F32).at[:bs].set(c).at[ctx_row].set(c_ctx)
    rope = _rope_tables(ls)

    xp = x_prompt.reshape(bp * lp, D_MODEL)
    xs = x_sample.reshape(bs * ls, D_MODEL)
    prompt_row = lambda r: ctx_row
    sample_row = lambda r: r // ls

    l = 0
    p = _layer_params(l, norm_ffn1, w1_ffn1, w3_ffn1, w2_ffn1, norm_mix, conv_w, conv_b,
                      ret_decay_logit, b_igate, b_fgate, ret_gn, m_gn, norm_ffn2, norm_final)
    mod = _ada_call(cvec, w_ada[l], b_ada[l][None, :]).reshape(ctx_row + 8, N_MOD, D_MODEL)

    n0 = state_mlstm_n[:, l].astype(F32).reshape(bs, N_PAIR, DH_M)
    m0 = jnp.broadcast_to(state_mlstm_m[:, l].astype(F32).reshape(bs, N_PAIR, 1), (bs, N_PAIR, CHUNK))
    init = (state_ret.astype(F32), state_mlstm_C.astype(F32), n0, m0)

    whole = lambda a: (a, False, ((0, a.shape[1]),))
    bg_start = D_IN - N_DIR * D_MODEL
    xp1, (p["w_in"], p["w_bg"]) = _ffn_call(
        xp, mod, prompt_row, p["norm_ffn1"], p["w1_ffn1"], p["w3_ffn1"], p["w2_ffn1"],
        casts=((w_in[l].T, True, ((0, D_IN), (bg_start, N_DIR * D_MODEL))),))
    xs1, (p["w1_ffn2"], p["w3_ffn2"], p["w2_ffn2"], p["w_ret_up"], p["w_m_up"], p["w_out"]) = _ffn_call(
        xs, mod, sample_row, p["norm_ffn1"], p["w1_ffn1"], p["w3_ffn1"], p["w2_ffn1"],
        casts=(whole(w1_ffn2[l]), whole(w3_ffn2[l]), whole(w2_ffn2[l]),
               whole(w_ret_up[l]), whole(w_m_up[l]), whole(w_out[l])))

    rg_p, hm_p, s_fin, c_fin, n_fin, m_fin = _mixer_core_call(xp1, mod, prompt_row, p, lp, PROMPT_GROUP_ROWS, None, None)
    rg_s, hm_s = _mixer_core_call(xs1, mod, sample_row, p, ls, ls, rope, init)

    yp = _mixer_out_call(xp1, rg_p, hm_p, mod, prompt_row, p)
    ys = _mixer_out_call(xs1, rg_s, hm_s, mod, sample_row, p)

    dt = x_prompt.dtype
    new_c = c_fin
    new_n = n_fin.reshape(bp, 1, N_DIR, H_M, DH_M)
    m_diag = jnp.stack([m_fin[:, d, d * H_M:(d + 1) * H_M, 0] for d in range(N_DIR)], axis=1)
    new_m = m_diag.reshape(bp, 1, N_DIR, H_M)
    return (yp.reshape(bp, lp, D_MODEL).astype(dt), ys.reshape(bs, ls, D_MODEL).astype(dt),
            s_fin.astype(dt), new_c.astype(dt), new_n.astype(dt), new_m.astype(dt))
```

```python
import functools

import jax
import jax.numpy as jnp
from jax import lax
from jax.experimental import pallas as pl
from jax.experimental.pallas import tpu as pltpu

F32 = jnp.float32
BF16 = jnp.bfloat16

D_MODEL = 1024
D_FF = 2816
CHUNK = 128
N_DIR = 2
H_RET = 4
DK_RET = 128
DV_RET = 256
H_M = 4
DH_M = 128
RET_QK = H_RET * DK_RET
RET_V = H_RET * DV_RET
M_W = H_M * DH_M
D_IN = 2 * RET_QK + 2 * RET_V + 4 * M_W + 2 * N_DIR * H_M + N_DIR * D_MODEL
GRID_W = 64
ROPE_BASE = 10000.0
EPS = 1e-6
GN_EPS = 1e-5
N_MOD = 9
N_PAIR = N_DIR * H_M
GATE_LANES = 128
LANES = 128
BF16_SUBLANES = 16
NORM_ROWS = BF16_SUBLANES

MAX_GROUP_ROWS = 1024
PROMPT_GROUP_ROWS = 512
ROW_BLOCK = 256
CHUNK_UNROLL = 4
FFN_TILE = 512
VMEM_LIMIT = 60 * 1024 * 1024


def _sigmoid(x):
    return 1.0 / (1.0 + jnp.exp(-x))


def _silu(x):
    return x * _sigmoid(x)


def _log_sigmoid(x):
    return -(jnp.maximum(-x, 0.0) + jnp.log1p(jnp.exp(-jnp.abs(x))))


def _norm_mod(x, gain, shift, scale):
    ms = jnp.mean(x * x, axis=-1, keepdims=True)
    y = x * lax.rsqrt(ms + EPS) * gain
    return y * (1.0 + scale) + shift


def _dot(a, b):
    return jnp.dot(a, b, preferred_element_type=F32)


def _dot_nt(a, b):
    return lax.dot_general(a, b, (((1,), (1,)), ((), ())), preferred_element_type=F32)


def _head_norm(o, gain):
    mu = jnp.mean(o, axis=-1, keepdims=True)
    oc = o - mu
    var = jnp.mean(oc * oc, axis=-1, keepdims=True)
    return oc * lax.rsqrt(var + GN_EPS) * gain


def _ada_kernel(c_ref, w_ref, b_ref, o_ref):
    s = _silu(c_ref[...]).astype(BF16)
    o_ref[...] = _dot(s, w_ref[...].astype(BF16)) + b_ref[...]


def _ada_call(cvec, w_ada, b_ada):
    rows = cvec.shape[0]
    n_out = w_ada.shape[1]
    tile = D_MODEL
    return pl.pallas_call(
        _ada_kernel,
        out_shape=jax.ShapeDtypeStruct((rows, n_out), F32),
        grid=(n_out // tile,),
        in_specs=[
            pl.BlockSpec((rows, D_MODEL), lambda j: (0, 0)),
            pl.BlockSpec((D_MODEL, tile), lambda j: (0, j)),
            pl.BlockSpec((1, tile), lambda j: (0, j)),
        ],
        out_specs=pl.BlockSpec((rows, tile), lambda j: (0, j)),
        compiler_params=pltpu.CompilerParams(
            dimension_semantics=("arbitrary",), vmem_limit_bytes=VMEM_LIMIT),
        name="ada",
    )(cvec, w_ada, b_ada)


def _swiglu_half_step(x, mod, k0, gain, w1_ref, w3_ref, w2_ref):
    u = _norm_mod(x, gain, mod[k0:k0 + 1], mod[k0 + 1:k0 + 2]).astype(BF16)
    h1 = _dot(u, w1_ref[...])
    h3 = _dot(u, w3_ref[...])
    h = (_silu(h1) * h3).astype(BF16)
    y = _dot(h, w2_ref[...])
    return x + 0.5 * mod[k0 + 2:k0 + 3] * y


def _ffn_kernel(*refs, cast_plan):
    n_in = 6 + len(cast_plan)
    x_ref, mod_ref, g_ref, w1_ref, w3_ref, w2_ref = refs[:6]
    o_ref = refs[n_in]
    o_ref[...] = _swiglu_half_step(x_ref[...], mod_ref[0], 0, g_ref[...], w1_ref, w3_ref, w2_ref)
    cast_out = iter(refs[n_in + 1:])
    for src_ref, (transposed, ranges) in zip(refs[6:n_in], cast_plan):
        block = src_ref[...].T if transposed else src_ref[...]
        for start, width in ranges:
            next(cast_out)[...] = block[:, start:start + width].astype(BF16)


def _const_spec(shape):
    nd = len(shape)
    return pl.BlockSpec(shape, lambda i: (0,) * nd)


def _ffn_call(x, mod, mod_row, gain, w1, w3, w2, casts=()):
    t = x.shape[0]
    tm = FFN_TILE
    steps = t // tm
    in_specs = [
        pl.BlockSpec((tm, D_MODEL), lambda i: (i, 0)),
        pl.BlockSpec((1, N_MOD, D_MODEL), lambda i: (mod_row(i * tm), 0, 0)),
        _const_spec((1, D_MODEL)),
        _const_spec((D_MODEL, D_FF)),
        _const_spec((D_MODEL, D_FF)),
        _const_spec((D_FF, D_MODEL)),
    ]
    out_shape = [jax.ShapeDtypeStruct((t, D_MODEL), F32)]
    out_specs = [pl.BlockSpec((tm, D_MODEL), lambda i: (i, 0))]
    for src, transposed, ranges in casts:
        rows, cols = src.shape[::-1] if transposed else src.shape
        if transposed:
            assert rows % (steps * LANES) == 0
            in_specs.append(pl.BlockSpec((cols, rows // steps), lambda i: (0, i)))
        else:
            assert rows % (steps * BF16_SUBLANES) == 0
            in_specs.append(pl.BlockSpec((rows // steps, cols), lambda i: (i, 0)))
        for _, width in ranges:
            out_shape.append(jax.ShapeDtypeStruct((rows, width), BF16))
            out_specs.append(pl.BlockSpec((rows // steps, width), lambda i: (i, 0)))
    outs = pl.pallas_call(
        functools.partial(_ffn_kernel, cast_plan=tuple((tr, r) for _, tr, r in casts)),
        out_shape=out_shape,
        grid=(steps,),
        in_specs=in_specs,
        out_specs=out_specs,
        compiler_params=pltpu.CompilerParams(
            dimension_semantics=("arbitrary",), vmem_limit_bytes=VMEM_LIMIT),
        name="ffn",
    )(x, mod, gain, w1, w3, w2, *[src for src, _, _ in casts])
    return outs[0], outs[1:]


def _mixer_core_kernel(*refs, group_rows, seq_len, use_rope, has_init, emit_state):
    it = iter(refs)
    x_ref, mod_ref, nmix_ref = next(it), next(it), next(it)
    wrqk_ref, wrv_ref, wrg_ref = next(it), next(it), next(it)
    wmqk_ref, wmv_ref, wmo_ref, wgt_ref = next(it), next(it), next(it), next(it)
    convw_ref, convb_ref, gbias_ref, decay_ref = next(it), next(it), next(it), next(it)
    rgn_ref, mgn_ref = next(it), next(it)
    if use_rope:
        cos_ref, sin_ref = next(it), next(it)
    if has_init:
        s0_ref, c0_ref, n0_ref, m0_ref = next(it), next(it), next(it), next(it)
    rg_out, hm_out = next(it), next(it)
    if emit_state:
        sfin_ref, cfin_ref, nfin_ref, mfin_ref = next(it), next(it), next(it), next(it)
    (u_s, q_s, k_s, v_s, t_s, qt_s, kv_s, wide_s, igb_s, zt_s, wt_s, dec_s, stat_s,
     mprev_s) = (next(it) for _ in range(14))

    rows_total = group_rows
    n_chunks = rows_total // CHUNK
    cps = seq_len // CHUNK
    n_seq = rows_total // seq_len
    n_rb = rows_total // ROW_BLOCK
    mod = mod_ref[0]

    def rb_rows(rb):
        return slice(rb * ROW_BLOCK, (rb + 1) * ROW_BLOCK)

    def head_cols(h, width):
        return slice(h * width, (h + 1) * width)

    chunks_per_rb = ROW_BLOCK // CHUNK
    for rb in range(n_rb):
        rows = rb_rows(rb)
        ub = _norm_mod(x_ref[rows, :], nmix_ref[...], mod[3:4], mod[4:5]).astype(BF16)
        u_s[rows, :] = ub
        qk = _dot(ub, wrqk_ref[...])
        q = qk[:, :RET_QK]
        k = qk[:, RET_QK:] * (DK_RET ** -0.5)
        for h in range(H_RET):
            cols = head_cols(h, DK_RET)
            qh, kh = q[:, cols], k[:, cols]
            if use_rope:
                cs, sn = cos_ref[rows, :], sin_ref[rows, :]
                qh = qh * cs + pltpu.roll(qh, DK_RET // 2, 1) * sn
                kh = kh * cs + pltpu.roll(kh, DK_RET // 2, 1) * sn
            q_s[rows, cols] = qh
            for ch in range(chunks_per_rb):
                t_s[rb * chunks_per_rb + ch, h] = kh[ch * CHUNK:(ch + 1) * CHUNK, :].T
        v_s[rows, :] = _dot(ub, wrv_ref[...]).astype(BF16)

    pos_i = lax.broadcasted_iota(jnp.int32, (CHUNK, CHUNK), 0).astype(F32)
    pos_j = lax.broadcasted_iota(jnp.int32, (CHUNK, CHUNK), 1).astype(F32)
    lg_rows = _log_sigmoid(decay_ref[...])
    for h in range(H_RET):
        lgf = lg_rows[h:h + 1, :CHUNK]
        lgb = lg_rows[H_RET + h:H_RET + h + 1, :CHUNK]
        dec_s[h, 0] = jnp.where(pos_i > pos_j, jnp.exp((pos_i - pos_j) * lgf),
                                jnp.where(pos_i < pos_j, jnp.exp((pos_j - pos_i) * lgb), 2.0))
        dec_s[h, 1] = jnp.exp((CHUNK - 1.0 - pos_j) * lgf)
        dec_s[h, 2] = jnp.exp(pos_j * lgb)
        dec_s[h, 3] = jnp.exp((pos_i + 1.0) * lgf)
        dec_s[h, 4] = jnp.exp((CHUNK - pos_i) * lgb)
    g_chunk = jnp.exp(CHUNK * lg_rows)

    def ret_pass1(n, carry):
        rows = pl.ds(pl.multiple_of(n * CHUNK, CHUNK), CHUNK)
        for h in range(H_RET):
            k_t = t_s[n, h]
            kcat = jnp.concatenate([k_t * dec_s[h, 1], k_t * dec_s[h, 2]], axis=0).astype(BF16)
            kv = _dot(kcat, v_s[rows, head_cols(h, DV_RET)])
            kv_s[n, 2 * h] = kv[:DK_RET]
            kv_s[n, 2 * h + 1] = kv[DK_RET:]
        return carry

    lax.fori_loop(0, n_chunks, ret_pass1, 0, unroll=CHUNK_UNROLL)

    for g in range(n_seq):
        for h in range(H_RET):
            for d in range(N_DIR):
                slot = 2 * h + d
                decay = g_chunk[d * H_RET + h:d * H_RET + h + 1, :]
                state = s0_ref[0, 0, d, h] if has_init else jnp.zeros((DK_RET, DV_RET), F32)
                order = range(g * cps, (g + 1) * cps)
                for n in (order if d == 0 else reversed(order)):
                    local = kv_s[n, slot]
                    kv_s[n, slot] = state
                    state = decay * state + local
                if emit_state:
                    sfin_ref[g, 0, d, h] = state

    def ret_pass2(n, carry):
        rows = pl.ds(pl.multiple_of(n * CHUNK, CHUNK), CHUNK)
        all_scores = [_dot(q_s[rows, head_cols(h, DK_RET)].astype(BF16), t_s[n, h].astype(BF16))
                      for h in range(H_RET)]
        for h in range(H_RET):
            qh = q_s[rows, head_cols(h, DK_RET)]
            vh = v_s[rows, head_cols(h, DV_RET)]
            sm = (all_scores[h] * dec_s[h, 0]).astype(BF16)
            qcat = jnp.concatenate([qh * dec_s[h, 3], qh * dec_s[h, 4]], axis=1).astype(BF16)
            scat = jnp.concatenate([kv_s[n, 2 * h], kv_s[n, 2 * h + 1]], axis=0).astype(BF16)
            o = _dot(sm, vh) + _dot(qcat, scat)
            wide_s[rows, head_cols(h, DV_RET)] = _head_norm(o, rgn_ref[:, head_cols(h, DV_RET)])
        return carry

    lax.fori_loop(0, n_chunks, ret_pass2, 0, unroll=CHUNK_UNROLL)

    for rb in range(n_rb):
        rows = rb_rows(rb)
        rg = _dot(u_s[rows, :], wrg_ref[...])
        rg_out[rows, :] = (wide_s[rows, :] * _silu(rg)).astype(BF16)

    for rb in range(n_rb):
        rows = rb_rows(rb)
        ub = u_s[rows, :]
        wide_s[rows, :] = _dot(ub, wmqk_ref[...])
        mv = _dot(ub, wmv_ref[...])
        for ch in range(chunks_per_rb):
            for h in range(H_M):
                t_s[rb * chunks_per_rb + ch, h] = mv[ch * CHUNK:(ch + 1) * CHUNK, head_cols(h, DH_M)].T
        glane = lax.broadcasted_iota(jnp.int32, (ROW_BLOCK, GATE_LANES), 1)
        gt = jnp.where(glane < 2 * N_PAIR, _dot(ub, wgt_ref[...]) + gbias_ref[...], 0.0)
        igb_s[rows, :] = jnp.where(glane < N_PAIR, gt, _log_sigmoid(gt))

    row_id = lax.broadcasted_iota(jnp.int32, (rows_total, CHUNK), 0)
    seq_pos = row_id & (seq_len - 1)
    for cb in range(2 * M_W // CHUNK):
        cols = slice(cb * CHUNK, (cb + 1) * CHUNK)
        xc = wide_s[:, cols]
        prev = jnp.where(seq_pos == 0, 0.0, pltpu.roll(xc, 1, 0))
        nxt = jnp.where(seq_pos == seq_len - 1, 0.0, pltpu.roll(xc, rows_total - 1, 0))
        y = convb_ref[:, cols] + convw_ref[0:1, cols] * prev
        y = y + convw_ref[1:2, cols] * xc
        y = y + convw_ref[2:3, cols] * nxt
        y = _silu(y)
        if cb < H_M:
            for n in range(n_chunks):
                qt_s[n, cb] = y[n * CHUNK:(n + 1) * CHUNK, :].T.astype(BF16)
        else:
            k_s[:, head_cols(cb - H_M, DH_M)] = y * (DH_M ** -0.5)

    z = igb_s[...]
    cpos = row_id & (CHUNK - 1)
    glane = lax.broadcasted_iota(jnp.int32, (rows_total, GATE_LANES), 1)
    pre, suf = z, z
    step = 1
    while step < CHUNK:
        pre = pre + jnp.where(cpos >= step, pltpu.roll(pre, step, 0), 0.0)
        suf = suf + jnp.where(cpos < CHUNK - step, pltpu.roll(suf, rows_total - step, 0), 0.0)
        step *= 2
    cum = jnp.where(glane < N_PAIR + H_M, pre, suf)
    i_minus_b = z - pltpu.roll(cum, GATE_LANES - N_PAIR, 1)
    igb = jnp.where(glane < N_PAIR, z,
                    jnp.where(glane < 2 * N_PAIR, cum, pltpu.roll(i_minus_b, 2 * N_PAIR, 1)))
    igb_s[...] = igb

    for n in range(n_chunks):
        zt_s[n] = igb[n * CHUNK:(n + 1) * CHUNK, :].T[0:3 * N_PAIR, :]
    ig_all = zt_s[:, 0:N_PAIR, :]
    b_all = zt_s[:, N_PAIR:2 * N_PAIR, :]
    pair = lax.broadcasted_iota(jnp.int32, (n_chunks, N_PAIR, 1), 1)
    b_last = jnp.where(pair < H_M, b_all[:, :, CHUNK - 1:CHUNK], b_all[:, :, 0:1])
    g_all = (b_last - b_all) + ig_all
    m_loc = jnp.max(g_all, axis=-1, keepdims=True)
    wt_s[...] = jnp.exp(g_all - m_loc)
    stat_s[:, 0] = jnp.broadcast_to(b_last, (n_chunks, N_PAIR, CHUNK))
    stat_s[:, 1] = jnp.broadcast_to(m_loc, (n_chunks, N_PAIR, CHUNK))

    def m_pass1(n, carry):
        rows = pl.ds(pl.multiple_of(n * CHUNK, CHUNK), CHUNK)
        w_t = wt_s[n]
        for h in range(H_M):
            kh = k_s[rows, head_cols(h, DH_M)].astype(BF16)
            v_t = t_s[n, h]
            parts = []
            for d in range(N_DIR):
                w_row = w_t[d * H_M + h:d * H_M + h + 1, :]
                parts += [v_t * w_row, jnp.broadcast_to(w_row, (NORM_ROWS, CHUNK))]
            local = _dot(jnp.concatenate(parts, axis=0).astype(BF16), kh)
            for d in range(N_DIR):
                c = d * H_M + h
                base = d * (DH_M + NORM_ROWS)
                kv_s[n, c, :, :DH_M] = local[base:base + DH_M]
                kv_s[n, c, 0:NORM_ROWS, DH_M:] = local[base + DH_M:base + DH_M + NORM_ROWS]
        return carry

    lax.fori_loop(0, n_chunks, m_pass1, 0, unroll=CHUNK_UNROLL)

    for g in range(n_seq):
        for d in range(N_DIR):
            order = list(range(g * cps, (g + 1) * cps))
            if d == 1:
                order.reverse()
            m_prev = m0_ref[0] if has_init else jnp.zeros((N_PAIR, CHUNK), F32)
            a1s, a2s = [], []
            for n in order:
                b_last_n, m_loc_n = stat_s[n, 0], stat_s[n, 1]
                m_new = jnp.maximum(b_last_n + m_prev, m_loc_n)
                a1s.append(jnp.exp(b_last_n + m_prev - m_new))
                a2s.append(jnp.exp(m_loc_n - m_new))
                mprev_s[n, d] = m_prev
                m_prev = m_new
            if emit_state:
                mfin_ref[g, d] = m_prev
            for h in range(H_M):
                c = d * H_M + h
                if has_init:
                    mem = c0_ref[0, 0, d, h]
                    nrm = jnp.broadcast_to(n0_ref[0][c:c + 1, :], (NORM_ROWS, DH_M))
                else:
                    mem = jnp.zeros((DH_M, DH_M), F32)
                    nrm = jnp.zeros((NORM_ROWS, DH_M), F32)
                for idx, n in enumerate(order):
                    a1 = a1s[idx][c:c + 1, :]
                    a2 = a2s[idx][c:c + 1, :]
                    local_mem = kv_s[n, c, :, :DH_M]
                    local_nrm = kv_s[n, c, 0:NORM_ROWS, DH_M:]
                    kv_s[n, c, :, :DH_M] = mem
                    kv_s[n, c, 0:NORM_ROWS, DH_M:] = nrm
                    mem = a1 * mem + a2 * local_mem
                    nrm = a1 * nrm + a2 * local_nrm
                if emit_state:
                    cfin_ref[g, 0, d, h] = mem
                    nfin_ref[g, c:c + 1, :] = nrm[0:1]

    tri_j = lax.broadcasted_iota(jnp.int32, (CHUNK, CHUNK), 0)
    tri_i = lax.broadcasted_iota(jnp.int32, (CHUNK, CHUNK), 1)
    mem_rows = DH_M + NORM_ROWS

    def m_pass2(n, carry):
        rows = pl.ds(pl.multiple_of(n * CHUNK, CHUNK), CHUNK)
        zc = igb_s[rows, :]
        zt = zt_s[n]
        bigs = []
        for h in range(H_M):
            kh = k_s[rows, head_cols(h, DH_M)].astype(BF16)
            mems = [jnp.concatenate([kv_s[n, d * H_M + h, :, :DH_M], kv_s[n, d * H_M + h, 0:NORM_ROWS, DH_M:]],
                                    axis=0).astype(BF16) for d in range(N_DIR)]
            bigs.append(_dot(jnp.concatenate([kh] + mems, axis=0), qt_s[n, h]))
        for h in range(H_M):
            big = bigs[h]
            scores_t = big[:CHUNK]
            weights, crosses, w_ints, m_is = [], [], [], []
            for d in range(N_DIR):
                c = d * H_M + h
                a_col = zc[:, 2 * N_PAIR + c:2 * N_PAIR + c + 1]
                b_row = zt[N_PAIR + c:N_PAIR + c + 1, :]
                mask = (tri_j <= tri_i) if d == 0 else (tri_j >= tri_i)
                dlog = jnp.where(mask, b_row + a_col, -jnp.inf)
                lin = b_row + mprev_s[n, d][c:c + 1, :]
                m_i = jnp.maximum(jnp.max(dlog, axis=0, keepdims=True), lin)
                weights.append(scores_t * jnp.exp(dlog - m_i))
                crosses.append(big[CHUNK + d * mem_rows:CHUNK + (d + 1) * mem_rows])
                w_ints.append(jnp.exp(lin - m_i))
                m_is.append(m_i)
            intra = _dot(t_s[n, h].astype(BF16), jnp.concatenate(weights, axis=1).astype(BF16))
            h_sum_t = jnp.zeros((DH_M, CHUNK), F32)
            for d in range(N_DIR):
                cross, w_int = crosses[d], w_ints[d]
                num = intra[:, d * CHUNK:(d + 1) * CHUNK] + cross[:DH_M] * w_int
                den = jnp.sum(weights[d], axis=0, keepdims=True) + cross[DH_M:DH_M + 1] * w_int
                denom = jnp.maximum(jnp.abs(den), jnp.exp(-m_is[d]))
                h_sum_t = h_sum_t + num * (1.0 / denom)
            wide_s[rows, head_cols(h, DH_M)] = h_sum_t.T
        return carry

    lax.fori_loop(0, n_chunks, m_pass2, 0, unroll=CHUNK_UNROLL)

    for rb in range(n_rb):
        rows = rb_rows(rb)
        mo = _dot(u_s[rows, :], wmo_ref[...])
        hm = _sigmoid(mo) * wide_s[rows, :M_W]
        for h in range(H_M):
            cols = head_cols(h, DH_M)
            hm_out[rows, cols] = _head_norm(hm[:, cols], mgn_ref[:, cols]).astype(BF16)


def _mixer_core_call(x, mod, mod_row, p, seq_len, group_rows, rope, init):
    t = x.shape[0]
    n_groups = t // group_rows
    n_chunks = group_rows // CHUNK
    seq_per_group = group_rows // seq_len
    use_rope = rope is not None
    has_init = init is not None
    emit_state = not has_init

    w_in = p["w_in"]
    groups = []
    start = 0
    for width in (2 * RET_QK, RET_V, RET_V, 2 * M_W, M_W, M_W, GATE_LANES):
        assert start % width == 0
        groups.append(pl.BlockSpec((D_MODEL, width), functools.partial(lambda i, c: (0, c), c=start // width)))
        start += width
    small = [p["conv_w"], p["conv_b"], p["gate_bias"], p["decay_rows"], p["ret_gn"], p["m_gn"]]
    inputs = [x, mod, p["norm_mix"]] + [w_in] * len(groups) + small
    in_specs = [
        pl.BlockSpec((group_rows, D_MODEL), lambda i: (i, 0)),
        pl.BlockSpec((1, N_MOD, D_MODEL), lambda i: (mod_row(i * group_rows), 0, 0)),
        _const_spec(p["norm_mix"].shape),
    ] + groups + [_const_spec(a.shape) for a in small]
    if use_rope:
        inputs += [rope[0], rope[1]]
        in_specs += [_const_spec(rope[0].shape), _const_spec(rope[1].shape)]
    if has_init:
        s0, c0, n0, m0 = init
        inputs += [s0, c0, n0, m0]
        in_specs += [
            pl.BlockSpec((1, 1, N_DIR, H_RET, DK_RET, DV_RET), lambda i: (i, 0, 0, 0, 0, 0)),
            pl.BlockSpec((1, 1, N_DIR, H_M, DH_M, DH_M), lambda i: (i, 0, 0, 0, 0, 0)),
            pl.BlockSpec((1, N_PAIR, DH_M), lambda i: (i, 0, 0)),
            pl.BlockSpec((1, N_PAIR, CHUNK), lambda i: (i, 0, 0)),
        ]

    out_shape = [jax.ShapeDtypeStruct((t, RET_V), BF16), jax.ShapeDtypeStruct((t, M_W), BF16)]
    out_specs = [pl.BlockSpec((group_rows, RET_V), lambda i: (i, 0)),
                 pl.BlockSpec((group_rows, M_W), lambda i: (i, 0))]
    if emit_state:
        n_seq = t // seq_len
        out_shape += [
            jax.ShapeDtypeStruct((n_seq, 1, N_DIR, H_RET, DK_RET, DV_RET), F32),
            jax.ShapeDtypeStruct((n_seq, 1, N_DIR, H_M, DH_M, DH_M), F32),
            jax.ShapeDtypeStruct((n_seq, N_PAIR, DH_M), F32),
            jax.ShapeDtypeStruct((n_seq, N_DIR, N_PAIR, CHUNK), F32),
        ]
        out_specs += [
            pl.BlockSpec((seq_per_group, 1, N_DIR, H_RET, DK_RET, DV_RET), lambda i: (i, 0, 0, 0, 0, 0)),
            pl.BlockSpec((seq_per_group, 1, N_DIR, H_M, DH_M, DH_M), lambda i: (i, 0, 0, 0, 0, 0)),
            pl.BlockSpec((seq_per_group, N_PAIR, DH_M), lambda i: (i, 0, 0)),
            pl.BlockSpec((seq_per_group, N_DIR, N_PAIR, CHUNK), lambda i: (i, 0, 0, 0)),
        ]

    scratch = [
        pltpu.VMEM((group_rows, D_MODEL), BF16),
        pltpu.VMEM((group_rows, RET_QK), F32),
        pltpu.VMEM((group_rows, RET_QK), F32),
        pltpu.VMEM((group_rows, RET_V), BF16),
        pltpu.VMEM((n_chunks, H_M, DH_M, CHUNK), F32),
        pltpu.VMEM((n_chunks, H_M, DH_M, CHUNK), BF16),
        pltpu.VMEM((n_chunks, N_PAIR, DK_RET, DV_RET), F32),
        pltpu.VMEM((group_rows, RET_V), F32),
        pltpu.VMEM((group_rows, GATE_LANES), F32),
        pltpu.VMEM((n_chunks, 3 * N_PAIR, CHUNK), F32),
        pltpu.VMEM((n_chunks, N_PAIR, CHUNK), F32),
        pltpu.VMEM((H_RET, 5, CHUNK, CHUNK), F32),
        pltpu.VMEM((n_chunks, 2, N_PAIR, CHUNK), F32),
        pltpu.VMEM((n_chunks, N_DIR, N_PAIR, CHUNK), F32),
    ]
    kern = functools.partial(_mixer_core_kernel, group_rows=group_rows, seq_len=seq_len, use_rope=use_rope,
                             has_init=has_init, emit_state=emit_state)
    return pl.pallas_call(
        kern,
        out_shape=out_shape,
        grid=(n_groups,),
        in_specs=in_specs,
        out_specs=out_specs,
        scratch_shapes=scratch,
        compiler_params=pltpu.CompilerParams(
            dimension_semantics=("arbitrary",), vmem_limit_bytes=VMEM_LIMIT),
        name="mixer_core",
    )(*inputs)


def _mixer_out_kernel(x_ref, rg_ref, hm_ref, mod_ref, nmix_ref, wbg_ref, wru_ref, wmu_ref, wout_ref,
                      nffn_ref, w1_ref, w3_ref, w2_ref, nfin_ref, o_ref):
    mod = mod_ref[0]
    x = x_ref[...]
    u = _norm_mod(x, nmix_ref[...], mod[3:4], mod[4:5]).astype(BF16)
    gates = _sigmoid(_dot(u, wbg_ref[...]))
    ret_branch = _dot(rg_ref[...], wru_ref[...])
    m_branch = _dot(hm_ref[...], wmu_ref[...])
    merged = (gates[:, :D_MODEL] * ret_branch + gates[:, D_MODEL:] * m_branch).astype(BF16)
    x = x + mod[5:6] * _dot(merged, wout_ref[...])
    x = _swiglu_half_step(x, mod, 6, nffn_ref[...], w1_ref, w3_ref, w2_ref)
    ms = jnp.mean(x * x, axis=-1, keepdims=True)
    o_ref[...] = x * lax.rsqrt(ms + EPS) * nfin_ref[...]


def _mixer_out_call(x, rgated, hmn, mod, mod_row, p):
    t = x.shape[0]
    tm = FFN_TILE
    consts = [p["norm_mix"], p["w_bg"], p["w_ret_up"], p["w_m_up"], p["w_out"],
              p["norm_ffn2"], p["w1_ffn2"], p["w3_ffn2"], p["w2_ffn2"], p["norm_final"]]
    return pl.pallas_call(
        _mixer_out_kernel,
        out_shape=jax.ShapeDtypeStruct((t, D_MODEL), F32),
        grid=(t // tm,),
        in_specs=[
            pl.BlockSpec((tm, D_MODEL), lambda i: (i, 0)),
            pl.BlockSpec((tm, RET_V), lambda i: (i, 0)),
            pl.BlockSpec((tm, M_W), lambda i: (i, 0)),
            pl.BlockSpec((1, N_MOD, D_MODEL), lambda i: (mod_row(i * tm), 0, 0)),
        ] + [_const_spec(a.shape) for a in consts],
        out_specs=pl.BlockSpec((tm, D_MODEL), lambda i: (i, 0)),
        compiler_params=pltpu.CompilerParams(
            dimension_semantics=("arbitrary",), vmem_limit_bytes=VMEM_LIMIT),
        name="mixer_out",
    )(x, rgated, hmn, mod, *consts)


def _rope_tables(seq_len):
    rows = seq_len // GRID_W
    r = jnp.repeat(jnp.arange(rows, dtype=F32), GRID_W)
    col = (jnp.arange(seq_len) % GRID_W).astype(F32)
    n_f = DK_RET // 4
    freqs = ROPE_BASE ** (-jnp.arange(n_f, dtype=F32) / n_f)
    ang = jnp.concatenate([r[:, None] * freqs, col[:, None] * freqs], axis=-1)
    cos, sin = jnp.cos(ang), jnp.sin(ang)
    return jnp.concatenate([cos, cos], axis=-1), jnp.concatenate([-sin, sin], axis=-1)


def _layer_params(l, norm_ffn1, w1_ffn1, w3_ffn1, w2_ffn1, norm_mix, conv_w, conv_b,
                  ret_decay_logit, b_igate, b_fgate, ret_gn, m_gn, norm_ffn2, norm_final):
    gate_bias = jnp.pad(jnp.concatenate([b_igate[l], b_fgate[l]]), (0, GATE_LANES - 2 * N_PAIR))
    p = dict(
        norm_ffn1=norm_ffn1[l][None, :], norm_mix=norm_mix[l][None, :], norm_ffn2=norm_ffn2[l][None, :],
        norm_final=norm_final[None, :],
        w1_ffn1=w1_ffn1[l].astype(BF16), w3_ffn1=w3_ffn1[l].astype(BF16), w2_ffn1=w2_ffn1[l].astype(BF16),
        conv_w=conv_w[l], conv_b=conv_b[l][None, :],
        gate_bias=gate_bias[None, :],
        decay_rows=jnp.broadcast_to(ret_decay_logit[l].reshape(N_DIR * H_RET, 1), (N_DIR * H_RET, DV_RET)),
        ret_gn=ret_gn[l][None, :], m_gn=m_gn[l][None, :],
    )
    return p


def kernel(x_prompt, x_sample, c, state_ret, state_mlstm_C, state_mlstm_n, state_mlstm_m, c_ctx, w_ada, b_ada, norm_ffn1, w1_ffn1, w3_ffn1, w2_ffn1, norm_mix, w_in, conv_w, conv_b, ret_decay_logit, b_igate, b_fgate, ret_gn, m_gn, w_ret_up, w_m_up, w_out, norm_ffn2, w1_ffn2, w3_ffn2, w2_ffn2, norm_final):
    bp, lp, _ = x_prompt.shape
    bs, ls, _ = x_sample.shape
    depth = w_ada.shape[0]
    assert depth == 1 and lp % CHUNK == 0 and PROMPT_GROUP_ROWS % lp == 0 and ls <= MAX_GROUP_ROWS
    assert lp & (lp - 1) == 0 and ls & (ls - 1) == 0

    ctx_row = 8 * ((bs + 7) // 8)
    cvec = jnp.zeros((ctx_row + 8, D_MODEL), F32).at[:bs].set(c).at[ctx_row].set(c_ctx)
    rope = _rope_tables(ls)

    xp = x_prompt.reshape(bp * lp, D_MODEL)
    xs = x_sample.reshape(bs * ls, D_MODEL)
    prompt_row = lambda r: ctx_row
    sample_row = lambda r: r // ls

    l = 0
    p = _layer_params(l, norm_ffn1, w1_ffn1, w3_ffn1, w2_ffn1, norm_mix, conv_w, conv_b,
                      ret_decay_logit, b_igate, b_fgate, ret_gn, m_gn, norm_ffn2, norm_final)
    mod = _ada_call(cvec, w_ada[l], b_ada[l][None, :]).reshape(ctx_row + 8, N_MOD, D_MODEL)

    n0 = state_mlstm_n[:, l].astype(F32).reshape(bs, N_PAIR, DH_M)
    m0 = jnp.broadcast_to(state_mlstm_m[:, l].astype(F32).reshape(bs, N_PAIR, 1), (bs, N_PAIR, CHUNK))
    init = (state_ret.astype(F32), state_mlstm_C.astype(F32), n0, m0)

    whole = lambda a: (a, False, ((0, a.shape[1]),))
    bg_start = D_IN - N_DIR * D_MODEL
    xp1, (p["w_in"], p["w_bg"]) = _ffn_call(
        xp, mod, prompt_row, p["norm_ffn1"], p["w1_ffn1"], p["w3_ffn1"], p["w2_ffn1"],
        casts=((w_in[l].T, True, ((0, D_IN), (bg_start, N_DIR * D_MODEL))),))
    xs1, (p["w1_ffn2"], p["w3_ffn2"], p["w2_ffn2"], p["w_ret_up"], p["w_m_up"], p["w_out"]) = _ffn_call(
        xs, mod, sample_row, p["norm_ffn1"], p["w1_ffn1"], p["w3_ffn1"], p["w2_ffn1"],
        casts=(whole(w1_ffn2[l]), whole(w3_ffn2[l]), whole(w2_ffn2[l]),
               whole(w_ret_up[l]), whole(w_m_up[l]), whole(w_out[l])))

    rg_p, hm_p, s_fin, c_fin, n_fin, m_fin = _mixer_core_call(xp1, mod, prompt_row, p, lp, PROMPT_GROUP_ROWS, None, None)
    rg_s, hm_s = _mixer_core_call(xs1, mod, sample_row, p, ls, ls, rope, init)

    yp = _mixer_out_call(xp1, rg_p, hm_p, mod, prompt_row, p)
    ys = _mixer_out_call(xs1, rg_s, hm_s, mod, sample_row, p)

    dt = x_prompt.dtype
    new_c = c_fin
    new_n = n_fin.reshape(bp, 1, N_DIR, H_M, DH_M)
    m_diag = jnp.stack([m_fin[:, d, d * H_M:(d + 1) * H_M, 0] for d in range(N_DIR)], axis=1)
    new_m = m_diag.reshape(bp, 1, N_DIR, H_M)
    return (yp.reshape(bp, lp, D_MODEL).astype(dt), ys.reshape(bs, ls, D_MODEL).astype(dt),
            s_fin.astype(dt), new_c.astype(dt), new_n.astype(dt), new_m.astype(dt))
```

```python
import functools

import jax
import jax.numpy as jnp
from jax import lax
from jax.experimental import pallas as pl
from jax.experimental.pallas import tpu as pltpu

F32 = jnp.float32
BF16 = jnp.bfloat16

D_MODEL = 1024
D_FF = 2816
CHUNK = 128
N_DIR = 2
H_RET = 4
DK_RET = 128
DV_RET = 256
H_M = 4
DH_M = 128
RET_QK = H_RET * DK_RET
RET_V = H_RET * DV_RET
M_W = H_M * DH_M
D_IN = 2 * RET_QK + 2 * RET_V + 4 * M_W + 2 * N_DIR * H_M + N_DIR * D_MODEL
GRID_W = 64
ROPE_BASE = 10000.0
EPS = 1e-6
GN_EPS = 1e-5
N_MOD = 9
N_PAIR = N_DIR * H_M
GATE_LANES = 128
LANES = 128
BF16_SUBLANES = 16
NORM_ROWS = BF16_SUBLANES

MAX_GROUP_ROWS = 1024
PROMPT_GROUP_ROWS = 512
ROW_BLOCK = 256
CHUNK_UNROLL = 8
FFN_TILE = 512
VMEM_LIMIT = 60 * 1024 * 1024


def _sigmoid(x):
    return 1.0 / (1.0 + jnp.exp(-x))


def _silu(x):
    return x * _sigmoid(x)


def _log_sigmoid(x):
    return -(jnp.maximum(-x, 0.0) + jnp.log1p(jnp.exp(-jnp.abs(x))))


def _norm_mod(x, gain, shift, scale):
    ms = jnp.mean(x * x, axis=-1, keepdims=True)
    y = x * lax.rsqrt(ms + EPS) * gain
    return y * (1.0 + scale) + shift


def _dot(a, b):
    return jnp.dot(a, b, preferred_element_type=F32)


def _dot_nt(a, b):
    return lax.dot_general(a, b, (((1,), (1,)), ((), ())), preferred_element_type=F32)


def _head_norm(o, gain):
    mu = jnp.mean(o, axis=-1, keepdims=True)
    oc = o - mu
    var = jnp.mean(oc * oc, axis=-1, keepdims=True)
    return oc * lax.rsqrt(var + GN_EPS) * gain


def _ada_kernel(c_ref, w_ref, b_ref, o_ref):
    s = _silu(c_ref[...]).astype(BF16)
    o_ref[...] = _dot(s, w_ref[...].astype(BF16)) + b_ref[...]


def _ada_call(cvec, w_ada, b_ada):
    rows = cvec.shape[0]
    n_out = w_ada.shape[1]
    tile = D_MODEL
    return pl.pallas_call(
        _ada_kernel,
        out_shape=jax.ShapeDtypeStruct((rows, n_out), F32),
        grid=(n_out // tile,),
        in_specs=[
            pl.BlockSpec((rows, D_MODEL), lambda j: (0, 0)),
            pl.BlockSpec((D_MODEL, tile), lambda j: (0, j)),
            pl.BlockSpec((1, tile), lambda j: (0, j)),
        ],
        out_specs=pl.BlockSpec((rows, tile), lambda j: (0, j)),
        compiler_params=pltpu.CompilerParams(
            dimension_semantics=("arbitrary",), vmem_limit_bytes=VMEM_LIMIT),
        name="ada",
    )(cvec, w_ada, b_ada)


def _swiglu_half_step(x, mod, k0, gain, w1_ref, w3_ref, w2_ref):
    u = _norm_mod(x, gain, mod[k0:k0 + 1], mod[k0 + 1:k0 + 2]).astype(BF16)
    h1 = _dot(u, w1_ref[...])
    h3 = _dot(u, w3_ref[...])
    h = (_silu(h1) * h3).astype(BF16)
    y = _dot(h, w2_ref[...])
    return x + 0.5 * mod[k0 + 2:k0 + 3] * y


def _ffn_kernel(*refs, cast_plan):
    n_in = 6 + len(cast_plan)
    x_ref, mod_ref, g_ref, w1_ref, w3_ref, w2_ref = refs[:6]
    o_ref = refs[n_in]
    o_ref[...] = _swiglu_half_step(x_ref[...], mod_ref[0], 0, g_ref[...], w1_ref, w3_ref, w2_ref)
    cast_out = iter(refs[n_in + 1:])
    for src_ref, (transposed, ranges) in zip(refs[6:n_in], cast_plan):
        block = src_ref[...].T if transposed else src_ref[...]
        for start, width in ranges:
            next(cast_out)[...] = block[:, start:start + width].astype(BF16)


def _const_spec(shape):
    nd = len(shape)
    return pl.BlockSpec(shape, lambda i: (0,) * nd)


def _ffn_call(x, mod, mod_row, gain, w1, w3, w2, casts=()):
    t = x.shape[0]
    tm = FFN_TILE
    steps = t // tm
    in_specs = [
        pl.BlockSpec((tm, D_MODEL), lambda i: (i, 0)),
        pl.BlockSpec((1, N_MOD, D_MODEL), lambda i: (mod_row(i * tm), 0, 0)),
        _const_spec((1, D_MODEL)),
        _const_spec((D_MODEL, D_FF)),
        _const_spec((D_MODEL, D_FF)),
        _const_spec((D_FF, D_MODEL)),
    ]
    out_shape = [jax.ShapeDtypeStruct((t, D_MODEL), F32)]
    out_specs = [pl.BlockSpec((tm, D_MODEL), lambda i: (i, 0))]
    for src, transposed, ranges in casts:
        rows, cols = src.shape[::-1] if transposed else src.shape
        if transposed:
            assert rows % (steps * LANES) == 0
            in_specs.append(pl.BlockSpec((cols, rows // steps), lambda i: (0, i)))
        else:
            assert rows % (steps * BF16_SUBLANES) == 0
            in_specs.append(pl.BlockSpec((rows // steps, cols), lambda i: (i, 0)))
        for _, width in ranges:
            out_shape.append(jax.ShapeDtypeStruct((rows, width), BF16))
            out_specs.append(pl.BlockSpec((rows // steps, width), lambda i: (i, 0)))
    outs = pl.pallas_call(
        functools.partial(_ffn_kernel, cast_plan=tuple((tr, r) for _, tr, r in casts)),
        out_shape=out_shape,
        grid=(steps,),
        in_specs=in_specs,
        out_specs=out_specs,
        compiler_params=pltpu.CompilerParams(
            dimension_semantics=("arbitrary",), vmem_limit_bytes=VMEM_LIMIT),
        name="ffn",
    )(x, mod, gain, w1, w3, w2, *[src for src, _, _ in casts])
    return outs[0], outs[1:]


def _mixer_core_kernel(*refs, group_rows, seq_len, use_rope, has_init, emit_state):
    it = iter(refs)
    x_ref, mod_ref, nmix_ref = next(it), next(it), next(it)
    wrqk_ref, wrv_ref, wrg_ref = next(it), next(it), next(it)
    wmqk_ref, wmv_ref, wmo_ref, wgt_ref = next(it), next(it), next(it), next(it)
    convw_ref, convb_ref, gbias_ref, decay_ref = next(it), next(it), next(it), next(it)
    rgn_ref, mgn_ref = next(it), next(it)
    if use_rope:
        cos_ref, sin_ref = next(it), next(it)
    if has_init:
        s0_ref, c0_ref, n0_ref, m0_ref = next(it), next(it), next(it), next(it)
    rg_out, hm_out = next(it), next(it)
    if emit_state:
        sfin_ref, cfin_ref, nfin_ref, mfin_ref = next(it), next(it), next(it), next(it)
    (u_s, q_s, k_s, v_s, t_s, qt_s, kv_s, wide_s, igb_s, zt_s, wt_s, dec_s, stat_s,
     mprev_s) = (next(it) for _ in range(14))

    rows_total = group_rows
    n_chunks = rows_total // CHUNK
    cps = seq_len // CHUNK
    n_seq = rows_total // seq_len
    n_rb = rows_total // ROW_BLOCK
    mod = mod_ref[0]

    def rb_rows(rb):
        return slice(rb * ROW_BLOCK, (rb + 1) * ROW_BLOCK)

    def head_cols(h, width):
        return slice(h * width, (h + 1) * width)

    chunks_per_rb = ROW_BLOCK // CHUNK
    for rb in range(n_rb):
        rows = rb_rows(rb)
        ub = _norm_mod(x_ref[rows, :], nmix_ref[...], mod[3:4], mod[4:5]).astype(BF16)
        u_s[rows, :] = ub
        qk = _dot(ub, wrqk_ref[...])
        q = qk[:, :RET_QK]
        k = qk[:, RET_QK:] * (DK_RET ** -0.5)
        for h in range(H_RET):
            cols = head_cols(h, DK_RET)
            qh, kh = q[:, cols], k[:, cols]
            if use_rope:
                cs, sn = cos_ref[rows, :], sin_ref[rows, :]
                qh = qh * cs + pltpu.roll(qh, DK_RET // 2, 1) * sn
                kh = kh * cs + pltpu.roll(kh, DK_RET // 2, 1) * sn
            q_s[rows, cols] = qh
            for ch in range(chunks_per_rb):
                t_s[rb * chunks_per_rb + ch, h] = kh[ch * CHUNK:(ch + 1) * CHUNK, :].T
        v_s[rows, :] = _dot(ub, wrv_ref[...]).astype(BF16)

    pos_i = lax.broadcasted_iota(jnp.int32, (CHUNK, CHUNK), 0).astype(F32)
    pos_j = lax.broadcasted_iota(jnp.int32, (CHUNK, CHUNK), 1).astype(F32)
    lg_rows = _log_sigmoid(decay_ref[...])
    for h in range(H_RET):
        lgf = lg_rows[h:h + 1, :CHUNK]
        lgb = lg_rows[H_RET + h:H_RET + h + 1, :CHUNK]
        dec_s[h, 0] = jnp.where(pos_i > pos_j, jnp.exp((pos_i - pos_j) * lgf),
                                jnp.where(pos_i < pos_j, jnp.exp((pos_j - pos_i) * lgb), 2.0))
        dec_s[h, 1] = jnp.exp((CHUNK - 1.0 - pos_j) * lgf)
        dec_s[h, 2] = jnp.exp(pos_j * lgb)
        dec_s[h, 3] = jnp.exp((pos_i + 1.0) * lgf)
        dec_s[h, 4] = jnp.exp((CHUNK - pos_i) * lgb)
    g_chunk = jnp.exp(CHUNK * lg_rows)

    def ret_pass1(n, carry):
        rows = pl.ds(pl.multiple_of(n * CHUNK, CHUNK), CHUNK)
        for h in range(H_RET):
            k_t = t_s[n, h]
            kcat = jnp.concatenate([k_t * dec_s[h, 1], k_t * dec_s[h, 2]], axis=0).astype(BF16)
            kv = _dot(kcat, v_s[rows, head_cols(h, DV_RET)])
            kv_s[n, 2 * h] = kv[:DK_RET]
            kv_s[n, 2 * h + 1] = kv[DK_RET:]
        return carry

    lax.fori_loop(0, n_chunks, ret_pass1, 0, unroll=CHUNK_UNROLL)

    for g in range(n_seq):
        for h in range(H_RET):
            for d in range(N_DIR):
                slot = 2 * h + d
                decay = g_chunk[d * H_RET + h:d * H_RET + h + 1, :]
                state = s0_ref[0, 0, d, h] if has_init else jnp.zeros((DK_RET, DV_RET), F32)
                order = range(g * cps, (g + 1) * cps)
                for n in (order if d == 0 else reversed(order)):
                    local = kv_s[n, slot]
                    kv_s[n, slot] = state
                    state = decay * state + local
                if emit_state:
                    sfin_ref[g, 0, d, h] = state

    def ret_pass2(n, carry):
        rows = pl.ds(pl.multiple_of(n * CHUNK, CHUNK), CHUNK)
        all_scores = [_dot(q_s[rows, head_cols(h, DK_RET)].astype(BF16), t_s[n, h].astype(BF16))
                      for h in range(H_RET)]
        for h in range(H_RET):
            qh = q_s[rows, head_cols(h, DK_RET)]
            vh = v_s[rows, head_cols(h, DV_RET)]
            sm = (all_scores[h] * dec_s[h, 0]).astype(BF16)
            qcat = jnp.concatenate([qh * dec_s[h, 3], qh * dec_s[h, 4]], axis=1).astype(BF16)
            scat = jnp.concatenate([kv_s[n, 2 * h], kv_s[n, 2 * h + 1]], axis=0).astype(BF16)
            o = _dot(sm, vh) + _dot(qcat, scat)
            wide_s[rows, head_cols(h, DV_RET)] = _head_norm(o, rgn_ref[:, head_cols(h, DV_RET)])
        return carry

    lax.fori_loop(0, n_chunks, ret_pass2, 0, unroll=CHUNK_UNROLL)

    for rb in range(n_rb):
        rows = rb_rows(rb)
        rg = _dot(u_s[rows, :], wrg_ref[...])
        rg_out[rows, :] = (wide_s[rows, :] * _silu(rg)).astype(BF16)

    for rb in range(n_rb):
        rows = rb_rows(rb)
        ub = u_s[rows, :]
        wide_s[rows, :] = _dot(ub, wmqk_ref[...])
        mv = _dot(ub, wmv_ref[...])
        for ch in range(chunks_per_rb):
            for h in range(H_M):
                t_s[rb * chunks_per_rb + ch, h] = mv[ch * CHUNK:(ch + 1) * CHUNK, head_cols(h, DH_M)].T
        glane = lax.broadcasted_iota(jnp.int32, (ROW_BLOCK, GATE_LANES), 1)
        gt = jnp.where(glane < 2 * N_PAIR, _dot(ub, wgt_ref[...]) + gbias_ref[...], 0.0)
        igb_s[rows, :] = jnp.where(glane < N_PAIR, gt, _log_sigmoid(gt))

    row_id = lax.broadcasted_iota(jnp.int32, (rows_total, CHUNK), 0)
    seq_pos = row_id & (seq_len - 1)
    for cb in range(2 * M_W // CHUNK):
        cols = slice(cb * CHUNK, (cb + 1) * CHUNK)
        xc = wide_s[:, cols]
        prev = jnp.where(seq_pos == 0, 0.0, pltpu.roll(xc, 1, 0))
        nxt = jnp.where(seq_pos == seq_len - 1, 0.0, pltpu.roll(xc, rows_total - 1, 0))
        y = convb_ref[:, cols] + convw_ref[0:1, cols] * prev
        y = y + convw_ref[1:2, cols] * xc
        y = y + convw_ref[2:3, cols] * nxt
        y = _silu(y)
        if cb < H_M:
            for n in range(n_chunks):
                qt_s[n, cb] = y[n * CHUNK:(n + 1) * CHUNK, :].T.astype(BF16)
        else:
            k_s[:, head_cols(cb - H_M, DH_M)] = y * (DH_M ** -0.5)

    z = igb_s[...]
    cpos = row_id & (CHUNK - 1)
    glane = lax.broadcasted_iota(jnp.int32, (rows_total, GATE_LANES), 1)
    pre, suf = z, z
    step = 1
    while step < CHUNK:
        pre = pre + jnp.where(cpos >= step, pltpu.roll(pre, step, 0), 0.0)
        suf = suf + jnp.where(cpos < CHUNK - step, pltpu.roll(suf, rows_total - step, 0), 0.0)
        step *= 2
    cum = jnp.where(glane < N_PAIR + H_M, pre, suf)
    i_minus_b = z - pltpu.roll(cum, GATE_LANES - N_PAIR, 1)
    igb = jnp.where(glane < N_PAIR, z,
                    jnp.where(glane < 2 * N_PAIR, cum, pltpu.roll(i_minus_b, 2 * N_PAIR, 1)))
    igb_s[...] = igb

    for n in range(n_chunks):
        zt_s[n] = igb[n * CHUNK:(n + 1) * CHUNK, :].T[0:3 * N_PAIR, :]
    ig_all = zt_s[:, 0:N_PAIR, :]
    b_all = zt_s[:, N_PAIR:2 * N_PAIR, :]
    pair = lax.broadcasted_iota(jnp.int32, (n_chunks, N_PAIR, 1), 1)
    b_last = jnp.where(pair < H_M, b_all[:, :, CHUNK - 1:CHUNK], b_all[:, :, 0:1])
    g_all = (b_last - b_all) + ig_all
    m_loc = jnp.max(g_all, axis=-1, keepdims=True)
    wt_s[...] = jnp.exp(g_all - m_loc)
    stat_s[:, 0] = jnp.broadcast_to(b_last, (n_chunks, N_PAIR, CHUNK))
    stat_s[:, 1] = jnp.broadcast_to(m_loc, (n_chunks, N_PAIR, CHUNK))

    def m_pass1(n, carry):
        rows = pl.ds(pl.multiple_of(n * CHUNK, CHUNK), CHUNK)
        w_t = wt_s[n]
        for h in range(H_M):
            kh = k_s[rows, head_cols(h, DH_M)].astype(BF16)
            v_t = t_s[n, h]
            parts = []
            for d in range(N_DIR):
                w_row = w_t[d * H_M + h:d * H_M + h + 1, :]
                parts += [v_t * w_row, jnp.broadcast_to(w_row, (NORM_ROWS, CHUNK))]
            local = _dot(jnp.concatenate(parts, axis=0).astype(BF16), kh)
            for d in range(N_DIR):
                c = d * H_M + h
                base = d * (DH_M + NORM_ROWS)
                kv_s[n, c, :, :DH_M] = local[base:base + DH_M]
                kv_s[n, c, 0:NORM_ROWS, DH_M:] = local[base + DH_M:base + DH_M + NORM_ROWS]
        return carry

    lax.fori_loop(0, n_chunks, m_pass1, 0, unroll=CHUNK_UNROLL)

    for g in range(n_seq):
        for d in range(N_DIR):
            order = list(range(g * cps, (g + 1) * cps))
            if d == 1:
                order.reverse()
            m_prev = m0_ref[0] if has_init else jnp.zeros((N_PAIR, CHUNK), F32)
            a1s, a2s = [], []
            for n in order:
                b_last_n, m_loc_n = stat_s[n, 0], stat_s[n, 1]
                m_new = jnp.maximum(b_last_n + m_prev, m_loc_n)
                a1s.append(jnp.exp(b_last_n + m_prev - m_new))
                a2s.append(jnp.exp(m_loc_n - m_new))
                mprev_s[n, d] = m_prev
                m_prev = m_new
            if emit_state:
                mfin_ref[g, d] = m_prev
            for h in range(H_M):
                c = d * H_M + h
                if has_init:
                    mem = c0_ref[0, 0, d, h]
                    nrm = jnp.broadcast_to(n0_ref[0][c:c + 1, :], (NORM_ROWS, DH_M))
                else:
                    mem = jnp.zeros((DH_M, DH_M), F32)
                    nrm = jnp.zeros((NORM_ROWS, DH_M), F32)
                for idx, n in enumerate(order):
                    a1 = a1s[idx][c:c + 1, :]
                    a2 = a2s[idx][c:c + 1, :]
                    local_mem = kv_s[n, c, :, :DH_M]
                    local_nrm = kv_s[n, c, 0:NORM_ROWS, DH_M:]
                    kv_s[n, c, :, :DH_M] = mem
                    kv_s[n, c, 0:NORM_ROWS, DH_M:] = nrm
                    mem = a1 * mem + a2 * local_mem
                    nrm = a1 * nrm + a2 * local_nrm
                if emit_state:
                    cfin_ref[g, 0, d, h] = mem
                    nfin_ref[g, c:c + 1, :] = nrm[0:1]

    tri_j = lax.broadcasted_iota(jnp.int32, (CHUNK, CHUNK), 0)
    tri_i = lax.broadcasted_iota(jnp.int32, (CHUNK, CHUNK), 1)
    mem_rows = DH_M + NORM_ROWS

    def m_pass2(n, carry):
        rows = pl.ds(pl.multiple_of(n * CHUNK, CHUNK), CHUNK)
        zc = igb_s[rows, :]
        zt = zt_s[n]
        bigs = []
        for h in range(H_M):
            kh = k_s[rows, head_cols(h, DH_M)].astype(BF16)
            mems = [jnp.concatenate([kv_s[n, d * H_M + h, :, :DH_M], kv_s[n, d * H_M + h, 0:NORM_ROWS, DH_M:]],
                                    axis=0).astype(BF16) for d in range(N_DIR)]
            bigs.append(_dot(jnp.concatenate([kh] + mems, axis=0), qt_s[n, h]))
        for h in range(H_M):
            big = bigs[h]
            scores_t = big[:CHUNK]
            weights, crosses, w_ints, m_is = [], [], [], []
            for d in range(N_DIR):
                c = d * H_M + h
                a_col = zc[:, 2 * N_PAIR + c:2 * N_PAIR + c + 1]
                b_row = zt[N_PAIR + c:N_PAIR + c + 1, :]
                mask = (tri_j <= tri_i) if d == 0 else (tri_j >= tri_i)
                dlog = jnp.where(mask, b_row + a_col, -jnp.inf)
                lin = b_row + mprev_s[n, d][c:c + 1, :]
                m_i = jnp.maximum(jnp.max(dlog, axis=0, keepdims=True), lin)
                weights.append(scores_t * jnp.exp(dlog - m_i))
                crosses.append(big[CHUNK + d * mem_rows:CHUNK + (d + 1) * mem_rows])
                w_ints.append(jnp.exp(lin - m_i))
                m_is.append(m_i)
            intra = _dot(t_s[n, h].astype(BF16), jnp.concatenate(weights, axis=1).astype(BF16))
            h_sum_t = jnp.zeros((DH_M, CHUNK), F32)
            for d in range(N_DIR):
                cross, w_int = crosses[d], w_ints[d]
                num = intra[:, d * CHUNK:(d + 1) * CHUNK] + cross[:DH_M] * w_int
                den = jnp.sum(weights[d], axis=0, keepdims=True) + cross[DH_M:DH_M + 1] * w_int
                denom = jnp.maximum(jnp.abs(den), jnp.exp(-m_is[d]))
                h_sum_t = h_sum_t + num * (1.0 / denom)
            wide_s[rows, head_cols(h, DH_M)] = h_sum_t.T
        return carry

    lax.fori_loop(0, n_chunks, m_pass2, 0, unroll=CHUNK_UNROLL)

    for rb in range(n_rb):
        rows = rb_rows(rb)
        mo = _dot(u_s[rows, :], wmo_ref[...])
        hm = _sigmoid(mo) * wide_s[rows, :M_W]
        for h in range(H_M):
            cols = head_cols(h, DH_M)
            hm_out[rows, cols] = _head_norm(hm[:, cols], mgn_ref[:, cols]).astype(BF16)


def _mixer_core_call(x, mod, mod_row, p, seq_len, group_rows, rope, init):
    t = x.shape[0]
    n_groups = t // group_rows
    n_chunks = group_rows // CHUNK
    seq_per_group = group_rows // seq_len
    use_rope = rope is not None
    has_init = init is not None
    emit_state = not has_init

    w_in = p["w_in"]
    groups = []
    start = 0
    for width in (2 * RET_QK, RET_V, RET_V, 2 * M_W, M_W, M_W, GATE_LANES):
        assert start % width == 0
        groups.append(pl.BlockSpec((D_MODEL, width), functools.partial(lambda i, c: (0, c), c=start // width)))
        start += width
    small = [p["conv_w"], p["conv_b"], p["gate_bias"], p["decay_rows"], p["ret_gn"], p["m_gn"]]
    inputs = [x, mod, p["norm_mix"]] + [w_in] * len(groups) + small
    in_specs = [
        pl.BlockSpec((group_rows, D_MODEL), lambda i: (i, 0)),
        pl.BlockSpec((1, N_MOD, D_MODEL), lambda i: (mod_row(i * group_rows), 0, 0)),
        _const_spec(p["norm_mix"].shape),
    ] + groups + [_const_spec(a.shape) for a in small]
    if use_rope:
        inputs += [rope[0], rope[1]]
        in_specs += [_const_spec(rope[0].shape), _const_spec(rope[1].shape)]
    if has_init:
        s0, c0, n0, m0 = init
        inputs += [s0, c0, n0, m0]
        in_specs += [
            pl.BlockSpec((1, 1, N_DIR, H_RET, DK_RET, DV_RET), lambda i: (i, 0, 0, 0, 0, 0)),
            pl.BlockSpec((1, 1, N_DIR, H_M, DH_M, DH_M), lambda i: (i, 0, 0, 0, 0, 0)),
            pl.BlockSpec((1, N_PAIR, DH_M), lambda i: (i, 0, 0)),
            pl.BlockSpec((1, N_PAIR, CHUNK), lambda i: (i, 0, 0)),
        ]

    out_shape = [jax.ShapeDtypeStruct((t, RET_V), BF16), jax.ShapeDtypeStruct((t, M_W), BF16)]
    out_specs = [pl.BlockSpec((group_rows, RET_V), lambda i: (i, 0)),
                 pl.BlockSpec((group_rows, M_W), lambda i: (i, 0))]
    if emit_state:
        n_seq = t // seq_len
        out_shape += [
            jax.ShapeDtypeStruct((n_seq, 1, N_DIR, H_RET, DK_RET, DV_RET), F32),
            jax.ShapeDtypeStruct((n_seq, 1, N_DIR, H_M, DH_M, DH_M), F32),
            jax.ShapeDtypeStruct((n_seq, N_PAIR, DH_M), F32),
            jax.ShapeDtypeStruct((n_seq, N_DIR, N_PAIR, CHUNK), F32),
        ]
        out_specs += [
            pl.BlockSpec((seq_per_group, 1, N_DIR, H_RET, DK_RET, DV_RET), lambda i: (i, 0, 0, 0, 0, 0)),
            pl.BlockSpec((seq_per_group, 1, N_DIR, H_M, DH_M, DH_M), lambda i: (i, 0, 0, 0, 0, 0)),
            pl.BlockSpec((seq_per_group, N_PAIR, DH_M), lambda i: (i, 0, 0)),
            pl.BlockSpec((seq_per_group, N_DIR, N_PAIR, CHUNK), lambda i: (i, 0, 0, 0)),
        ]

    scratch = [
        pltpu.VMEM((group_rows, D_MODEL), BF16),
        pltpu.VMEM((group_rows, RET_QK), F32),
        pltpu.VMEM((group_rows, RET_QK), F32),
        pltpu.VMEM((group_rows, RET_V), BF16),
        pltpu.VMEM((n_chunks, H_M, DH_M, CHUNK), F32),
        pltpu.VMEM((n_chunks, H_M, DH_M, CHUNK), BF16),
        pltpu.VMEM((n_chunks, N_PAIR, DK_RET, DV_RET), F32),
        pltpu.VMEM((group_rows, RET_V), F32),
        pltpu.VMEM((group_rows, GATE_LANES), F32),
        pltpu.VMEM((n_chunks, 3 * N_PAIR, CHUNK), F32),
        pltpu.VMEM((n_chunks, N_PAIR, CHUNK), F32),
        pltpu.VMEM((H_RET, 5, CHUNK, CHUNK), F32),
        pltpu.VMEM((n_chunks, 2, N_PAIR, CHUNK), F32),
        pltpu.VMEM((n_chunks, N_DIR, N_PAIR, CHUNK), F32),
    ]
    kern = functools.partial(_mixer_core_kernel, group_rows=group_rows, seq_len=seq_len, use_rope=use_rope,
                             has_init=has_init, emit_state=emit_state)
    return pl.pallas_call(
        kern,
        out_shape=out_shape,
        grid=(n_groups,),
        in_specs=in_specs,
        out_specs=out_specs,
        scratch_shapes=scratch,
        compiler_params=pltpu.CompilerParams(
            dimension_semantics=("arbitrary",), vmem_limit_bytes=VMEM_LIMIT),
        name="mixer_core",
    )(*inputs)


def _mixer_out_kernel(x_ref, rg_ref, hm_ref, mod_ref, nmix_ref, wbg_ref, wru_ref, wmu_ref, wout_ref,
                      nffn_ref, w1_ref, w3_ref, w2_ref, nfin_ref, o_ref):
    mod = mod_ref[0]
    x = x_ref[...]
    u = _norm_mod(x, nmix_ref[...], mod[3:4], mod[4:5]).astype(BF16)
    gates = _sigmoid(_dot(u, wbg_ref[...]))
    ret_branch = _dot(rg_ref[...], wru_ref[...])
    m_branch = _dot(hm_ref[...], wmu_ref[...])
    merged = (gates[:, :D_MODEL] * ret_branch + gates[:, D_MODEL:] * m_branch).astype(BF16)
    x = x + mod[5:6] * _dot(merged, wout_ref[...])
    x = _swiglu_half_step(x, mod, 6, nffn_ref[...], w1_ref, w3_ref, w2_ref)
    ms = jnp.mean(x * x, axis=-1, keepdims=True)
    o_ref[...] = x * lax.rsqrt(ms + EPS) * nfin_ref[...]


def _mixer_out_call(x, rgated, hmn, mod, mod_row, p):
    t = x.shape[0]
    tm = FFN_TILE
    consts = [p["norm_mix"], p["w_bg"], p["w_ret_up"], p["w_m_up"], p["w_out"],
              p["norm_ffn2"], p["w1_ffn2"], p["w3_ffn2"], p["w2_ffn2"], p["norm_final"]]
    return pl.pallas_call(
        _mixer_out_kernel,
        out_shape=jax.ShapeDtypeStruct((t, D_MODEL), F32),
        grid=(t // tm,),
        in_specs=[
            pl.BlockSpec((tm, D_MODEL), lambda i: (i, 0)),
            pl.BlockSpec((tm, RET_V), lambda i: (i, 0)),
            pl.BlockSpec((tm, M_W), lambda i: (i, 0)),
            pl.BlockSpec((1, N_MOD, D_MODEL), lambda i: (mod_row(i * tm), 0, 0)),
        ] + [_const_spec(a.shape) for a in consts],
        out_specs=pl.BlockSpec((tm, D_MODEL), lambda i: (i, 0)),
        compiler_params=pltpu.CompilerParams(
            dimension_semantics=("arbitrary",), vmem_limit_bytes=VMEM_LIMIT),
        name="mixer_out",
    )(x, rgated, hmn, mod, *consts)


def _rope_tables(seq_len):
    rows = seq_len // GRID_W
    r = jnp.repeat(jnp.arange(rows, dtype=F32), GRID_W)
    col = (jnp.arange(seq_len) % GRID_W).astype(F32)
    n_f = DK_RET // 4
    freqs = ROPE_BASE ** (-jnp.arange(n_f, dtype=F32) / n_f)
    ang = jnp.concatenate([r[:, None] * freqs, col[:, None] * freqs], axis=-1)
    cos, sin = jnp.cos(ang), jnp.sin(ang)
    return jnp.concatenate([cos, cos], axis=-1), jnp.concatenate([-sin, sin], axis=-1)


def _layer_params(l, norm_ffn1, w1_ffn1, w3_ffn1, w2_ffn1, norm_mix, conv_w, conv_b,
                  ret_decay_logit, b_igate, b_fgate, ret_gn, m_gn, norm_ffn2, norm_final):
    gate_bias = jnp.pad(jnp.concatenate([b_igate[l], b_fgate[l]]), (0, GATE_LANES - 2 * N_PAIR))
    p = dict(
        norm_ffn1=norm_ffn1[l][None, :], norm_mix=norm_mix[l][None, :], norm_ffn2=norm_ffn2[l][None, :],
        norm_final=norm_final[None, :],
        w1_ffn1=w1_ffn1[l].astype(BF16), w3_ffn1=w3_ffn1[l].astype(BF16), w2_ffn1=w2_ffn1[l].astype(BF16),
        conv_w=conv_w[l], conv_b=conv_b[l][None, :],
        gate_bias=gate_bias[None, :],
        decay_rows=jnp.broadcast_to(ret_decay_logit[l].reshape(N_DIR * H_RET, 1), (N_DIR * H_RET, DV_RET)),
        ret_gn=ret_gn[l][None, :], m_gn=m_gn[l][None, :],
    )
    return p


def kernel(x_prompt, x_sample, c, state_ret, state_mlstm_C, state_mlstm_n, state_mlstm_m, c_ctx, w_ada, b_ada, norm_ffn1, w1_ffn1, w3_ffn1, w2_ffn1, norm_mix, w_in, conv_w, conv_b, ret_decay_logit, b_igate, b_fgate, ret_gn, m_gn, w_ret_up, w_m_up, w_out, norm_ffn2, w1_ffn2, w3_ffn2, w2_ffn2, norm_final):
    bp, lp, _ = x_prompt.shape
    bs, ls, _ = x_sample.shape
    depth = w_ada.shape[0]
    assert depth == 1 and lp % CHUNK == 0 and PROMPT_GROUP_ROWS % lp == 0 and ls <= MAX_GROUP_ROWS
    assert lp & (lp - 1) == 0 and ls & (ls - 1) == 0

    ctx_row = 8 * ((bs + 7) // 8)
    cvec = jnp.zeros((ctx_row + 8, D_MODEL), F32).at[:bs].set(c).at[ctx_row].set(c_ctx)
    rope = _rope_tables(ls)

    xp = x_prompt.reshape(bp * lp, D_MODEL)
    xs = x_sample.reshape(bs * ls, D_MODEL)
    prompt_row = lambda r: ctx_row
    sample_row = lambda r: r // ls

    l = 0
    p = _layer_params(l, norm_ffn1, w1_ffn1, w3_ffn1, w2_ffn1, norm_mix, conv_w, conv_b,
                      ret_decay_logit, b_igate, b_fgate, ret_gn, m_gn, norm_ffn2, norm_final)
    mod = _ada_call(cvec, w_ada[l], b_ada[l][None, :]).reshape(ctx_row + 8, N_MOD, D_MODEL)

    n0 = state_mlstm_n[:, l].astype(F32).reshape(bs, N_PAIR, DH_M)
    m0 = jnp.broadcast_to(state_mlstm_m[:, l].astype(F32).reshape(bs, N_PAIR, 1), (bs, N_PAIR, CHUNK))
    init = (state_ret.astype(F32), state_mlstm_C.astype(F32), n0, m0)

    whole = lambda a: (a, False, ((0, a.shape[1]),))
    bg_start = D_IN - N_DIR * D_MODEL
    xp1, (p["w_in"], p["w_bg"]) = _ffn_call(
        xp, mod, prompt_row, p["norm_ffn1"], p["w1_ffn1"], p["w3_ffn1"], p["w2_ffn1"],
        casts=((w_in[l].T, True, ((0, D_IN), (bg_start, N_DIR * D_MODEL))),))
    xs1, (p["w1_ffn2"], p["w3_ffn2"], p["w2_ffn2"], p["w_ret_up"], p["w_m_up"], p["w_out"]) = _ffn_call(
        xs, mod, sample_row, p["norm_ffn1"], p["w1_ffn1"], p["w3_ffn1"], p["w2_ffn1"],
        casts=(whole(w1_ffn2[l]), whole(w3_ffn2[l]), whole(w2_ffn2[l]),
               whole(w_ret_up[l]), whole(w_m_up[l]), whole(w_out[l])))

    rg_p, hm_p, s_fin, c_fin, n_fin, m_fin = _mixer_core_call(xp1, mod, prompt_row, p, lp, PROMPT_GROUP_ROWS, None, None)
    rg_s, hm_s = _mixer_core_call(xs1, mod, sample_row, p, ls, ls, rope, init)

    yp = _mixer_out_call(xp1, rg_p, hm_p, mod, prompt_row, p)
    ys = _mixer_out_call(xs1, rg_s, hm_s, mod, sample_row, p)

    dt = x_prompt.dtype
    new_c = c_fin
    new_n = n_fin.reshape(bp, 1, N_DIR, H_M, DH_M)
    m_diag = jnp.stack([m_fin[:, d, d * H_M:(d + 1) * H_M, 0] for d in range(N_DIR)], axis=1)
    new_m = m_diag.reshape(bp, 1, N_DIR, H_M)
    return (yp.reshape(bp, lp, D_MODEL).astype(dt), ys.reshape(bs, ls, D_MODEL).astype(dt),
            s_fin.astype(dt), new_c.astype(dt), new_n.astype(dt), new_m.astype(dt))
```

```python
import functools

import jax
import jax.numpy as jnp
from jax import lax
from jax.experimental import pallas as pl
from jax.experimental.pallas import tpu as pltpu

F32 = jnp.float32
BF16 = jnp.bfloat16

D_MODEL = 1024
D_FF = 2816
CHUNK = 128
N_DIR = 2
H_RET = 4
DK_RET = 128
DV_RET = 256
H_M = 4
DH_M = 128
RET_QK = H_RET * DK_RET
RET_V = H_RET * DV_RET
M_W = H_M * DH_M
D_IN = 2 * RET_QK + 2 * RET_V + 4 * M_W + 2 * N_DIR * H_M + N_DIR * D_MODEL
GRID_W = 64
ROPE_BASE = 10000.0
EPS = 1e-6
GN_EPS = 1e-5
N_MOD = 9
N_PAIR = N_DIR * H_M
GATE_LANES = 128
LANES = 128
BF16_SUBLANES = 16
NORM_ROWS = BF16_SUBLANES

MAX_GROUP_ROWS = 1024
PROMPT_GROUP_ROWS = 512
ROW_BLOCK = 256
FFN_TILE = 512
VMEM_LIMIT = 60 * 1024 * 1024


def _sigmoid(x):
    return 1.0 / (1.0 + jnp.exp(-x))


def _silu(x):
    return x * _sigmoid(x)


def _log_sigmoid(x):
    return -(jnp.maximum(-x, 0.0) + jnp.log1p(jnp.exp(-jnp.abs(x))))


def _norm_mod(x, gain, shift, scale):
    ms = jnp.mean(x * x, axis=-1, keepdims=True)
    y = x * lax.rsqrt(ms + EPS) * gain
    return y * (1.0 + scale) + shift


def _dot(a, b):
    return jnp.dot(a, b, preferred_element_type=F32)


def _dot_nt(a, b):
    return lax.dot_general(a, b, (((1,), (1,)), ((), ())), preferred_element_type=F32)


def _head_norm(o, gain):
    mu = jnp.mean(o, axis=-1, keepdims=True)
    oc = o - mu
    var = jnp.mean(oc * oc, axis=-1, keepdims=True)
    return oc * lax.rsqrt(var + GN_EPS) * gain


def _ada_kernel(c_ref, w_ref, b_ref, o_ref):
    s = _silu(c_ref[...]).astype(BF16)
    o_ref[...] = _dot(s, w_ref[...].astype(BF16)) + b_ref[...]


def _ada_call(cvec, w_ada, b_ada):
    rows = cvec.shape[0]
    n_out = w_ada.shape[1]
    tile = D_MODEL
    return pl.pallas_call(
        _ada_kernel,
        out_shape=jax.ShapeDtypeStruct((rows, n_out), F32),
        grid=(n_out // tile,),
        in_specs=[
            pl.BlockSpec((rows, D_MODEL), lambda j: (0, 0)),
            pl.BlockSpec((D_MODEL, tile), lambda j: (0, j)),
            pl.BlockSpec((1, tile), lambda j: (0, j)),
        ],
        out_specs=pl.BlockSpec((rows, tile), lambda j: (0, j)),
        compiler_params=pltpu.CompilerParams(
            dimension_semantics=("arbitrary",), vmem_limit_bytes=VMEM_LIMIT),
        name="ada",
    )(cvec, w_ada, b_ada)


def _swiglu_half_step(x, mod, k0, gain, w1_ref, w3_ref, w2_ref):
    u = _norm_mod(x, gain, mod[k0:k0 + 1], mod[k0 + 1:k0 + 2]).astype(BF16)
    h1 = _dot(u, w1_ref[...])
    h3 = _dot(u, w3_ref[...])
    h = (_silu(h1) * h3).astype(BF16)
    y = _dot(h, w2_ref[...])
    return x + 0.5 * mod[k0 + 2:k0 + 3] * y


def _ffn_kernel(*refs, cast_plan):
    n_in = 6 + len(cast_plan)
    x_ref, mod_ref, g_ref, w1_ref, w3_ref, w2_ref = refs[:6]
    o_ref = refs[n_in]
    o_ref[...] = _swiglu_half_step(x_ref[...], mod_ref[0], 0, g_ref[...], w1_ref, w3_ref, w2_ref)
    cast_out = iter(refs[n_in + 1:])
    for src_ref, (transposed, ranges) in zip(refs[6:n_in], cast_plan):
        block = src_ref[...].T if transposed else src_ref[...]
        for start, width in ranges:
            next(cast_out)[...] = block[:, start:start + width].astype(BF16)


def _const_spec(shape):
    nd = len(shape)
    return pl.BlockSpec(shape, lambda i: (0,) * nd)


def _ffn_call(x, mod, mod_row, gain, w1, w3, w2, casts=()):
    t = x.shape[0]
    tm = FFN_TILE
    steps = t // tm
    in_specs = [
        pl.BlockSpec((tm, D_MODEL), lambda i: (i, 0)),
        pl.BlockSpec((1, N_MOD, D_MODEL), lambda i: (mod_row(i * tm), 0, 0)),
        _const_spec((1, D_MODEL)),
        _const_spec((D_MODEL, D_FF)),
        _const_spec((D_MODEL, D_FF)),
        _const_spec((D_FF, D_MODEL)),
    ]
    out_shape = [jax.ShapeDtypeStruct((t, D_MODEL), F32)]
    out_specs = [pl.BlockSpec((tm, D_MODEL), lambda i: (i, 0))]
    for src, transposed, ranges in casts:
        rows, cols = src.shape[::-1] if transposed else src.shape
        if transposed:
            assert rows % (steps * LANES) == 0
            in_specs.append(pl.BlockSpec((cols, rows // steps), lambda i: (0, i)))
        else:
            assert rows % (steps * BF16_SUBLANES) == 0
            in_specs.append(pl.BlockSpec((rows // steps, cols), lambda i: (i, 0)))
        for _, width in ranges:
            out_shape.append(jax.ShapeDtypeStruct((rows, width), BF16))
            out_specs.append(pl.BlockSpec((rows // steps, width), lambda i: (i, 0)))
    outs = pl.pallas_call(
        functools.partial(_ffn_kernel, cast_plan=tuple((tr, r) for _, tr, r in casts)),
        out_shape=out_shape,
        grid=(steps,),
        in_specs=in_specs,
        out_specs=out_specs,
        compiler_params=pltpu.CompilerParams(
            dimension_semantics=("arbitrary",), vmem_limit_bytes=VMEM_LIMIT),
        name="ffn",
    )(x, mod, gain, w1, w3, w2, *[src for src, _, _ in casts])
    return outs[0], outs[1:]


def _mixer_core_kernel(*refs, group_rows, seq_len, use_rope, has_init, emit_state):
    it = iter(refs)
    x_ref, mod_ref, nmix_ref = next(it), next(it), next(it)
    wrqk_ref, wrv_ref, wrg_ref = next(it), next(it), next(it)
    wmqk_ref, wmv_ref, wmo_ref, wgt_ref = next(it), next(it), next(it), next(it)
    convw_ref, convb_ref, gbias_ref, decay_ref = next(it), next(it), next(it), next(it)
    rgn_ref, mgn_ref = next(it), next(it)
    if use_rope:
        cos_ref, sin_ref = next(it), next(it)
    if has_init:
        s0_ref, c0_ref, n0_ref, m0_ref = next(it), next(it), next(it), next(it)
    rg_out, hm_out = next(it), next(it)
    if emit_state:
        sfin_ref, cfin_ref, nfin_ref, mfin_ref = next(it), next(it), next(it), next(it)
    (u_s, q_s, k_s, v_s, t_s, qt_s, kv_s, wide_s, gate_s, pre_s, igb_s, zt_s, wt_s, dec_s, gch_s, stat_s,
     mprev_s) = (next(it) for _ in range(17))

    rows_total = group_rows
    n_chunks = rows_total // CHUNK
    cps = seq_len // CHUNK
    n_seq = rows_total // seq_len
    n_rb = rows_total // ROW_BLOCK
    mod = mod_ref[0]

    def rb_rows(rb):
        return slice(rb * ROW_BLOCK, (rb + 1) * ROW_BLOCK)

    def chunk_rows(n):
        return slice(n * CHUNK, (n + 1) * CHUNK)

    def head_cols(h, width):
        return slice(h * width, (h + 1) * width)

    @pl.when(pl.program_id(0) == 0)
    def _():
        pos_i = lax.broadcasted_iota(jnp.int32, (CHUNK, CHUNK), 0).astype(F32)
        pos_j = lax.broadcasted_iota(jnp.int32, (CHUNK, CHUNK), 1).astype(F32)
        lg_rows = _log_sigmoid(decay_ref[...])
        for h in range(H_RET):
            lgf = lg_rows[h:h + 1, :CHUNK]
            lgb = lg_rows[H_RET + h:H_RET + h + 1, :CHUNK]
            dec_s[h, 0] = jnp.where(pos_i > pos_j, jnp.exp((pos_i - pos_j) * lgf),
                                    jnp.where(pos_i < pos_j, jnp.exp((pos_j - pos_i) * lgb), 2.0))
            dec_s[h, 1] = jnp.exp((CHUNK - 1.0 - pos_j) * lgf)
            dec_s[h, 2] = jnp.exp(pos_j * lgb)
            dec_s[h, 3] = jnp.exp((pos_i + 1.0) * lgf)
            dec_s[h, 4] = jnp.exp((CHUNK - pos_i) * lgb)
        gch_s[...] = jnp.exp(CHUNK * lg_rows)

    g_chunk = gch_s[...]

    chunks_per_rb = ROW_BLOCK // CHUNK
    for rb in range(n_rb):
        rows = rb_rows(rb)
        u_s[rows, :] = _norm_mod(x_ref[rows, :], nmix_ref[...], mod[3:4], mod[4:5]).astype(BF16)
        ub = u_s[rows, :]
        qk = _dot(ub, wrqk_ref[...])
        q = qk[:, :RET_QK]
        k = qk[:, RET_QK:] * (DK_RET ** -0.5)
        for h in range(H_RET):
            cols = head_cols(h, DK_RET)
            qh, kh = q[:, cols], k[:, cols]
            if use_rope:
                cs, sn = cos_ref[rows, :], sin_ref[rows, :]
                qh = qh * cs + pltpu.roll(qh, DK_RET // 2, 1) * sn
                kh = kh * cs + pltpu.roll(kh, DK_RET // 2, 1) * sn
            q_s[rows, cols] = qh
            for ch in range(chunks_per_rb):
                t_s[rb * chunks_per_rb + ch, h] = kh[ch * CHUNK:(ch + 1) * CHUNK, :].T
        v_s[rows, :] = _dot(ub, wrv_ref[...]).astype(BF16)

    for n in range(n_chunks):
        for h in range(H_RET):
            k_t = t_s[n, h]
            kcat = jnp.concatenate([k_t * dec_s[h, 1], k_t * dec_s[h, 2]], axis=0).astype(BF16)
            kv = _dot(kcat, v_s[chunk_rows(n), head_cols(h, DV_RET)])
            kv_s[n, 2 * h] = kv[:DK_RET]
            kv_s[n, 2 * h + 1] = kv[DK_RET:]

    for rb in range(n_rb):
        gate_s[rb_rows(rb), :] = _dot(u_s[rb_rows(rb), :], wrg_ref[...])

    for g in range(n_seq):
        for h in range(H_RET):
            for d in range(N_DIR):
                slot = 2 * h + d
                decay = g_chunk[d * H_RET + h:d * H_RET + h + 1, :]
                state = s0_ref[0, 0, d, h] if has_init else jnp.zeros((DK_RET, DV_RET), F32)
                order = range(g * cps, (g + 1) * cps)
                for n in (order if d == 0 else reversed(order)):
                    local = kv_s[n, slot]
                    kv_s[n, slot] = state
                    state = decay * state + local
                if emit_state:
                    sfin_ref[g, 0, d, h] = state

    def ret_scores(n):
        return [_dot(q_s[chunk_rows(n), head_cols(h, DK_RET)].astype(BF16), t_s[n, h].astype(BF16))
                for h in range(H_RET)]

    def ret_outputs(n, all_scores):
        rows = chunk_rows(n)
        for h in range(H_RET):
            qh = q_s[rows, head_cols(h, DK_RET)]
            vh = v_s[rows, head_cols(h, DV_RET)]
            sm = (all_scores[h] * dec_s[h, 0]).astype(BF16)
            qcat = jnp.concatenate([qh * dec_s[h, 3], qh * dec_s[h, 4]], axis=1).astype(BF16)
            scat = jnp.concatenate([kv_s[n, 2 * h], kv_s[n, 2 * h + 1]], axis=0).astype(BF16)
            o = _dot(sm, vh) + _dot(qcat, scat)
            wide_s[rows, head_cols(h, DV_RET)] = _head_norm(o, rgn_ref[:, head_cols(h, DV_RET)])

    for rb in range(n_rb):
        rows = rb_rows(rb)
        chunks = range(rb * chunks_per_rb, (rb + 1) * chunks_per_rb)
        scores = [ret_scores(n) for n in chunks]
        ub = u_s[rows, :]
        glane = lax.broadcasted_iota(jnp.int32, (ROW_BLOCK, GATE_LANES), 1)
        gt = jnp.where(glane < 2 * N_PAIR, _dot(ub, wgt_ref[...]) + gbias_ref[...], 0.0)
        igb_s[rows, :] = jnp.where(glane < N_PAIR, gt, _log_sigmoid(gt))
        pre_s[rows, :] = _dot(ub, wmqk_ref[...])
        for n, sc in zip(chunks, scores):
            ret_outputs(n, sc)

    for rb in range(n_rb):
        rows = rb_rows(rb)
        rg_out[rows, :] = (wide_s[rows, :] * _silu(gate_s[rows, :])).astype(BF16)

    for rb in range(n_rb):
        mv = _dot(u_s[rb_rows(rb), :], wmv_ref[...])
        for ch in range(chunks_per_rb):
            for h in range(H_M):
                t_s[rb * chunks_per_rb + ch, h] = mv[ch * CHUNK:(ch + 1) * CHUNK, head_cols(h, DH_M)].T

    row_id = lax.broadcasted_iota(jnp.int32, (rows_total, CHUNK), 0)
    seq_pos = row_id & (seq_len - 1)
    for cb in range(2 * M_W // CHUNK):
        cols = slice(cb * CHUNK, (cb + 1) * CHUNK)
        xc = pre_s[:, cols]
        prev = jnp.where(seq_pos == 0, 0.0, pltpu.roll(xc, 1, 0))
        nxt = jnp.where(seq_pos == seq_len - 1, 0.0, pltpu.roll(xc, rows_total - 1, 0))
        y = convb_ref[:, cols] + convw_ref[0:1, cols] * prev
        y = y + convw_ref[1:2, cols] * xc
        y = y + convw_ref[2:3, cols] * nxt
        y = _silu(y)
        if cb < H_M:
            for n in range(n_chunks):
                qt_s[n, cb] = y[n * CHUNK:(n + 1) * CHUNK, :].T.astype(BF16)
        else:
            k_s[:, head_cols(cb - H_M, DH_M)] = y * (DH_M ** -0.5)

    z = igb_s[...]
    cpos = row_id & (CHUNK - 1)
    glane = lax.broadcasted_iota(jnp.int32, (rows_total, GATE_LANES), 1)
    pre, suf = z, z
    step = 1
    while step < CHUNK:
        pre = pre + jnp.where(cpos >= step, pltpu.roll(pre, step, 0), 0.0)
        suf = suf + jnp.where(cpos < CHUNK - step, pltpu.roll(suf, rows_total - step, 0), 0.0)
        step *= 2
    cum = jnp.where(glane < N_PAIR + H_M, pre, suf)
    i_minus_b = z - pltpu.roll(cum, GATE_LANES - N_PAIR, 1)
    igb = jnp.where(glane < N_PAIR, z,
                    jnp.where(glane < 2 * N_PAIR, cum, pltpu.roll(i_minus_b, 2 * N_PAIR, 1)))
    igb_s[...] = igb

    for n in range(n_chunks):
        zt_s[n] = igb[n * CHUNK:(n + 1) * CHUNK, :].T[0:3 * N_PAIR, :]
    ig_all = zt_s[:, 0:N_PAIR, :]
    b_all = zt_s[:, N_PAIR:2 * N_PAIR, :]
    pair = lax.broadcasted_iota(jnp.int32, (n_chunks, N_PAIR, 1), 1)
    b_last = jnp.where(pair < H_M, b_all[:, :, CHUNK - 1:CHUNK], b_all[:, :, 0:1])
    g_all = (b_last - b_all) + ig_all
    m_loc = jnp.max(g_all, axis=-1, keepdims=True)
    wt_s[...] = jnp.exp(g_all - m_loc)
    stat_s[:, 0] = jnp.broadcast_to(b_last, (n_chunks, N_PAIR, CHUNK))
    stat_s[:, 1] = jnp.broadcast_to(m_loc, (n_chunks, N_PAIR, CHUNK))

    for n in range(n_chunks):
        rows = chunk_rows(n)
        w_t = wt_s[n]
        for h in range(H_M):
            kh = k_s[rows, head_cols(h, DH_M)].astype(BF16)
            v_t = t_s[n, h]
            parts = []
            for d in range(N_DIR):
                w_row = w_t[d * H_M + h:d * H_M + h + 1, :]
                parts += [v_t * w_row, jnp.broadcast_to(w_row, (NORM_ROWS, CHUNK))]
            local = _dot(jnp.concatenate(parts, axis=0).astype(BF16), kh)
            for d in range(N_DIR):
                c = d * H_M + h
                base = d * (DH_M + NORM_ROWS)
                kv_s[n, c, :, :DH_M] = local[base:base + DH_M]
                kv_s[n, c, 0:NORM_ROWS, DH_M:] = local[base + DH_M:base + DH_M + NORM_ROWS]

    for rb in range(n_rb):
        gate_s[rb_rows(rb), :M_W] = _dot(u_s[rb_rows(rb), :], wmo_ref[...])

    for g in range(n_seq):
        for d in range(N_DIR):
            order = list(range(g * cps, (g + 1) * cps))
            if d == 1:
                order.reverse()
            m_prev = m0_ref[0] if has_init else jnp.zeros((N_PAIR, CHUNK), F32)
            a1s, a2s = [], []
            for n in order:
                b_last_n, m_loc_n = stat_s[n, 0], stat_s[n, 1]
                m_new = jnp.maximum(b_last_n + m_prev, m_loc_n)
                a1s.append(jnp.exp(b_last_n + m_prev - m_new))
                a2s.append(jnp.exp(m_loc_n - m_new))
                mprev_s[n, d] = m_prev
                m_prev = m_new
            if emit_state:
                mfin_ref[g, d] = m_prev
            for h in range(H_M):
                c = d * H_M + h
                if has_init:
                    mem = c0_ref[0, 0, d, h]
                    nrm = jnp.broadcast_to(n0_ref[0][c:c + 1, :], (NORM_ROWS, DH_M))
                else:
                    mem = jnp.zeros((DH_M, DH_M), F32)
                    nrm = jnp.zeros((NORM_ROWS, DH_M), F32)
                for idx, n in enumerate(order):
                    a1 = a1s[idx][c:c + 1, :]
                    a2 = a2s[idx][c:c + 1, :]
                    local_mem = kv_s[n, c, :, :DH_M]
                    local_nrm = kv_s[n, c, 0:NORM_ROWS, DH_M:]
                    kv_s[n, c, :, :DH_M] = mem
                    kv_s[n, c, 0:NORM_ROWS, DH_M:] = nrm
                    mem = a1 * mem + a2 * local_mem
                    nrm = a1 * nrm + a2 * local_nrm
                if emit_state:
                    cfin_ref[g, 0, d, h] = mem
                    nfin_ref[g, c:c + 1, :] = nrm[0:1]

    tri_j = lax.broadcasted_iota(jnp.int32, (CHUNK, CHUNK), 0)
    tri_i = lax.broadcasted_iota(jnp.int32, (CHUNK, CHUNK), 1)
    mem_rows = DH_M + NORM_ROWS

    for n in range(n_chunks):
        rows = chunk_rows(n)
        zc = igb_s[rows, :]
        zt = zt_s[n]
        bigs = []
        for h in range(H_M):
            kh = k_s[rows, head_cols(h, DH_M)].astype(BF16)
            mems = [jnp.concatenate([kv_s[n, d * H_M + h, :, :DH_M], kv_s[n, d * H_M + h, 0:NORM_ROWS, DH_M:]],
                                    axis=0).astype(BF16) for d in range(N_DIR)]
            bigs.append(_dot(jnp.concatenate([kh] + mems, axis=0), qt_s[n, h]))
        for h in range(H_M):
            big = bigs[h]
            scores_t = big[:CHUNK]
            weights, crosses, w_ints, m_is = [], [], [], []
            for d in range(N_DIR):
                c = d * H_M + h
                a_col = zc[:, 2 * N_PAIR + c:2 * N_PAIR + c + 1]
                b_row = zt[N_PAIR + c:N_PAIR + c + 1, :]
                mask = (tri_j <= tri_i) if d == 0 else (tri_j >= tri_i)
                dlog = jnp.where(mask, b_row + a_col, -jnp.inf)
                lin = b_row + mprev_s[n, d][c:c + 1, :]
                m_i = jnp.maximum(jnp.max(dlog, axis=0, keepdims=True), lin)
                weights.append(scores_t * jnp.exp(dlog - m_i))
                crosses.append(big[CHUNK + d * mem_rows:CHUNK + (d + 1) * mem_rows])
                w_ints.append(jnp.exp(lin - m_i))
                m_is.append(m_i)
            intra = _dot(t_s[n, h].astype(BF16), jnp.concatenate(weights, axis=1).astype(BF16))
            h_sum_t = jnp.zeros((DH_M, CHUNK), F32)
            for d in range(N_DIR):
                cross, w_int = crosses[d], w_ints[d]
                num = intra[:, d * CHUNK:(d + 1) * CHUNK] + cross[:DH_M] * w_int
                den = jnp.sum(weights[d], axis=0, keepdims=True) + cross[DH_M:DH_M + 1] * w_int
                denom = jnp.maximum(jnp.abs(den), jnp.exp(-m_is[d]))
                h_sum_t = h_sum_t + num * (1.0 / denom)
            wide_s[rows, head_cols(h, DH_M)] = h_sum_t.T

    for rb in range(n_rb):
        rows = rb_rows(rb)
        hm = _sigmoid(gate_s[rows, :M_W]) * wide_s[rows, :M_W]
        for h in range(H_M):
            cols = head_cols(h, DH_M)
            hm_out[rows, cols] = _head_norm(hm[:, cols], mgn_ref[:, cols]).astype(BF16)


def _mixer_core_call(x, mod, mod_row, p, seq_len, group_rows, rope, init):
    t = x.shape[0]
    n_groups = t // group_rows
    n_chunks = group_rows // CHUNK
    seq_per_group = group_rows // seq_len
    use_rope = rope is not None
    has_init = init is not None
    emit_state = not has_init

    w_in = p["w_in"]
    groups = []
    start = 0
    for width in (2 * RET_QK, RET_V, RET_V, 2 * M_W, M_W, M_W, GATE_LANES):
        assert start % width == 0
        groups.append(pl.BlockSpec((D_MODEL, width), functools.partial(lambda i, c: (0, c), c=start // width)))
        start += width
    small = [p["conv_w"], p["conv_b"], p["gate_bias"], p["decay_rows"], p["ret_gn"], p["m_gn"]]
    inputs = [x, mod, p["norm_mix"]] + [w_in] * len(groups) + small
    in_specs = [
        pl.BlockSpec((group_rows, D_MODEL), lambda i: (i, 0)),
        pl.BlockSpec((1, N_MOD, D_MODEL), lambda i: (mod_row(i * group_rows), 0, 0)),
        _const_spec(p["norm_mix"].shape),
    ] + groups + [_const_spec(a.shape) for a in small]
    if use_rope:
        inputs += [rope[0], rope[1]]
        in_specs += [_const_spec(rope[0].shape), _const_spec(rope[1].shape)]
    if has_init:
        s0, c0, n0, m0 = init
        inputs += [s0, c0, n0, m0]
        in_specs += [
            pl.BlockSpec((1, 1, N_DIR, H_RET, DK_RET, DV_RET), lambda i: (i, 0, 0, 0, 0, 0)),
            pl.BlockSpec((1, 1, N_DIR, H_M, DH_M, DH_M), lambda i: (i, 0, 0, 0, 0, 0)),
            pl.BlockSpec((1, N_PAIR, DH_M), lambda i: (i, 0, 0)),
            pl.BlockSpec((1, N_PAIR, CHUNK), lambda i: (i, 0, 0)),
        ]

    out_shape = [jax.ShapeDtypeStruct((t, RET_V), BF16), jax.ShapeDtypeStruct((t, M_W), BF16)]
    out_specs = [pl.BlockSpec((group_rows, RET_V), lambda i: (i, 0)),
                 pl.BlockSpec((group_rows, M_W), lambda i: (i, 0))]
    if emit_state:
        n_seq = t // seq_len
        out_shape += [
            jax.ShapeDtypeStruct((n_seq, 1, N_DIR, H_RET, DK_RET, DV_RET), F32),
            jax.ShapeDtypeStruct((n_seq, 1, N_DIR, H_M, DH_M, DH_M), F32),
            jax.ShapeDtypeStruct((n_seq, N_PAIR, DH_M), F32),
            jax.ShapeDtypeStruct((n_seq, N_DIR, N_PAIR, CHUNK), F32),
        ]
        out_specs += [
            pl.BlockSpec((seq_per_group, 1, N_DIR, H_RET, DK_RET, DV_RET), lambda i: (i, 0, 0, 0, 0, 0)),
            pl.BlockSpec((seq_per_group, 1, N_DIR, H_M, DH_M, DH_M), lambda i: (i, 0, 0, 0, 0, 0)),
            pl.BlockSpec((seq_per_group, N_PAIR, DH_M), lambda i: (i, 0, 0)),
            pl.BlockSpec((seq_per_group, N_DIR, N_PAIR, CHUNK), lambda i: (i, 0, 0, 0)),
        ]

    scratch = [
        pltpu.VMEM((group_rows, D_MODEL), BF16),
        pltpu.VMEM((group_rows, RET_QK), F32),
        pltpu.VMEM((group_rows, RET_QK), F32),
        pltpu.VMEM((group_rows, RET_V), BF16),
        pltpu.VMEM((n_chunks, H_M, DH_M, CHUNK), F32),
        pltpu.VMEM((n_chunks, H_M, DH_M, CHUNK), BF16),
        pltpu.VMEM((n_chunks, N_PAIR, DK_RET, DV_RET), F32),
        pltpu.VMEM((group_rows, RET_V), F32),
        pltpu.VMEM((group_rows, RET_V), F32),
        pltpu.VMEM((group_rows, 2 * M_W), F32),
        pltpu.VMEM((group_rows, GATE_LANES), F32),
        pltpu.VMEM((n_chunks, 3 * N_PAIR, CHUNK), F32),
        pltpu.VMEM((n_chunks, N_PAIR, CHUNK), F32),
        pltpu.VMEM((H_RET, 5, CHUNK, CHUNK), F32),
        pltpu.VMEM((N_DIR * H_RET, DV_RET), F32),
        pltpu.VMEM((n_chunks, 2, N_PAIR, CHUNK), F32),
        pltpu.VMEM((n_chunks, N_DIR, N_PAIR, CHUNK), F32),
    ]
    kern = functools.partial(_mixer_core_kernel, group_rows=group_rows, seq_len=seq_len, use_rope=use_rope,
                             has_init=has_init, emit_state=emit_state)
    return pl.pallas_call(
        kern,
        out_shape=out_shape,
        grid=(n_groups,),
        in_specs=in_specs,
        out_specs=out_specs,
        scratch_shapes=scratch,
        compiler_params=pltpu.CompilerParams(
            dimension_semantics=("arbitrary",), vmem_limit_bytes=VMEM_LIMIT),
        name="mixer_core",
    )(*inputs)


def _mixer_out_kernel(x_ref, rg_ref, hm_ref, mod_ref, nmix_ref, wbg_ref, wru_ref, wmu_ref, wout_ref,
                      nffn_ref, w1_ref, w3_ref, w2_ref, nfin_ref, o_ref):
    mod = mod_ref[0]
    x = x_ref[...]
    u = _norm_mod(x, nmix_ref[...], mod[3:4], mod[4:5]).astype(BF16)
    gates = _sigmoid(_dot(u, wbg_ref[...]))
    ret_branch = _dot(rg_ref[...], wru_ref[...])
    m_branch = _dot(hm_ref[...], wmu_ref[...])
    merged = (gates[:, :D_MODEL] * ret_branch + gates[:, D_MODEL:] * m_branch).astype(BF16)
    x = x + mod[5:6] * _dot(merged, wout_ref[...])
    x = _swiglu_half_step(x, mod, 6, nffn_ref[...], w1_ref, w3_ref, w2_ref)
    ms = jnp.mean(x * x, axis=-1, keepdims=True)
    o_ref[...] = x * lax.rsqrt(ms + EPS) * nfin_ref[...]


def _mixer_out_call(x, rgated, hmn, mod, mod_row, p):
    t = x.shape[0]
    tm = FFN_TILE
    consts = [p["norm_mix"], p["w_bg"], p["w_ret_up"], p["w_m_up"], p["w_out"],
              p["norm_ffn2"], p["w1_ffn2"], p["w3_ffn2"], p["w2_ffn2"], p["norm_final"]]
    return pl.pallas_call(
        _mixer_out_kernel,
        out_shape=jax.ShapeDtypeStruct((t, D_MODEL), F32),
        grid=(t // tm,),
        in_specs=[
            pl.BlockSpec((tm, D_MODEL), lambda i: (i, 0)),
            pl.BlockSpec((tm, RET_V), lambda i: (i, 0)),
            pl.BlockSpec((tm, M_W), lambda i: (i, 0)),
            pl.BlockSpec((1, N_MOD, D_MODEL), lambda i: (mod_row(i * tm), 0, 0)),
        ] + [_const_spec(a.shape) for a in consts],
        out_specs=pl.BlockSpec((tm, D_MODEL), lambda i: (i, 0)),
        compiler_params=pltpu.CompilerParams(
            dimension_semantics=("arbitrary",), vmem_limit_bytes=VMEM_LIMIT),
        name="mixer_out",
    )(x, rgated, hmn, mod, *consts)


def _rope_tables(seq_len):
    rows = seq_len // GRID_W
    r = jnp.repeat(jnp.arange(rows, dtype=F32), GRID_W)
    col = (jnp.arange(seq_len) % GRID_W).astype(F32)
    n_f = DK_RET // 4
    freqs = ROPE_BASE ** (-jnp.arange(n_f, dtype=F32) / n_f)
    ang = jnp.concatenate([r[:, None] * freqs, col[:, None] * freqs], axis=-1)
    cos, sin = jnp.cos(ang), jnp.sin(ang)
    return jnp.concatenate([cos, cos], axis=-1), jnp.concatenate([-sin, sin], axis=-1)


def _layer_params(l, norm_ffn1, w1_ffn1, w3_ffn1, w2_ffn1, norm_mix, conv_w, conv_b,
                  ret_decay_logit, b_igate, b_fgate, ret_gn, m_gn, norm_ffn2, norm_final):
    gate_bias = jnp.pad(jnp.concatenate([b_igate[l], b_fgate[l]]), (0, GATE_LANES - 2 * N_PAIR))
    p = dict(
        norm_ffn1=norm_ffn1[l][None, :], norm_mix=norm_mix[l][None, :], norm_ffn2=norm_ffn2[l][None, :],
        norm_final=norm_final[None, :],
        w1_ffn1=w1_ffn1[l].astype(BF16), w3_ffn1=w3_ffn1[l].astype(BF16), w2_ffn1=w2_ffn1[l].astype(BF16),
        conv_w=conv_w[l], conv_b=conv_b[l][None, :],
        gate_bias=gate_bias[None, :],
        decay_rows=jnp.broadcast_to(ret_decay_logit[l].reshape(N_DIR * H_RET, 1), (N_DIR * H_RET, DV_RET)),
        ret_gn=ret_gn[l][None, :], m_gn=m_gn[l][None, :],
    )
    return p


def kernel(x_prompt, x_sample, c, state_ret, state_mlstm_C, state_mlstm_n, state_mlstm_m, c_ctx, w_ada, b_ada, norm_ffn1, w1_ffn1, w3_ffn1, w2_ffn1, norm_mix, w_in, conv_w, conv_b, ret_decay_logit, b_igate, b_fgate, ret_gn, m_gn, w_ret_up, w_m_up, w_out, norm_ffn2, w1_ffn2, w3_ffn2, w2_ffn2, norm_final):
    bp, lp, _ = x_prompt.shape
    bs, ls, _ = x_sample.shape
    depth = w_ada.shape[0]
    assert depth == 1 and lp % CHUNK == 0 and PROMPT_GROUP_ROWS % lp == 0 and ls <= MAX_GROUP_ROWS
    assert lp & (lp - 1) == 0 and ls & (ls - 1) == 0

    ctx_row = 8 * ((bs + 7) // 8)
    cvec = jnp.zeros((ctx_row + 8, D_MODEL), F32).at[:bs].set(c).at[ctx_row].set(c_ctx)
    rope = _rope_tables(ls)

    xp = x_prompt.reshape(bp * lp, D_MODEL)
    xs = x_sample.reshape(bs * ls, D_MODEL)
    prompt_row = lambda r: ctx_row
    sample_row = lambda r: r // ls

    l = 0
    p = _layer_params(l, norm_ffn1, w1_ffn1, w3_ffn1, w2_ffn1, norm_mix, conv_w, conv_b,
                      ret_decay_logit, b_igate, b_fgate, ret_gn, m_gn, norm_ffn2, norm_final)
    mod = _ada_call(cvec, w_ada[l], b_ada[l][None, :]).reshape(ctx_row + 8, N_MOD, D_MODEL)

    n0 = state_mlstm_n[:, l].astype(F32).reshape(bs, N_PAIR, DH_M)
    m0 = jnp.broadcast_to(state_mlstm_m[:, l].astype(F32).reshape(bs, N_PAIR, 1), (bs, N_PAIR, CHUNK))
    init = (state_ret.astype(F32), state_mlstm_C.astype(F32), n0, m0)

    whole = lambda a: (a, False, ((0, a.shape[1]),))
    bg_start = D_IN - N_DIR * D_MODEL
    xp1, (p["w_in"], p["w_bg"]) = _ffn_call(
        xp, mod, prompt_row, p["norm_ffn1"], p["w1_ffn1"], p["w3_ffn1"], p["w2_ffn1"],
        casts=((w_in[l].T, True, ((0, D_IN), (bg_start, N_DIR * D_MODEL))),))
    xs1, (p["w1_ffn2"], p["w3_ffn2"], p["w2_ffn2"], p["w_ret_up"], p["w_m_up"], p["w_out"]) = _ffn_call(
        xs, mod, sample_row, p["norm_ffn1"], p["w1_ffn1"], p["w3_ffn1"], p["w2_ffn1"],
        casts=(whole(w1_ffn2[l]), whole(w3_ffn2[l]), whole(w2_ffn2[l]),
               whole(w_ret_up[l]), whole(w_m_up[l]), whole(w_out[l])))

    rg_p, hm_p, s_fin, c_fin, n_fin, m_fin = _mixer_core_call(xp1, mod, prompt_row, p, lp, PROMPT_GROUP_ROWS, None, None)
    rg_s, hm_s = _mixer_core_call(xs1, mod, sample_row, p, ls, ls, rope, init)

    yp = _mixer_out_call(xp1, rg_p, hm_p, mod, prompt_row, p)
    ys = _mixer_out_call(xs1, rg_s, hm_s, mod, sample_row, p)

    dt = x_prompt.dtype
    new_c = c_fin
    new_n = n_fin.reshape(bp, 1, N_DIR, H_M, DH_M)
    m_diag = jnp.stack([m_fin[:, d, d * H_M:(d + 1) * H_M, 0] for d in range(N_DIR)], axis=1)
    new_m = m_diag.reshape(bp, 1, N_DIR, H_M)
    return (yp.reshape(bp, lp, D_MODEL).astype(dt), ys.reshape(bs, ls, D_MODEL).astype(dt),
            s_fin.astype(dt), new_c.astype(dt), new_n.astype(dt), new_m.astype(dt))
```

```python
import functools

import jax
import jax.numpy as jnp
from jax import lax
from jax.experimental import pallas as pl
from jax.experimental.pallas import tpu as pltpu

F32 = jnp.float32
BF16 = jnp.bfloat16

D_MODEL = 1024
D_FF = 2816
CHUNK = 128
N_DIR = 2
H_RET = 4
DK_RET = 128
DV_RET = 256
H_M = 4
DH_M = 128
RET_QK = H_RET * DK_RET
RET_V = H_RET * DV_RET
M_W = H_M * DH_M
D_IN = 2 * RET_QK + 2 * RET_V + 4 * M_W + 2 * N_DIR * H_M + N_DIR * D_MODEL
GRID_W = 64
ROPE_BASE = 10000.0
EPS = 1e-6
GN_EPS = 1e-5
N_MOD = 9
N_PAIR = N_DIR * H_M
GATE_LANES = 128
LANES = 128
BF16_SUBLANES = 16
NORM_ROWS = BF16_SUBLANES

MAX_GROUP_ROWS = 1024
PROMPT_GROUP_ROWS = 512
ROW_BLOCK = 256
SUB_ROWS = 256
FFN_TILE = 1024
CAST_FFN_TILE = 512
MIXER_OUT_TILE = 512
VMEM_LIMIT = 60 * 1024 * 1024


def _sigmoid(x):
    return 1.0 / (1.0 + jnp.exp(-x))


def _silu(x):
    return x * _sigmoid(x)


def _log_sigmoid(x):
    return -(jnp.maximum(-x, 0.0) + jnp.log1p(jnp.exp(-jnp.abs(x))))


def _norm_mod(x, gain, shift, scale):
    ms = jnp.mean(x * x, axis=-1, keepdims=True)
    y = x * lax.rsqrt(ms + EPS) * gain
    return y * (1.0 + scale) + shift


def _dot(a, b):
    return jnp.dot(a, b, preferred_element_type=F32)


def _dot_nt(a, b):
    return lax.dot_general(a, b, (((1,), (1,)), ((), ())), preferred_element_type=F32)


def _head_norm(o, gain):
    mu = jnp.mean(o, axis=-1, keepdims=True)
    oc = o - mu
    var = jnp.mean(oc * oc, axis=-1, keepdims=True)
    return oc * lax.rsqrt(var + GN_EPS) * gain


def _ada_kernel(c_ref, w_ref, b_ref, o_ref):
    s = _silu(c_ref[...]).astype(BF16)
    o_ref[...] = _dot(s, w_ref[...].astype(BF16)) + b_ref[...]


def _ada_call(cvec, w_ada, b_ada):
    rows = cvec.shape[0]
    n_out = w_ada.shape[1]
    tile = D_MODEL
    return pl.pallas_call(
        _ada_kernel,
        out_shape=jax.ShapeDtypeStruct((rows, n_out), F32),
        grid=(n_out // tile,),
        in_specs=[
            pl.BlockSpec((rows, D_MODEL), lambda j: (0, 0)),
            pl.BlockSpec((D_MODEL, tile), lambda j: (0, j)),
            pl.BlockSpec((1, tile), lambda j: (0, j)),
        ],
        out_specs=pl.BlockSpec((rows, tile), lambda j: (0, j)),
        compiler_params=pltpu.CompilerParams(
            dimension_semantics=("arbitrary",), vmem_limit_bytes=VMEM_LIMIT),
        name="ada",
    )(cvec, w_ada, b_ada)


def _row_blocks(n_rows):
    return [slice(r, r + SUB_ROWS) for r in range(0, n_rows, SUB_ROWS)]


def _swiglu_half_steps(xs, mod, k0, gain, w1_ref, w3_ref, w2_ref):
    shift, scale, gate = mod[k0:k0 + 1], mod[k0 + 1:k0 + 2], mod[k0 + 2:k0 + 3]

    def finish(x, h1, h3):
        h = (_silu(h1) * h3).astype(BF16)
        return x + 0.5 * gate * _dot(h, w2_ref[...])

    outs, pending = [], None
    for x in xs:
        u = _norm_mod(x, gain, shift, scale).astype(BF16)
        up = (x, _dot(u, w1_ref[...]), _dot(u, w3_ref[...]))
        if pending is not None:
            outs.append(finish(*pending))
        pending = up
    outs.append(finish(*pending))
    return outs


def _ffn_kernel(*refs, cast_plan):
    n_in = 6 + len(cast_plan)
    x_ref, mod_ref, g_ref, w1_ref, w3_ref, w2_ref = refs[:6]
    o_ref = refs[n_in]
    blocks = _row_blocks(x_ref.shape[0])
    ys = _swiglu_half_steps([x_ref[rows, :] for rows in blocks], mod_ref[0], 0, g_ref[...],
                            w1_ref, w3_ref, w2_ref)
    for rows, y in zip(blocks, ys):
        o_ref[rows, :] = y
    cast_out = iter(refs[n_in + 1:])
    for src_ref, (transposed, ranges) in zip(refs[6:n_in], cast_plan):
        block = src_ref[...].T if transposed else src_ref[...]
        for start, width in ranges:
            next(cast_out)[...] = block[:, start:start + width].astype(BF16)


def _const_spec(shape):
    nd = len(shape)
    return pl.BlockSpec(shape, lambda i: (0,) * nd)


def _ffn_call(x, mod, mod_row, gain, w1, w3, w2, tm, casts=()):
    t = x.shape[0]
    steps = t // tm
    in_specs = [
        pl.BlockSpec((tm, D_MODEL), lambda i: (i, 0)),
        pl.BlockSpec((1, N_MOD, D_MODEL), lambda i: (mod_row(i * tm), 0, 0)),
        _const_spec((1, D_MODEL)),
        _const_spec((D_MODEL, D_FF)),
        _const_spec((D_MODEL, D_FF)),
        _const_spec((D_FF, D_MODEL)),
    ]
    out_shape = [jax.ShapeDtypeStruct((t, D_MODEL), F32)]
    out_specs = [pl.BlockSpec((tm, D_MODEL), lambda i: (i, 0))]
    for src, transposed, ranges in casts:
        rows, cols = src.shape[::-1] if transposed else src.shape
        if transposed:
            assert rows % (steps * LANES) == 0
            in_specs.append(pl.BlockSpec((cols, rows // steps), lambda i: (0, i)))
        else:
            assert rows % (steps * BF16_SUBLANES) == 0
            in_specs.append(pl.BlockSpec((rows // steps, cols), lambda i: (i, 0)))
        for _, width in ranges:
            out_shape.append(jax.ShapeDtypeStruct((rows, width), BF16))
            out_specs.append(pl.BlockSpec((rows // steps, width), lambda i: (i, 0)))
    outs = pl.pallas_call(
        functools.partial(_ffn_kernel, cast_plan=tuple((tr, r) for _, tr, r in casts)),
        out_shape=out_shape,
        grid=(steps,),
        in_specs=in_specs,
        out_specs=out_specs,
        compiler_params=pltpu.CompilerParams(
            dimension_semantics=("arbitrary",), vmem_limit_bytes=VMEM_LIMIT),
        name="ffn",
    )(x, mod, gain, w1, w3, w2, *[src for src, _, _ in casts])
    return outs[0], outs[1:]


def _mixer_core_kernel(*refs, group_rows, seq_len, use_rope, has_init, emit_state):
    it = iter(refs)
    x_ref, mod_ref, nmix_ref = next(it), next(it), next(it)
    wrqk_ref, wrv_ref, wrg_ref = next(it), next(it), next(it)
    wmqk_ref, wmv_ref, wmo_ref, wgt_ref = next(it), next(it), next(it), next(it)
    convw_ref, convb_ref, gbias_ref, decay_ref = next(it), next(it), next(it), next(it)
    rgn_ref, mgn_ref = next(it), next(it)
    if use_rope:
        cos_ref, sin_ref = next(it), next(it)
    if has_init:
        s0_ref, c0_ref, n0_ref, m0_ref = next(it), next(it), next(it), next(it)
    rg_out, hm_out = next(it), next(it)
    if emit_state:
        sfin_ref, cfin_ref, nfin_ref, mfin_ref = next(it), next(it), next(it), next(it)
    (u_s, q_s, k_s, v_s, t_s, qt_s, kv_s, wide_s, gate_s, pre_s, igb_s, zt_s, wt_s, dec_s, gch_s, stat_s,
     mprev_s) = (next(it) for _ in range(17))

    rows_total = group_rows
    n_chunks = rows_total // CHUNK
    cps = seq_len // CHUNK
    n_seq = rows_total // seq_len
    n_rb = rows_total // ROW_BLOCK
    mod = mod_ref[0]

    def rb_rows(rb):
        return slice(rb * ROW_BLOCK, (rb + 1) * ROW_BLOCK)

    def chunk_rows(n):
        return slice(n * CHUNK, (n + 1) * CHUNK)

    def head_cols(h, width):
        return slice(h * width, (h + 1) * width)

    @pl.when(pl.program_id(0) == 0)
    def _():
        pos_i = lax.broadcasted_iota(jnp.int32, (CHUNK, CHUNK), 0).astype(F32)
        pos_j = lax.broadcasted_iota(jnp.int32, (CHUNK, CHUNK), 1).astype(F32)
        lg_rows = _log_sigmoid(decay_ref[...])
        for h in range(H_RET):
            lgf = lg_rows[h:h + 1, :CHUNK]
            lgb = lg_rows[H_RET + h:H_RET + h + 1, :CHUNK]
            dec_s[h, 0] = jnp.where(pos_i > pos_j, jnp.exp((pos_i - pos_j) * lgf),
                                    jnp.where(pos_i < pos_j, jnp.exp((pos_j - pos_i) * lgb), 2.0))
            dec_s[h, 1] = jnp.exp((CHUNK - 1.0 - pos_j) * lgf)
            dec_s[h, 2] = jnp.exp(pos_j * lgb)
            dec_s[h, 3] = jnp.exp((pos_i + 1.0) * lgf)
            dec_s[h, 4] = jnp.exp((CHUNK - pos_i) * lgb)
        gch_s[...] = jnp.exp(CHUNK * lg_rows)

    g_chunk = gch_s[...]

    chunks_per_rb = ROW_BLOCK // CHUNK
    for rb in range(n_rb):
        rows = rb_rows(rb)
        u_s[rows, :] = _norm_mod(x_ref[rows, :], nmix_ref[...], mod[3:4], mod[4:5]).astype(BF16)
        ub = u_s[rows, :]
        qk = _dot(ub, wrqk_ref[...])
        q = qk[:, :RET_QK]
        k = qk[:, RET_QK:] * (DK_RET ** -0.5)
        for h in range(H_RET):
            cols = head_cols(h, DK_RET)
            qh, kh = q[:, cols], k[:, cols]
            if use_rope:
                cs, sn = cos_ref[rows, :], sin_ref[rows, :]
                qh = qh * cs + pltpu.roll(qh, DK_RET // 2, 1) * sn
                kh = kh * cs + pltpu.roll(kh, DK_RET // 2, 1) * sn
            q_s[rows, cols] = qh
            for ch in range(chunks_per_rb):
                t_s[rb * chunks_per_rb + ch, h] = kh[ch * CHUNK:(ch + 1) * CHUNK, :].T
        v_s[rows, :] = _dot(ub, wrv_ref[...]).astype(BF16)

    for n in range(n_chunks):
        for h in range(H_RET):
            k_t = t_s[n, h]
            kcat = jnp.concatenate([k_t * dec_s[h, 1], k_t * dec_s[h, 2]], axis=0).astype(BF16)
            kv = _dot(kcat, v_s[chunk_rows(n), head_cols(h, DV_RET)])
            kv_s[n, 2 * h] = kv[:DK_RET]
            kv_s[n, 2 * h + 1] = kv[DK_RET:]

    for rb in range(n_rb):
        gate_s[rb_rows(rb), :] = _dot(u_s[rb_rows(rb), :], wrg_ref[...])

    for g in range(n_seq):
        for h in range(H_RET):
            for d in range(N_DIR):
                slot = 2 * h + d
                decay = g_chunk[d * H_RET + h:d * H_RET + h + 1, :]
                state = s0_ref[0, 0, d, h] if has_init else jnp.zeros((DK_RET, DV_RET), F32)
                order = range(g * cps, (g + 1) * cps)
                for n in (order if d == 0 else reversed(order)):
                    local = kv_s[n, slot]
                    kv_s[n, slot] = state
                    state = decay * state + local
                if emit_state:
                    sfin_ref[g, 0, d, h] = state

    def ret_scores(n):
        return [_dot(q_s[chunk_rows(n), head_cols(h, DK_RET)].astype(BF16), t_s[n, h].astype(BF16))
                for h in range(H_RET)]

    def ret_outputs(n, all_scores):
        rows = chunk_rows(n)
        for h in range(H_RET):
            qh = q_s[rows, head_cols(h, DK_RET)]
            vh = v_s[rows, head_cols(h, DV_RET)]
            sm = (all_scores[h] * dec_s[h, 0]).astype(BF16)
            qcat = jnp.concatenate([qh * dec_s[h, 3], qh * dec_s[h, 4]], axis=1).astype(BF16)
            scat = jnp.concatenate([kv_s[n, 2 * h], kv_s[n, 2 * h + 1]], axis=0).astype(BF16)
            o = _dot(sm, vh) + _dot(qcat, scat)
            wide_s[rows, head_cols(h, DV_RET)] = _head_norm(o, rgn_ref[:, head_cols(h, DV_RET)])

    for rb in range(n_rb):
        rows = rb_rows(rb)
        chunks = range(rb * chunks_per_rb, (rb + 1) * chunks_per_rb)
        scores = [ret_scores(n) for n in chunks]
        ub = u_s[rows, :]
        glane = lax.broadcasted_iota(jnp.int32, (ROW_BLOCK, GATE_LANES), 1)
        gt = jnp.where(glane < 2 * N_PAIR, _dot(ub, wgt_ref[...]) + gbias_ref[...], 0.0)
        igb_s[rows, :] = jnp.where(glane < N_PAIR, gt, _log_sigmoid(gt))
        pre_s[rows, :] = _dot(ub, wmqk_ref[...])
        for n, sc in zip(chunks, scores):
            ret_outputs(n, sc)

    for rb in range(n_rb):
        rows = rb_rows(rb)
        rg_out[rows, :] = (wide_s[rows, :] * _silu(gate_s[rows, :])).astype(BF16)

    for rb in range(n_rb):
        mv = _dot(u_s[rb_rows(rb), :], wmv_ref[...])
        for ch in range(chunks_per_rb):
            for h in range(H_M):
                t_s[rb * chunks_per_rb + ch, h] = mv[ch * CHUNK:(ch + 1) * CHUNK, head_cols(h, DH_M)].T

    row_id = lax.broadcasted_iota(jnp.int32, (rows_total, CHUNK), 0)
    seq_pos = row_id & (seq_len - 1)
    for cb in range(2 * M_W // CHUNK):
        cols = slice(cb * CHUNK, (cb + 1) * CHUNK)
        xc = pre_s[:, cols]
        prev = jnp.where(seq_pos == 0, 0.0, pltpu.roll(xc, 1, 0))
        nxt = jnp.where(seq_pos == seq_len - 1, 0.0, pltpu.roll(xc, rows_total - 1, 0))
        y = convb_ref[:, cols] + convw_ref[0:1, cols] * prev
        y = y + convw_ref[1:2, cols] * xc
        y = y + convw_ref[2:3, cols] * nxt
        y = _silu(y)
        if cb < H_M:
            for n in range(n_chunks):
                qt_s[n, cb] = y[n * CHUNK:(n + 1) * CHUNK, :].T.astype(BF16)
        else:
            k_s[:, head_cols(cb - H_M, DH_M)] = y * (DH_M ** -0.5)

    z = igb_s[...]
    cpos = row_id & (CHUNK - 1)
    glane = lax.broadcasted_iota(jnp.int32, (rows_total, GATE_LANES), 1)
    pre, suf = z, z
    step = 1
    while step < CHUNK:
        pre = pre + jnp.where(cpos >= step, pltpu.roll(pre, step, 0), 0.0)
        suf = suf + jnp.where(cpos < CHUNK - step, pltpu.roll(suf, rows_total - step, 0), 0.0)
        step *= 2
    cum = jnp.where(glane < N_PAIR + H_M, pre, suf)
    i_minus_b = z - pltpu.roll(cum, GATE_LANES - N_PAIR, 1)
    igb = jnp.where(glane < N_PAIR, z,
                    jnp.where(glane < 2 * N_PAIR, cum, pltpu.roll(i_minus_b, 2 * N_PAIR, 1)))
    igb_s[...] = igb

    for n in range(n_chunks):
        zt_s[n] = igb[n * CHUNK:(n + 1) * CHUNK, :].T[0:3 * N_PAIR, :]
    ig_all = zt_s[:, 0:N_PAIR, :]
    b_all = zt_s[:, N_PAIR:2 * N_PAIR, :]
    pair = lax.broadcasted_iota(jnp.int32, (n_chunks, N_PAIR, 1), 1)
    b_last = jnp.where(pair < H_M, b_all[:, :, CHUNK - 1:CHUNK], b_all[:, :, 0:1])
    g_all = (b_last - b_all) + ig_all
    m_loc = jnp.max(g_all, axis=-1, keepdims=True)
    wt_s[...] = jnp.exp(g_all - m_loc)
    stat_s[:, 0] = jnp.broadcast_to(b_last, (n_chunks, N_PAIR, CHUNK))
    stat_s[:, 1] = jnp.broadcast_to(m_loc, (n_chunks, N_PAIR, CHUNK))

    for n in range(n_chunks):
        rows = chunk_rows(n)
        w_t = wt_s[n]
        for h in range(H_M):
            kh = k_s[rows, head_cols(h, DH_M)].astype(BF16)
            v_t = t_s[n, h]
            parts = []
            for d in range(N_DIR):
                w_row = w_t[d * H_M + h:d * H_M + h + 1, :]
                parts += [v_t * w_row, jnp.broadcast_to(w_row, (NORM_ROWS, CHUNK))]
            local = _dot(jnp.concatenate(parts, axis=0).astype(BF16), kh)
            for d in range(N_DIR):
                c = d * H_M + h
                base = d * (DH_M + NORM_ROWS)
                kv_s[n, c, :, :DH_M] = local[base:base + DH_M]
                kv_s[n, c, 0:NORM_ROWS, DH_M:] = local[base + DH_M:base + DH_M + NORM_ROWS]

    for rb in range(n_rb):
        gate_s[rb_rows(rb), :M_W] = _dot(u_s[rb_rows(rb), :], wmo_ref[...])

    for g in range(n_seq):
        for d in range(N_DIR):
            order = list(range(g * cps, (g + 1) * cps))
            if d == 1:
                order.reverse()
            m_prev = m0_ref[0] if has_init else jnp.zeros((N_PAIR, CHUNK), F32)
            a1s, a2s = [], []
            for n in order:
                b_last_n, m_loc_n = stat_s[n, 0], stat_s[n, 1]
                m_new = jnp.maximum(b_last_n + m_prev, m_loc_n)
                a1s.append(jnp.exp(b_last_n + m_prev - m_new))
                a2s.append(jnp.exp(m_loc_n - m_new))
                mprev_s[n, d] = m_prev
                m_prev = m_new
            if emit_state:
                mfin_ref[g, d] = m_prev
            for h in range(H_M):
                c = d * H_M + h
                if has_init:
                    mem = c0_ref[0, 0, d, h]
                    nrm = jnp.broadcast_to(n0_ref[0][c:c + 1, :], (NORM_ROWS, DH_M))
                else:
                    mem = jnp.zeros((DH_M, DH_M), F32)
                    nrm = jnp.zeros((NORM_ROWS, DH_M), F32)
                for idx, n in enumerate(order):
                    a1 = a1s[idx][c:c + 1, :]
                    a2 = a2s[idx][c:c + 1, :]
                    local_mem = kv_s[n, c, :, :DH_M]
                    local_nrm = kv_s[n, c, 0:NORM_ROWS, DH_M:]
                    kv_s[n, c, :, :DH_M] = mem
                    kv_s[n, c, 0:NORM_ROWS, DH_M:] = nrm
                    mem = a1 * mem + a2 * local_mem
                    nrm = a1 * nrm + a2 * local_nrm
                if emit_state:
                    cfin_ref[g, 0, d, h] = mem
                    nfin_ref[g, c:c + 1, :] = nrm[0:1]

    tri_j = lax.broadcasted_iota(jnp.int32, (CHUNK, CHUNK), 0)
    tri_i = lax.broadcasted_iota(jnp.int32, (CHUNK, CHUNK), 1)
    mem_rows = DH_M + NORM_ROWS

    for n in range(n_chunks):
        rows = chunk_rows(n)
        zc = igb_s[rows, :]
        zt = zt_s[n]
        bigs = []
        for h in range(H_M):
            kh = k_s[rows, head_cols(h, DH_M)].astype(BF16)
            mems = [jnp.concatenate([kv_s[n, d * H_M + h, :, :DH_M], kv_s[n, d * H_M + h, 0:NORM_ROWS, DH_M:]],
                                    axis=0).astype(BF16) for d in range(N_DIR)]
            bigs.append(_dot(jnp.concatenate([kh] + mems, axis=0), qt_s[n, h]))
        for h in range(H_M):
            big = bigs[h]
            scores_t = big[:CHUNK]
            weights, crosses, w_ints, m_is = [], [], [], []
            for d in range(N_DIR):
                c = d * H_M + h
                a_col = zc[:, 2 * N_PAIR + c:2 * N_PAIR + c + 1]
                b_row = zt[N_PAIR + c:N_PAIR + c + 1, :]
                mask = (tri_j <= tri_i) if d == 0 else (tri_j >= tri_i)
                dlog = jnp.where(mask, b_row + a_col, -jnp.inf)
                lin = b_row + mprev_s[n, d][c:c + 1, :]
                m_i = jnp.maximum(jnp.max(dlog, axis=0, keepdims=True), lin)
                weights.append(scores_t * jnp.exp(dlog - m_i))
                crosses.append(big[CHUNK + d * mem_rows:CHUNK + (d + 1) * mem_rows])
                w_ints.append(jnp.exp(lin - m_i))
                m_is.append(m_i)
            intra = _dot(t_s[n, h].astype(BF16), jnp.concatenate(weights, axis=1).astype(BF16))
            h_sum_t = jnp.zeros((DH_M, CHUNK), F32)
            for d in range(N_DIR):
                cross, w_int = crosses[d], w_ints[d]
                num = intra[:, d * CHUNK:(d + 1) * CHUNK] + cross[:DH_M] * w_int
                den = jnp.sum(weights[d], axis=0, keepdims=True) + cross[DH_M:DH_M + 1] * w_int
                denom = jnp.maximum(jnp.abs(den), jnp.exp(-m_is[d]))
                h_sum_t = h_sum_t + num * (1.0 / denom)
            wide_s[rows, head_cols(h, DH_M)] = h_sum_t.T

    for rb in range(n_rb):
        rows = rb_rows(rb)
        hm = _sigmoid(gate_s[rows, :M_W]) * wide_s[rows, :M_W]
        for h in range(H_M):
            cols = head_cols(h, DH_M)
            hm_out[rows, cols] = _head_norm(hm[:, cols], mgn_ref[:, cols]).astype(BF16)


def _mixer_core_call(x, mod, mod_row, p, seq_len, group_rows, rope, init):
    t = x.shape[0]
    n_groups = t // group_rows
    n_chunks = group_rows // CHUNK
    seq_per_group = group_rows // seq_len
    use_rope = rope is not None
    has_init = init is not None
    emit_state = not has_init

    w_in = p["w_in"]
    groups = []
    start = 0
    for width in (2 * RET_QK, RET_V, RET_V, 2 * M_W, M_W, M_W, GATE_LANES):
        assert start % width == 0
        groups.append(pl.BlockSpec((D_MODEL, width), functools.partial(lambda i, c: (0, c), c=start // width)))
        start += width
    small = [p["conv_w"], p["conv_b"], p["gate_bias"], p["decay_rows"], p["ret_gn"], p["m_gn"]]
    inputs = [x, mod, p["norm_mix"]] + [w_in] * len(groups) + small
    in_specs = [
        pl.BlockSpec((group_rows, D_MODEL), lambda i: (i, 0)),
        pl.BlockSpec((1, N_MOD, D_MODEL), lambda i: (mod_row(i * group_rows), 0, 0)),
        _const_spec(p["norm_mix"].shape),
    ] + groups + [_const_spec(a.shape) for a in small]
    if use_rope:
        inputs += [rope[0], rope[1]]
        in_specs += [_const_spec(rope[0].shape), _const_spec(rope[1].shape)]
    if has_init:
        s0, c0, n0, m0 = init
        inputs += [s0, c0, n0, m0]
        in_specs += [
            pl.BlockSpec((1, 1, N_DIR, H_RET, DK_RET, DV_RET), lambda i: (i, 0, 0, 0, 0, 0)),
            pl.BlockSpec((1, 1, N_DIR, H_M, DH_M, DH_M), lambda i: (i, 0, 0, 0, 0, 0)),
            pl.BlockSpec((1, N_PAIR, DH_M), lambda i: (i, 0, 0)),
            pl.BlockSpec((1, N_PAIR, CHUNK), lambda i: (i, 0, 0)),
        ]

    out_shape = [jax.ShapeDtypeStruct((t, RET_V), BF16), jax.ShapeDtypeStruct((t, M_W), BF16)]
    out_specs = [pl.BlockSpec((group_rows, RET_V), lambda i: (i, 0)),
                 pl.BlockSpec((group_rows, M_W), lambda i: (i, 0))]
    if emit_state:
        n_seq = t // seq_len
        out_shape += [
            jax.ShapeDtypeStruct((n_seq, 1, N_DIR, H_RET, DK_RET, DV_RET), F32),
            jax.ShapeDtypeStruct((n_seq, 1, N_DIR, H_M, DH_M, DH_M), F32),
            jax.ShapeDtypeStruct((n_seq, N_PAIR, DH_M), F32),
            jax.ShapeDtypeStruct((n_seq, N_DIR, N_PAIR, CHUNK), F32),
        ]
        out_specs += [
            pl.BlockSpec((seq_per_group, 1, N_DIR, H_RET, DK_RET, DV_RET), lambda i: (i, 0, 0, 0, 0, 0)),
            pl.BlockSpec((seq_per_group, 1, N_DIR, H_M, DH_M, DH_M), lambda i: (i, 0, 0, 0, 0, 0)),
            pl.BlockSpec((seq_per_group, N_PAIR, DH_M), lambda i: (i, 0, 0)),
            pl.BlockSpec((seq_per_group, N_DIR, N_PAIR, CHUNK), lambda i: (i, 0, 0, 0)),
        ]

    scratch = [
        pltpu.VMEM((group_rows, D_MODEL), BF16),
        pltpu.VMEM((group_rows, RET_QK), F32),
        pltpu.VMEM((group_rows, RET_QK), F32),
        pltpu.VMEM((group_rows, RET_V), BF16),
        pltpu.VMEM((n_chunks, H_M, DH_M, CHUNK), F32),
        pltpu.VMEM((n_chunks, H_M, DH_M, CHUNK), BF16),
        pltpu.VMEM((n_chunks, N_PAIR, DK_RET, DV_RET), F32),
        pltpu.VMEM((group_rows, RET_V), F32),
        pltpu.VMEM((group_rows, RET_V), F32),
        pltpu.VMEM((group_rows, 2 * M_W), F32),
        pltpu.VMEM((group_rows, GATE_LANES), F32),
        pltpu.VMEM((n_chunks, 3 * N_PAIR, CHUNK), F32),
        pltpu.VMEM((n_chunks, N_PAIR, CHUNK), F32),
        pltpu.VMEM((H_RET, 5, CHUNK, CHUNK), F32),
        pltpu.VMEM((N_DIR * H_RET, DV_RET), F32),
        pltpu.VMEM((n_chunks, 2, N_PAIR, CHUNK), F32),
        pltpu.VMEM((n_chunks, N_DIR, N_PAIR, CHUNK), F32),
    ]
    kern = functools.partial(_mixer_core_kernel, group_rows=group_rows, seq_len=seq_len, use_rope=use_rope,
                             has_init=has_init, emit_state=emit_state)
    return pl.pallas_call(
        kern,
        out_shape=out_shape,
        grid=(n_groups,),
        in_specs=in_specs,
        out_specs=out_specs,
        scratch_shapes=scratch,
        compiler_params=pltpu.CompilerParams(
            dimension_semantics=("arbitrary",), vmem_limit_bytes=VMEM_LIMIT),
        name="mixer_core",
    )(*inputs)


def _mixer_out_kernel(x_ref, rg_ref, hm_ref, mod_ref, nmix_ref, wbg_ref, wru_ref, wmu_ref, wout_ref,
                      nffn_ref, w1_ref, w3_ref, w2_ref, nfin_ref, o_ref):
    mod = mod_ref[0]
    blocks = _row_blocks(x_ref.shape[0])

    def merge(x, bg, ret_branch, m_branch):
        gates = _sigmoid(bg)
        merged = (gates[:, :D_MODEL] * ret_branch + gates[:, D_MODEL:] * m_branch).astype(BF16)
        return x + mod[5:6] * _dot(merged, wout_ref[...])

    xs, pending = [], None
    for rows in blocks:
        x = x_ref[rows, :]
        u = _norm_mod(x, nmix_ref[...], mod[3:4], mod[4:5]).astype(BF16)
        branches = (x, _dot(u, wbg_ref[...]), _dot(rg_ref[rows, :], wru_ref[...]),
                    _dot(hm_ref[rows, :], wmu_ref[...]))
        if pending is not None:
            xs.append(merge(*pending))
        pending = branches
    xs.append(merge(*pending))

    xs = _swiglu_half_steps(xs, mod, 6, nffn_ref[...], w1_ref, w3_ref, w2_ref)
    for rows, x in zip(blocks, xs):
        ms = jnp.mean(x * x, axis=-1, keepdims=True)
        o_ref[rows, :] = x * lax.rsqrt(ms + EPS) * nfin_ref[...]


def _mixer_out_call(x, rgated, hmn, mod, mod_row, p):
    t = x.shape[0]
    tm = MIXER_OUT_TILE
    consts = [p["norm_mix"], p["w_bg"], p["w_ret_up"], p["w_m_up"], p["w_out"],
              p["norm_ffn2"], p["w1_ffn2"], p["w3_ffn2"], p["w2_ffn2"], p["norm_final"]]
    return pl.pallas_call(
        _mixer_out_kernel,
        out_shape=jax.ShapeDtypeStruct((t, D_MODEL), F32),
        grid=(t // tm,),
        in_specs=[
            pl.BlockSpec((tm, D_MODEL), lambda i: (i, 0)),
            pl.BlockSpec((tm, RET_V), lambda i: (i, 0)),
            pl.BlockSpec((tm, M_W), lambda i: (i, 0)),
            pl.BlockSpec((1, N_MOD, D_MODEL), lambda i: (mod_row(i * tm), 0, 0)),
        ] + [_const_spec(a.shape) for a in consts],
        out_specs=pl.BlockSpec((tm, D_MODEL), lambda i: (i, 0)),
        compiler_params=pltpu.CompilerParams(
            dimension_semantics=("arbitrary",), vmem_limit_bytes=VMEM_LIMIT),
        name="mixer_out",
    )(x, rgated, hmn, mod, *consts)


def _rope_tables(seq_len):
    rows = seq_len // GRID_W
    r = jnp.repeat(jnp.arange(rows, dtype=F32), GRID_W)
    col = (jnp.arange(seq_len) % GRID_W).astype(F32)
    n_f = DK_RET // 4
    freqs = ROPE_BASE ** (-jnp.arange(n_f, dtype=F32) / n_f)
    ang = jnp.concatenate([r[:, None] * freqs, col[:, None] * freqs], axis=-1)
    cos, sin = jnp.cos(ang), jnp.sin(ang)
    return jnp.concatenate([cos, cos], axis=-1), jnp.concatenate([-sin, sin], axis=-1)


def _layer_params(l, norm_ffn1, w1_ffn1, w3_ffn1, w2_ffn1, norm_mix, conv_w, conv_b,
                  ret_decay_logit, b_igate, b_fgate, ret_gn, m_gn, norm_ffn2, norm_final):
    gate_bias = jnp.pad(jnp.concatenate([b_igate[l], b_fgate[l]]), (0, GATE_LANES - 2 * N_PAIR))
    p = dict(
        norm_ffn1=norm_ffn1[l][None, :], norm_mix=norm_mix[l][None, :], norm_ffn2=norm_ffn2[l][None, :],
        norm_final=norm_final[None, :],
        w1_ffn1=w1_ffn1[l].astype(BF16), w3_ffn1=w3_ffn1[l].astype(BF16), w2_ffn1=w2_ffn1[l].astype(BF16),
        conv_w=conv_w[l], conv_b=conv_b[l][None, :],
        gate_bias=gate_bias[None, :],
        decay_rows=jnp.broadcast_to(ret_decay_logit[l].reshape(N_DIR * H_RET, 1), (N_DIR * H_RET, DV_RET)),
        ret_gn=ret_gn[l][None, :], m_gn=m_gn[l][None, :],
    )
    return p


def kernel(x_prompt, x_sample, c, state_ret, state_mlstm_C, state_mlstm_n, state_mlstm_m, c_ctx, w_ada, b_ada, norm_ffn1, w1_ffn1, w3_ffn1, w2_ffn1, norm_mix, w_in, conv_w, conv_b, ret_decay_logit, b_igate, b_fgate, ret_gn, m_gn, w_ret_up, w_m_up, w_out, norm_ffn2, w1_ffn2, w3_ffn2, w2_ffn2, norm_final):
    bp, lp, _ = x_prompt.shape
    bs, ls, _ = x_sample.shape
    depth = w_ada.shape[0]
    assert depth == 1 and lp % CHUNK == 0 and PROMPT_GROUP_ROWS % lp == 0 and ls <= MAX_GROUP_ROWS
    assert lp & (lp - 1) == 0 and ls & (ls - 1) == 0

    ctx_row = 8 * ((bs + 7) // 8)
    cvec = jnp.zeros((ctx_row + 8, D_MODEL), F32).at[:bs].set(c).at[ctx_row].set(c_ctx)
    rope = _rope_tables(ls)

    xp = x_prompt.reshape(bp * lp, D_MODEL)
    xs = x_sample.reshape(bs * ls, D_MODEL)
    prompt_row = lambda r: ctx_row
    sample_row = lambda r: r // ls

    l = 0
    p = _layer_params(l, norm_ffn1, w1_ffn1, w3_ffn1, w2_ffn1, norm_mix, conv_w, conv_b,
                      ret_decay_logit, b_igate, b_fgate, ret_gn, m_gn, norm_ffn2, norm_final)
    mod = _ada_call(cvec, w_ada[l], b_ada[l][None, :]).reshape(ctx_row + 8, N_MOD, D_MODEL)

    n0 = state_mlstm_n[:, l].astype(F32).reshape(bs, N_PAIR, DH_M)
    m0 = jnp.broadcast_to(state_mlstm_m[:, l].astype(F32).reshape(bs, N_PAIR, 1), (bs, N_PAIR, CHUNK))
    init = (state_ret.astype(F32), state_mlstm_C.astype(F32), n0, m0)

    whole = lambda a: (a, False, ((0, a.shape[1]),))
    bg_start = D_IN - N_DIR * D_MODEL
    xp1, (p["w_in"], p["w_bg"]) = _ffn_call(
        xp, mod, prompt_row, p["norm_ffn1"], p["w1_ffn1"], p["w3_ffn1"], p["w2_ffn1"], CAST_FFN_TILE,
        casts=((w_in[l].T, True, ((0, D_IN), (bg_start, N_DIR * D_MODEL))),))
    xs1, (p["w1_ffn2"], p["w3_ffn2"], p["w2_ffn2"], p["w_ret_up"], p["w_m_up"], p["w_out"]) = _ffn_call(
        xs, mod, sample_row, p["norm_ffn1"], p["w1_ffn1"], p["w3_ffn1"], p["w2_ffn1"], FFN_TILE,
        casts=(whole(w1_ffn2[l]), whole(w3_ffn2[l]), whole(w2_ffn2[l]),
               whole(w_ret_up[l]), whole(w_m_up[l]), whole(w_out[l])))

    rg_p, hm_p, s_fin, c_fin, n_fin, m_fin = _mixer_core_call(xp1, mod, prompt_row, p, lp, PROMPT_GROUP_ROWS, None, None)
    rg_s, hm_s = _mixer_core_call(xs1, mod, sample_row, p, ls, ls, rope, init)

    yp = _mixer_out_call(xp1, rg_p, hm_p, mod, prompt_row, p)
    ys = _mixer_out_call(xs1, rg_s, hm_s, mod, sample_row, p)

    dt = x_prompt.dtype
    new_c = c_fin
    new_n = n_fin.reshape(bp, 1, N_DIR, H_M, DH_M)
    m_diag = jnp.stack([m_fin[:, d, d * H_M:(d + 1) * H_M, 0] for d in range(N_DIR)], axis=1)
    new_m = m_diag.reshape(bp, 1, N_DIR, H_M)
    return (yp.reshape(bp, lp, D_MODEL).astype(dt), ys.reshape(bs, ls, D_MODEL).astype(dt),
            s_fin.astype(dt), new_c.astype(dt), new_n.astype(dt), new_m.astype(dt))
```

```python
import functools

import jax
import jax.numpy as jnp
from jax import lax
from jax.experimental import pallas as pl
from jax.experimental.pallas import tpu as pltpu

F32 = jnp.float32
BF16 = jnp.bfloat16

D_MODEL = 1024
D_FF = 2816
CHUNK = 128
N_DIR = 2
H_RET = 4
DK_RET = 128
DV_RET = 256
H_M = 4
DH_M = 128
RET_QK = H_RET * DK_RET
RET_V = H_RET * DV_RET
M_W = H_M * DH_M
D_IN = 2 * RET_QK + 2 * RET_V + 4 * M_W + 2 * N_DIR * H_M + N_DIR * D_MODEL
GRID_W = 64
ROPE_BASE = 10000.0
EPS = 1e-6
GN_EPS = 1e-5
N_MOD = 9
N_PAIR = N_DIR * H_M
GATE_LANES = 128
LANES = 128
F32_SUBLANES = 8
BF16_SUBLANES = 16
NORM_ROWS = BF16_SUBLANES

MAX_GROUP_ROWS = 1024
PROMPT_GROUP_ROWS = 512
ROW_BLOCK = 256
ADA_TILE = 3 * D_MODEL
SUB_ROWS = 256
FFN_TILE = 512
V7X_VMEM_BYTES = 64 * 1024 * 1024
VMEM_LIMIT = V7X_VMEM_BYTES - 4 * 1024 * 1024


def _sigmoid(x):
    return 1.0 / (1.0 + jnp.exp(-x))


def _silu(x):
    return x * _sigmoid(x)


def _log_sigmoid(x):
    return -(jnp.maximum(-x, 0.0) + jnp.log1p(jnp.exp(-jnp.abs(x))))


def _norm_mod(x, gain, shift, scale):
    ms = jnp.mean(x * x, axis=-1, keepdims=True)
    y = x * lax.rsqrt(ms + EPS) * gain
    return y * (1.0 + scale) + shift


def _dot(a, b):
    return jnp.dot(a, b, preferred_element_type=F32)


def _dot_nt(a, b):
    return lax.dot_general(a, b, (((1,), (1,)), ((), ())), preferred_element_type=F32)


def _head_norm(o, gain):
    mu = jnp.mean(o, axis=-1, keepdims=True)
    oc = o - mu
    var = jnp.mean(oc * oc, axis=-1, keepdims=True)
    return oc * lax.rsqrt(var + GN_EPS) * gain


def _ada_kernel(c_ref, w_ref, b_ref, o_ref):
    s = _silu(c_ref[...]).astype(BF16)
    o_ref[...] = _dot(s, w_ref[...].astype(BF16)) + b_ref[...]


def _ada_call(cvec, w_ada, b_ada):
    rows = cvec.shape[0]
    n_out = w_ada.shape[1]
    tile = ADA_TILE
    return pl.pallas_call(
        _ada_kernel,
        out_shape=jax.ShapeDtypeStruct((rows, n_out), F32),
        grid=(n_out // tile,),
        in_specs=[
            pl.BlockSpec((rows, D_MODEL), lambda j: (0, 0)),
            pl.BlockSpec((D_MODEL, tile), lambda j: (0, j)),
            pl.BlockSpec((1, tile), lambda j: (0, j)),
        ],
        out_specs=pl.BlockSpec((rows, tile), lambda j: (0, j)),
        compiler_params=pltpu.CompilerParams(
            dimension_semantics=("arbitrary",), vmem_limit_bytes=VMEM_LIMIT),
        name="ada",
    )(cvec, w_ada, b_ada)


def _row_blocks(n_rows):
    return [slice(r, r + SUB_ROWS) for r in range(0, n_rows, SUB_ROWS)]


def _swiglu_half_steps(xs, mod, k0, gain, w1_ref, w3_ref, w2_ref):
    shift, scale, gate = mod[k0:k0 + 1], mod[k0 + 1:k0 + 2], mod[k0 + 2:k0 + 3]

    def finish(x, h1, h3):
        h = (_silu(h1) * h3).astype(BF16)
        return x + 0.5 * gate * _dot(h, w2_ref[...])

    outs, pending = [], None
    for x in xs:
        u = _norm_mod(x, gain, shift, scale).astype(BF16)
        up = (x, _dot(u, w1_ref[...]), _dot(u, w3_ref[...]))
        if pending is not None:
            outs.append(finish(*pending))
        pending = up
    outs.append(finish(*pending))
    return outs


def _ffn_kernel(*refs, cast_plan):
    n_in = 6 + len(cast_plan)
    x_ref, mod_ref, g_ref, w1_ref, w3_ref, w2_ref = refs[:6]
    o_ref = refs[n_in]
    blocks = _row_blocks(x_ref.shape[0])
    ys = _swiglu_half_steps([x_ref[rows, :] for rows in blocks], mod_ref[0], 0, g_ref[...],
                            w1_ref, w3_ref, w2_ref)
    for rows, y in zip(blocks, ys):
        o_ref[rows, :] = y
    cast_out = iter(refs[n_in + 1:])
    for src_ref, (transposed, ranges) in zip(refs[6:n_in], cast_plan):
        block = src_ref[...].T if transposed else src_ref[...]
        for start, width in ranges:
            next(cast_out)[...] = block[:, start:start + width].astype(BF16)


def _const_spec(shape):
    nd = len(shape)
    return pl.BlockSpec(shape, lambda i: (0,) * nd)


def _ffn_call(x, mod, mod_row, gain, w1, w3, w2, casts=()):
    t = x.shape[0]
    tm = FFN_TILE
    steps = t // tm
    in_specs = [
        pl.BlockSpec((tm, D_MODEL), lambda i: (i, 0)),
        pl.BlockSpec((1, N_MOD, D_MODEL), lambda i: (mod_row(i * tm), 0, 0)),
        _const_spec((1, D_MODEL)),
        _const_spec((D_MODEL, D_FF)),
        _const_spec((D_MODEL, D_FF)),
        _const_spec((D_FF, D_MODEL)),
    ]
    out_shape = [jax.ShapeDtypeStruct((t, D_MODEL), F32)]
    out_specs = [pl.BlockSpec((tm, D_MODEL), lambda i: (i, 0))]
    for src, transposed, ranges in casts:
        rows, cols = src.shape[::-1] if transposed else src.shape
        if transposed:
            assert rows % (steps * LANES) == 0
            in_specs.append(pl.BlockSpec((cols, rows // steps), lambda i: (0, i)))
        else:
            assert rows % (steps * BF16_SUBLANES) == 0
            in_specs.append(pl.BlockSpec((rows // steps, cols), lambda i: (i, 0)))
        for _, width in ranges:
            out_shape.append(jax.ShapeDtypeStruct((rows, width), BF16))
            out_specs.append(pl.BlockSpec((rows // steps, width), lambda i: (i, 0)))
    outs = pl.pallas_call(
        functools.partial(_ffn_kernel, cast_plan=tuple((tr, r) for _, tr, r in casts)),
        out_shape=out_shape,
        grid=(steps,),
        in_specs=in_specs,
        out_specs=out_specs,
        compiler_params=pltpu.CompilerParams(
            dimension_semantics=("arbitrary",), vmem_limit_bytes=VMEM_LIMIT),
        name="ffn",
    )(x, mod, gain, w1, w3, w2, *[src for src, _, _ in casts])
    return outs[0], outs[1:]


def _mixer_core_kernel(*refs, group_rows, seq_len, use_rope, has_init, emit_state):
    it = iter(refs)
    x_ref, mod_ref, nmix_ref = next(it), next(it), next(it)
    wrqk_ref, wrv_ref, wrg_ref = next(it), next(it), next(it)
    wmqk_ref, wmv_ref, wmo_ref, wgt_ref = next(it), next(it), next(it), next(it)
    convw_ref, convb_ref, gbias_ref, decay_ref = next(it), next(it), next(it), next(it)
    rgn_ref, mgn_ref = next(it), next(it)
    if use_rope:
        cos_ref, sin_ref = next(it), next(it)
    if has_init:
        s0_ref, c0_ref, n0_ref, m0_ref = next(it), next(it), next(it), next(it)
    rg_out, hm_out = next(it), next(it)
    if emit_state:
        sfin_ref, cfin_ref, nfin_ref, mfin_ref = next(it), next(it), next(it), next(it)
    (u_s, q_s, k_s, v_s, t_s, qt_s, kv_s, wide_s, gate_s, pre_s, igb_s, zt_s, wt_s, dec_s, gch_s, stat_s,
     mprev_s) = (next(it) for _ in range(17))

    rows_total = group_rows
    n_chunks = rows_total // CHUNK
    cps = seq_len // CHUNK
    n_seq = rows_total // seq_len
    n_rb = rows_total // ROW_BLOCK
    mod = mod_ref[0]

    def rb_rows(rb):
        return slice(rb * ROW_BLOCK, (rb + 1) * ROW_BLOCK)

    def chunk_rows(n):
        return slice(n * CHUNK, (n + 1) * CHUNK)

    def head_cols(h, width):
        return slice(h * width, (h + 1) * width)

    @pl.when(pl.program_id(0) == 0)
    def _():
        pos_i = lax.broadcasted_iota(jnp.int32, (CHUNK, CHUNK), 0).astype(F32)
        pos_j = lax.broadcasted_iota(jnp.int32, (CHUNK, CHUNK), 1).astype(F32)
        lg_rows = _log_sigmoid(decay_ref[...])
        for h in range(H_RET):
            lgf = lg_rows[h:h + 1, :CHUNK]
            lgb = lg_rows[H_RET + h:H_RET + h + 1, :CHUNK]
            dec_s[h, 0] = jnp.where(pos_i > pos_j, jnp.exp((pos_i - pos_j) * lgf),
                                    jnp.where(pos_i < pos_j, jnp.exp((pos_j - pos_i) * lgb), 2.0))
            dec_s[h, 1] = jnp.exp((CHUNK - 1.0 - pos_j) * lgf)
            dec_s[h, 2] = jnp.exp(pos_j * lgb)
            dec_s[h, 3] = jnp.exp((pos_i + 1.0) * lgf)
            dec_s[h, 4] = jnp.exp((CHUNK - pos_i) * lgb)
        gch_s[...] = jnp.exp(CHUNK * lg_rows)

    g_chunk = gch_s[...]

    chunks_per_rb = ROW_BLOCK // CHUNK
    for rb in range(n_rb):
        rows = rb_rows(rb)
        u_s[rows, :] = _norm_mod(x_ref[rows, :], nmix_ref[...], mod[3:4], mod[4:5]).astype(BF16)
        ub = u_s[rows, :]
        qk = _dot(ub, wrqk_ref[...])
        q = qk[:, :RET_QK]
        k = qk[:, RET_QK:] * (DK_RET ** -0.5)
        for h in range(H_RET):
            cols = head_cols(h, DK_RET)
            qh, kh = q[:, cols], k[:, cols]
            if use_rope:
                cs, sn = cos_ref[rows, :], sin_ref[rows, :]
                qh = qh * cs + pltpu.roll(qh, DK_RET // 2, 1) * sn
                kh = kh * cs + pltpu.roll(kh, DK_RET // 2, 1) * sn
            q_s[rows, cols] = qh
            for ch in range(chunks_per_rb):
                t_s[rb * chunks_per_rb + ch, h] = kh[ch * CHUNK:(ch + 1) * CHUNK, :].T
        v_s[rows, :] = _dot(ub, wrv_ref[...]).astype(BF16)

    for n in range(n_chunks):
        for h in range(H_RET):
            k_t = t_s[n, h]
            kcat = jnp.concatenate([k_t * dec_s[h, 1], k_t * dec_s[h, 2]], axis=0).astype(BF16)
            kv = _dot(kcat, v_s[chunk_rows(n), head_cols(h, DV_RET)])
            kv_s[n, 2 * h] = kv[:DK_RET]
            kv_s[n, 2 * h + 1] = kv[DK_RET:]

    for rb in range(n_rb):
        gate_s[rb_rows(rb), :] = _dot(u_s[rb_rows(rb), :], wrg_ref[...])

    for g in range(n_seq):
        for h in range(H_RET):
            for d in range(N_DIR):
                slot = 2 * h + d
                decay = g_chunk[d * H_RET + h:d * H_RET + h + 1, :]
                state = s0_ref[0, 0, d, h] if has_init else jnp.zeros((DK_RET, DV_RET), F32)
                order = range(g * cps, (g + 1) * cps)
                for n in (order if d == 0 else reversed(order)):
                    local = kv_s[n, slot]
                    kv_s[n, slot] = state
                    state = decay * state + local
                if emit_state:
                    sfin_ref[g, 0, d, h] = state

    def ret_scores(n):
        return [_dot(q_s[chunk_rows(n), head_cols(h, DK_RET)].astype(BF16), t_s[n, h].astype(BF16))
                for h in range(H_RET)]

    def ret_outputs(n, all_scores):
        rows = chunk_rows(n)
        for h in range(H_RET):
            qh = q_s[rows, head_cols(h, DK_RET)]
            vh = v_s[rows, head_cols(h, DV_RET)]
            sm = (all_scores[h] * dec_s[h, 0]).astype(BF16)
            qcat = jnp.concatenate([qh * dec_s[h, 3], qh * dec_s[h, 4]], axis=1).astype(BF16)
            scat = jnp.concatenate([kv_s[n, 2 * h], kv_s[n, 2 * h + 1]], axis=0).astype(BF16)
            o = _dot(sm, vh) + _dot(qcat, scat)
            wide_s[rows, head_cols(h, DV_RET)] = _head_norm(o, rgn_ref[:, head_cols(h, DV_RET)])

    for rb in range(n_rb):
        rows = rb_rows(rb)
        chunks = range(rb * chunks_per_rb, (rb + 1) * chunks_per_rb)
        scores = [ret_scores(n) for n in chunks]
        ub = u_s[rows, :]
        glane = lax.broadcasted_iota(jnp.int32, (ROW_BLOCK, GATE_LANES), 1)
        gt = jnp.where(glane < 2 * N_PAIR, _dot(ub, wgt_ref[...]) + gbias_ref[...], 0.0)
        igb_s[rows, :] = jnp.where(glane < N_PAIR, gt, _log_sigmoid(gt))
        pre_s[rows, :] = _dot(ub, wmqk_ref[...])
        for n, sc in zip(chunks, scores):
            ret_outputs(n, sc)

    for rb in range(n_rb):
        rows = rb_rows(rb)
        rg_out[rows, :] = (wide_s[rows, :] * _silu(gate_s[rows, :])).astype(BF16)

    for rb in range(n_rb):
        mv = _dot(u_s[rb_rows(rb), :], wmv_ref[...])
        for ch in range(chunks_per_rb):
            for h in range(H_M):
                t_s[rb * chunks_per_rb + ch, h] = mv[ch * CHUNK:(ch + 1) * CHUNK, head_cols(h, DH_M)].T

    row_id = lax.broadcasted_iota(jnp.int32, (rows_total, CHUNK), 0)
    seq_pos = row_id & (seq_len - 1)
    for cb in range(2 * M_W // CHUNK):
        cols = slice(cb * CHUNK, (cb + 1) * CHUNK)
        xc = pre_s[:, cols]
        prev = jnp.where(seq_pos == 0, 0.0, pltpu.roll(xc, 1, 0))
        nxt = jnp.where(seq_pos == seq_len - 1, 0.0, pltpu.roll(xc, rows_total - 1, 0))
        y = convb_ref[:, cols] + convw_ref[0:1, cols] * prev
        y = y + convw_ref[1:2, cols] * xc
        y = y + convw_ref[2:3, cols] * nxt
        y = _silu(y)
        if cb < H_M:
            for n in range(n_chunks):
                qt_s[n, cb] = y[n * CHUNK:(n + 1) * CHUNK, :].T.astype(BF16)
        else:
            k_s[:, head_cols(cb - H_M, DH_M)] = y * (DH_M ** -0.5)

    z = igb_s[...]
    cpos = row_id & (CHUNK - 1)
    glane = lax.broadcasted_iota(jnp.int32, (rows_total, GATE_LANES), 1)
    pre, suf = z, z
    step = 1
    while step < CHUNK:
        pre = pre + jnp.where(cpos >= step, pltpu.roll(pre, step, 0), 0.0)
        suf = suf + jnp.where(cpos < CHUNK - step, pltpu.roll(suf, rows_total - step, 0), 0.0)
        step *= 2
    cum = jnp.where(glane < N_PAIR + H_M, pre, suf)
    i_minus_b = z - pltpu.roll(cum, GATE_LANES - N_PAIR, 1)
    igb = jnp.where(glane < N_PAIR, z,
                    jnp.where(glane < 2 * N_PAIR, cum, pltpu.roll(i_minus_b, 2 * N_PAIR, 1)))
    igb_s[...] = igb

    for n in range(n_chunks):
        zt_s[n] = igb[n * CHUNK:(n + 1) * CHUNK, :].T[0:3 * N_PAIR, :]
    ig_all = zt_s[:, 0:N_PAIR, :]
    b_all = zt_s[:, N_PAIR:2 * N_PAIR, :]
    pair = lax.broadcasted_iota(jnp.int32, (n_chunks, N_PAIR, 1), 1)
    b_last = jnp.where(pair < H_M, b_all[:, :, CHUNK - 1:CHUNK], b_all[:, :, 0:1])
    g_all = (b_last - b_all) + ig_all
    m_loc = jnp.max(g_all, axis=-1, keepdims=True)
    wt_s[...] = jnp.exp(g_all - m_loc)
    stat_s[:, 0] = jnp.broadcast_to(b_last, (n_chunks, N_PAIR, CHUNK))
    stat_s[:, 1] = jnp.broadcast_to(m_loc, (n_chunks, N_PAIR, CHUNK))

    for n in range(n_chunks):
        rows = chunk_rows(n)
        w_t = wt_s[n]
        for h in range(H_M):
            kh = k_s[rows, head_cols(h, DH_M)].astype(BF16)
            v_t = t_s[n, h]
            parts = []
            for d in range(N_DIR):
                w_row = w_t[d * H_M + h:d * H_M + h + 1, :]
                parts += [v_t * w_row, jnp.broadcast_to(w_row, (NORM_ROWS, CHUNK))]
            local = _dot(jnp.concatenate(parts, axis=0).astype(BF16), kh)
            for d in range(N_DIR):
                c = d * H_M + h
                base = d * (DH_M + NORM_ROWS)
                kv_s[n, c, :, :DH_M] = local[base:base + DH_M]
                kv_s[n, c, 0:NORM_ROWS, DH_M:] = local[base + DH_M:base + DH_M + NORM_ROWS]

    for rb in range(n_rb):
        gate_s[rb_rows(rb), :M_W] = _dot(u_s[rb_rows(rb), :], wmo_ref[...])

    for g in range(n_seq):
        for d in range(N_DIR):
            order = list(range(g * cps, (g + 1) * cps))
            if d == 1:
                order.reverse()
            m_prev = m0_ref[0] if has_init else jnp.zeros((N_PAIR, CHUNK), F32)
            a1s, a2s = [], []
            for n in order:
                b_last_n, m_loc_n = stat_s[n, 0], stat_s[n, 1]
                m_new = jnp.maximum(b_last_n + m_prev, m_loc_n)
                a1s.append(jnp.exp(b_last_n + m_prev - m_new))
                a2s.append(jnp.exp(m_loc_n - m_new))
                mprev_s[n, d] = m_prev
                m_prev = m_new
            if emit_state:
                mfin_ref[g, d] = m_prev
            for h in range(H_M):
                c = d * H_M + h
                if has_init:
                    mem = c0_ref[0, 0, d, h]
                    nrm = jnp.broadcast_to(n0_ref[0][c:c + 1, :], (NORM_ROWS, DH_M))
                else:
                    mem = jnp.zeros((DH_M, DH_M), F32)
                    nrm = jnp.zeros((NORM_ROWS, DH_M), F32)
                for idx, n in enumerate(order):
                    a1 = a1s[idx][c:c + 1, :]
                    a2 = a2s[idx][c:c + 1, :]
                    local_mem = kv_s[n, c, :, :DH_M]
                    local_nrm = kv_s[n, c, 0:NORM_ROWS, DH_M:]
                    kv_s[n, c, :, :DH_M] = mem
                    kv_s[n, c, 0:NORM_ROWS, DH_M:] = nrm
                    mem = a1 * mem + a2 * local_mem
                    nrm = a1 * nrm + a2 * local_nrm
                if emit_state:
                    cfin_ref[g, 0, d, h] = mem
                    nfin_ref[g, c:c + 1, :] = nrm[0:1]

    tri_j = lax.broadcasted_iota(jnp.int32, (CHUNK, CHUNK), 0)
    tri_i = lax.broadcasted_iota(jnp.int32, (CHUNK, CHUNK), 1)
    mem_rows = DH_M + NORM_ROWS

    for n in range(n_chunks):
        rows = chunk_rows(n)
        zc = igb_s[rows, :]
        zt = zt_s[n]
        bigs = []
        for h in range(H_M):
            kh = k_s[rows, head_cols(h, DH_M)].astype(BF16)
            mems = [jnp.concatenate([kv_s[n, d * H_M + h, :, :DH_M], kv_s[n, d * H_M + h, 0:NORM_ROWS, DH_M:]],
                                    axis=0).astype(BF16) for d in range(N_DIR)]
            bigs.append(_dot(jnp.concatenate([kh] + mems, axis=0), qt_s[n, h]))
        for h in range(H_M):
            big = bigs[h]
            scores_t = big[:CHUNK]
            weights, crosses, w_ints, m_is = [], [], [], []
            for d in range(N_DIR):
                c = d * H_M + h
                a_col = zc[:, 2 * N_PAIR + c:2 * N_PAIR + c + 1]
                b_row = zt[N_PAIR + c:N_PAIR + c + 1, :]
                mask = (tri_j <= tri_i) if d == 0 else (tri_j >= tri_i)
                dlog = jnp.where(mask, b_row + a_col, -jnp.inf)
                lin = b_row + mprev_s[n, d][c:c + 1, :]
                m_i = jnp.maximum(jnp.max(dlog, axis=0, keepdims=True), lin)
                weights.append(scores_t * jnp.exp(dlog - m_i))
                crosses.append(big[CHUNK + d * mem_rows:CHUNK + (d + 1) * mem_rows])
                w_ints.append(jnp.exp(lin - m_i))
                m_is.append(m_i)
            intra = _dot(t_s[n, h].astype(BF16), jnp.concatenate(weights, axis=1).astype(BF16))
            h_sum_t = jnp.zeros((DH_M, CHUNK), F32)
            for d in range(N_DIR):
                cross, w_int = crosses[d], w_ints[d]
                num = intra[:, d * CHUNK:(d + 1) * CHUNK] + cross[:DH_M] * w_int
                den = jnp.sum(weights[d], axis=0, keepdims=True) + cross[DH_M:DH_M + 1] * w_int
                denom = jnp.maximum(jnp.abs(den), jnp.exp(-m_is[d]))
                h_sum_t = h_sum_t + num * (1.0 / denom)
            wide_s[rows, head_cols(h, DH_M)] = h_sum_t.T

    for rb in range(n_rb):
        rows = rb_rows(rb)
        hm = _sigmoid(gate_s[rows, :M_W]) * wide_s[rows, :M_W]
        for h in range(H_M):
            cols = head_cols(h, DH_M)
            hm_out[rows, cols] = _head_norm(hm[:, cols], mgn_ref[:, cols]).astype(BF16)


def _mixer_core_call(x, mod, mod_row, p, seq_len, group_rows, rope, init):
    t = x.shape[0]
    n_groups = t // group_rows
    n_chunks = group_rows // CHUNK
    seq_per_group = group_rows // seq_len
    use_rope = rope is not None
    has_init = init is not None
    emit_state = not has_init

    w_in = p["w_in"]
    groups = []
    start = 0
    for width in (2 * RET_QK, RET_V, RET_V, 2 * M_W, M_W, M_W, GATE_LANES):
        assert start % width == 0
        groups.append(pl.BlockSpec((D_MODEL, width), functools.partial(lambda i, c: (0, c), c=start // width)))
        start += width
    small = [p["conv_w"], p["conv_b"], p["gate_bias"], p["decay_rows"], p["ret_gn"], p["m_gn"]]
    inputs = [x, mod, p["norm_mix"]] + [w_in] * len(groups) + small
    in_specs = [
        pl.BlockSpec((group_rows, D_MODEL), lambda i: (i, 0)),
        pl.BlockSpec((1, N_MOD, D_MODEL), lambda i: (mod_row(i * group_rows), 0, 0)),
        _const_spec(p["norm_mix"].shape),
    ] + groups + [_const_spec(a.shape) for a in small]
    if use_rope:
        inputs += [rope[0], rope[1]]
        in_specs += [_const_spec(rope[0].shape), _const_spec(rope[1].shape)]
    if has_init:
        s0, c0, n0, m0 = init
        inputs += [s0, c0, n0, m0]
        in_specs += [
            pl.BlockSpec((1, 1, N_DIR, H_RET, DK_RET, DV_RET), lambda i: (i, 0, 0, 0, 0, 0)),
            pl.BlockSpec((1, 1, N_DIR, H_M, DH_M, DH_M), lambda i: (i, 0, 0, 0, 0, 0)),
            pl.BlockSpec((1, N_PAIR, DH_M), lambda i: (i, 0, 0)),
            pl.BlockSpec((1, N_PAIR, CHUNK), lambda i: (i, 0, 0)),
        ]

    out_shape = [jax.ShapeDtypeStruct((t, RET_V), BF16), jax.ShapeDtypeStruct((t, M_W), BF16)]
    out_specs = [pl.BlockSpec((group_rows, RET_V), lambda i: (i, 0)),
                 pl.BlockSpec((group_rows, M_W), lambda i: (i, 0))]
    if emit_state:
        n_seq = t // seq_len
        out_shape += [
            jax.ShapeDtypeStruct((n_seq, 1, N_DIR, H_RET, DK_RET, DV_RET), F32),
            jax.ShapeDtypeStruct((n_seq, 1, N_DIR, H_M, DH_M, DH_M), F32),
            jax.ShapeDtypeStruct((n_seq, N_PAIR, DH_M), F32),
            jax.ShapeDtypeStruct((n_seq, N_DIR, N_PAIR, CHUNK), F32),
        ]
        out_specs += [
            pl.BlockSpec((seq_per_group, 1, N_DIR, H_RET, DK_RET, DV_RET), lambda i: (i, 0, 0, 0, 0, 0)),
            pl.BlockSpec((seq_per_group, 1, N_DIR, H_M, DH_M, DH_M), lambda i: (i, 0, 0, 0, 0, 0)),
            pl.BlockSpec((seq_per_group, N_PAIR, DH_M), lambda i: (i, 0, 0)),
            pl.BlockSpec((seq_per_group, N_DIR, N_PAIR, CHUNK), lambda i: (i, 0, 0, 0)),
        ]

    scratch = [
        pltpu.VMEM((group_rows, D_MODEL), BF16),
        pltpu.VMEM((group_rows, RET_QK), F32),
        pltpu.VMEM((group_rows, RET_QK), F32),
        pltpu.VMEM((group_rows, RET_V), BF16),
        pltpu.VMEM((n_chunks, H_M, DH_M, CHUNK), F32),
        pltpu.VMEM((n_chunks, H_M, DH_M, CHUNK), BF16),
        pltpu.VMEM((n_chunks, N_PAIR, DK_RET, DV_RET), F32),
        pltpu.VMEM((group_rows, RET_V), F32),
        pltpu.VMEM((group_rows, RET_V), F32),
        pltpu.VMEM((group_rows, 2 * M_W), F32),
        pltpu.VMEM((group_rows, GATE_LANES), F32),
        pltpu.VMEM((n_chunks, 3 * N_PAIR, CHUNK), F32),
        pltpu.VMEM((n_chunks, N_PAIR, CHUNK), F32),
        pltpu.VMEM((H_RET, 5, CHUNK, CHUNK), F32),
        pltpu.VMEM((N_DIR * H_RET, DV_RET), F32),
        pltpu.VMEM((n_chunks, 2, N_PAIR, CHUNK), F32),
        pltpu.VMEM((n_chunks, N_DIR, N_PAIR, CHUNK), F32),
    ]
    kern = functools.partial(_mixer_core_kernel, group_rows=group_rows, seq_len=seq_len, use_rope=use_rope,
                             has_init=has_init, emit_state=emit_state)
    return pl.pallas_call(
        kern,
        out_shape=out_shape,
        grid=(n_groups,),
        in_specs=in_specs,
        out_specs=out_specs,
        scratch_shapes=scratch,
        compiler_params=pltpu.CompilerParams(
            dimension_semantics=("arbitrary",), vmem_limit_bytes=VMEM_LIMIT),
        name="mixer_core",
    )(*inputs)


def _mixer_out_kernel(x_ref, rg_ref, hm_ref, mod_ref, nmix_ref, wbg_ref, wru_ref, wmu_ref, wout_ref,
                      nffn_ref, w1_ref, w3_ref, w2_ref, nfin_ref, o_ref):
    mod = mod_ref[0]
    blocks = _row_blocks(x_ref.shape[0])

    def merge(x, bg, ret_branch, m_branch):
        gates = _sigmoid(bg)
        merged = (gates[:, :D_MODEL] * ret_branch + gates[:, D_MODEL:] * m_branch).astype(BF16)
        return x + mod[5:6] * _dot(merged, wout_ref[...])

    xs, pending = [], None
    for rows in blocks:
        x = x_ref[rows, :]
        u = _norm_mod(x, nmix_ref[...], mod[3:4], mod[4:5]).astype(BF16)
        branches = (x, _dot(u, wbg_ref[...]), _dot(rg_ref[rows, :], wru_ref[...]),
                    _dot(hm_ref[rows, :], wmu_ref[...]))
        if pending is not None:
            xs.append(merge(*pending))
        pending = branches
    xs.append(merge(*pending))

    xs = _swiglu_half_steps(xs, mod, 6, nffn_ref[...], w1_ref, w3_ref, w2_ref)
    for rows, x in zip(blocks, xs):
        ms = jnp.mean(x * x, axis=-1, keepdims=True)
        o_ref[rows, :] = x * lax.rsqrt(ms + EPS) * nfin_ref[...]


def _mixer_out_call(x, rgated, hmn, mod, mod_row, p):
    t = x.shape[0]
    tm = FFN_TILE
    consts = [p["norm_mix"], p["w_bg"], p["w_ret_up"], p["w_m_up"], p["w_out"],
              p["norm_ffn2"], p["w1_ffn2"], p["w3_ffn2"], p["w2_ffn2"], p["norm_final"]]
    return pl.pallas_call(
        _mixer_out_kernel,
        out_shape=jax.ShapeDtypeStruct((t, D_MODEL), F32),
        grid=(t // tm,),
        in_specs=[
            pl.BlockSpec((tm, D_MODEL), lambda i: (i, 0)),
            pl.BlockSpec((tm, RET_V), lambda i: (i, 0)),
            pl.BlockSpec((tm, M_W), lambda i: (i, 0)),
            pl.BlockSpec((1, N_MOD, D_MODEL), lambda i: (mod_row(i * tm), 0, 0)),
        ] + [_const_spec(a.shape) for a in consts],
        out_specs=pl.BlockSpec((tm, D_MODEL), lambda i: (i, 0)),
        compiler_params=pltpu.CompilerParams(
            dimension_semantics=("arbitrary",), vmem_limit_bytes=VMEM_LIMIT),
        name="mixer_out",
    )(x, rgated, hmn, mod, *consts)


def _rope_tables(seq_len):
    rows = seq_len // GRID_W
    r = jnp.repeat(jnp.arange(rows, dtype=F32), GRID_W)
    col = (jnp.arange(seq_len) % GRID_W).astype(F32)
    n_f = DK_RET // 4
    freqs = ROPE_BASE ** (-jnp.arange(n_f, dtype=F32) / n_f)
    ang = jnp.concatenate([r[:, None] * freqs, col[:, None] * freqs], axis=-1)
    cos, sin = jnp.cos(ang), jnp.sin(ang)
    return jnp.concatenate([cos, cos], axis=-1), jnp.concatenate([-sin, sin], axis=-1)


def _layer_params(l, norm_ffn1, w1_ffn1, w3_ffn1, w2_ffn1, norm_mix, conv_w, conv_b,
                  ret_decay_logit, b_igate, b_fgate, ret_gn, m_gn, norm_ffn2, norm_final):
    gate_bias = jnp.pad(jnp.concatenate([b_igate[l], b_fgate[l]]), (0, GATE_LANES - 2 * N_PAIR))
    p = dict(
        norm_ffn1=norm_ffn1[l][None, :], norm_mix=norm_mix[l][None, :], norm_ffn2=norm_ffn2[l][None, :],
        norm_final=norm_final[None, :],
        w1_ffn1=w1_ffn1[l].astype(BF16), w3_ffn1=w3_ffn1[l].astype(BF16), w2_ffn1=w2_ffn1[l].astype(BF16),
        conv_w=conv_w[l], conv_b=conv_b[l][None, :],
        gate_bias=gate_bias[None, :],
        decay_rows=jnp.broadcast_to(ret_decay_logit[l].reshape(N_DIR * H_RET, 1), (N_DIR * H_RET, DV_RET)),
        ret_gn=ret_gn[l][None, :], m_gn=m_gn[l][None, :],
    )
    return p


def kernel(x_prompt, x_sample, c, state_ret, state_mlstm_C, state_mlstm_n, state_mlstm_m, c_ctx, w_ada, b_ada, norm_ffn1, w1_ffn1, w3_ffn1, w2_ffn1, norm_mix, w_in, conv_w, conv_b, ret_decay_logit, b_igate, b_fgate, ret_gn, m_gn, w_ret_up, w_m_up, w_out, norm_ffn2, w1_ffn2, w3_ffn2, w2_ffn2, norm_final):
    bp, lp, _ = x_prompt.shape
    bs, ls, _ = x_sample.shape
    depth = w_ada.shape[0]
    assert depth == 1 and lp % CHUNK == 0 and PROMPT_GROUP_ROWS % lp == 0 and ls <= MAX_GROUP_ROWS
    assert lp & (lp - 1) == 0 and ls & (ls - 1) == 0

    ctx_row = F32_SUBLANES * pl.cdiv(bs, F32_SUBLANES)
    n_cond = ctx_row + F32_SUBLANES
    cvec = jnp.zeros((n_cond, D_MODEL), F32).at[:bs].set(c).at[ctx_row].set(c_ctx)
    rope = _rope_tables(ls)

    xp = x_prompt.reshape(bp * lp, D_MODEL)
    xs = x_sample.reshape(bs * ls, D_MODEL)
    prompt_row = lambda r: ctx_row
    sample_row = lambda r: r // ls

    l = 0
    p = _layer_params(l, norm_ffn1, w1_ffn1, w3_ffn1, w2_ffn1, norm_mix, conv_w, conv_b,
                      ret_decay_logit, b_igate, b_fgate, ret_gn, m_gn, norm_ffn2, norm_final)
    mod = _ada_call(cvec, w_ada[l], b_ada[l][None, :]).reshape(n_cond, N_MOD, D_MODEL)

    n0 = state_mlstm_n[:, l].astype(F32).reshape(bs, N_PAIR, DH_M)
    m0 = jnp.broadcast_to(state_mlstm_m[:, l].astype(F32).reshape(bs, N_PAIR, 1), (bs, N_PAIR, CHUNK))
    init = (state_ret.astype(F32), state_mlstm_C.astype(F32), n0, m0)

    whole = lambda a: (a, False, ((0, a.shape[1]),))
    bg_start = D_IN - N_DIR * D_MODEL
    xp1, (p["w_in"], p["w_bg"]) = _ffn_call(
        xp, mod, prompt_row, p["norm_ffn1"], p["w1_ffn1"], p["w3_ffn1"], p["w2_ffn1"],
        casts=((w_in[l].T, True, ((0, D_IN), (bg_start, N_DIR * D_MODEL))),))
    xs1, (p["w1_ffn2"], p["w3_ffn2"], p["w2_ffn2"], p["w_ret_up"], p["w_m_up"], p["w_out"]) = _ffn_call(
        xs, mod, sample_row, p["norm_ffn1"], p["w1_ffn1"], p["w3_ffn1"], p["w2_ffn1"],
        casts=(whole(w1_ffn2[l]), whole(w3_ffn2[l]), whole(w2_ffn2[l]),
               whole(w_ret_up[l]), whole(w_m_up[l]), whole(w_out[l])))

    rg_p, hm_p, s_fin, c_fin, n_fin, m_fin = _mixer_core_call(xp1, mod, prompt_row, p, lp, PROMPT_GROUP_ROWS, None, None)
    rg_s, hm_s = _mixer_core_call(xs1, mod, sample_row, p, ls, ls, rope, init)

    yp = _mixer_out_call(xp1, rg_p, hm_p, mod, prompt_row, p)
    ys = _mixer_out_call(xs1, rg_s, hm_s, mod, sample_row, p)

    dt = x_prompt.dtype
    new_c = c_fin
    new_n = n_fin.reshape(bp, 1, N_DIR, H_M, DH_M)
    m_diag = jnp.stack([m_fin[:, d, d * H_M:(d + 1) * H_M, 0] for d in range(N_DIR)], axis=1)
    new_m = m_diag.reshape(bp, 1, N_DIR, H_M)
    return (yp.reshape(bp, lp, D_MODEL).astype(dt), ys.reshape(bs, ls, D_MODEL).astype(dt),
            s_fin.astype(dt), new_c.astype(dt), new_n.astype(dt), new_m.astype(dt))
```

```python
import functools

import jax
import jax.numpy as jnp
from jax import lax
from jax.experimental import pallas as pl
from jax.experimental.pallas import tpu as pltpu

F32 = jnp.float32
BF16 = jnp.bfloat16

D_MODEL = 1024
D_FF = 2816
CHUNK = 128
N_DIR = 2
H_RET = 4
DK_RET = 128
DV_RET = 256
H_M = 4
DH_M = 128
RET_QK = H_RET * DK_RET
RET_V = H_RET * DV_RET
M_W = H_M * DH_M
D_IN = 2 * RET_QK + 2 * RET_V + 4 * M_W + 2 * N_DIR * H_M + N_DIR * D_MODEL
GRID_W = 64
ROPE_BASE = 10000.0
EPS = 1e-6
GN_EPS = 1e-5
N_MOD = 9
N_PAIR = N_DIR * H_M
GATE_LANES = 128
LANES = 128
F32_SUBLANES = 8
BF16_SUBLANES = 16
NORM_ROWS = BF16_SUBLANES

MAX_GROUP_ROWS = 1024
PROMPT_GROUP_ROWS = 512
ROW_BLOCK = 256
ADA_TILE = 3 * D_MODEL
ADA_STREAMS = 4
SUB_ROWS = 256
FFN_TILE = 512
V7X_VMEM_BYTES = 64 * 1024 * 1024
VMEM_LIMIT = V7X_VMEM_BYTES - 4 * 1024 * 1024


def _sigmoid(x):
    return 1.0 / (1.0 + jnp.exp(-x))


def _silu(x):
    return x * _sigmoid(x)


def _log_sigmoid(x):
    return -(jnp.maximum(-x, 0.0) + jnp.log1p(jnp.exp(-jnp.abs(x))))


def _norm_mod(x, gain, shift, scale):
    ms = jnp.mean(x * x, axis=-1, keepdims=True)
    y = x * lax.rsqrt(ms + EPS) * gain
    return y * (1.0 + scale) + shift


def _dot(a, b):
    return jnp.dot(a, b, preferred_element_type=F32)


def _dot_nt(a, b):
    return lax.dot_general(a, b, (((1,), (1,)), ((), ())), preferred_element_type=F32)


def _head_norm(o, gain):
    mu = jnp.mean(o, axis=-1, keepdims=True)
    oc = o - mu
    var = jnp.mean(oc * oc, axis=-1, keepdims=True)
    return oc * lax.rsqrt(var + GN_EPS) * gain


def _ada_kernel(c_ref, *refs):
    w_refs, b_ref, o_ref = refs[:ADA_STREAMS], refs[ADA_STREAMS], refs[ADA_STREAMS + 1]
    s = _silu(c_ref[...]).astype(BF16)
    slab = D_MODEL // ADA_STREAMS
    acc = b_ref[...]
    for k, w_ref in enumerate(w_refs):
        acc = acc + _dot(s[:, k * slab:(k + 1) * slab], w_ref[...].astype(BF16))
    o_ref[...] = acc


def _ada_call(cvec, w_ada, b_ada):
    rows = cvec.shape[0]
    n_out = w_ada.shape[1]
    tile = ADA_TILE
    slab = D_MODEL // ADA_STREAMS
    return pl.pallas_call(
        _ada_kernel,
        out_shape=jax.ShapeDtypeStruct((rows, n_out), F32),
        grid=(n_out // tile,),
        in_specs=[pl.BlockSpec((rows, D_MODEL), lambda j: (0, 0))]
        + [pl.BlockSpec((slab, tile), functools.partial(lambda j, k: (k, j), k=k)) for k in range(ADA_STREAMS)]
        + [pl.BlockSpec((1, tile), lambda j: (0, j))],
        out_specs=pl.BlockSpec((rows, tile), lambda j: (0, j)),
        compiler_params=pltpu.CompilerParams(
            dimension_semantics=("arbitrary",), vmem_limit_bytes=VMEM_LIMIT),
        name="ada",
    )(cvec, *[w_ada] * ADA_STREAMS, b_ada)


def _row_blocks(n_rows):
    return [slice(r, r + SUB_ROWS) for r in range(0, n_rows, SUB_ROWS)]


def _swiglu_half_steps(xs, mod, k0, gain, w1_ref, w3_ref, w2_ref):
    shift, scale, gate = mod[k0:k0 + 1], mod[k0 + 1:k0 + 2], mod[k0 + 2:k0 + 3]

    def finish(x, h1, h3):
        h = (_silu(h1) * h3).astype(BF16)
        return x + 0.5 * gate * _dot(h, w2_ref[...])

    outs, pending = [], None
    for x in xs:
        u = _norm_mod(x, gain, shift, scale).astype(BF16)
        up = (x, _dot(u, w1_ref[...]), _dot(u, w3_ref[...]))
        if pending is not None:
            outs.append(finish(*pending))
        pending = up
    outs.append(finish(*pending))
    return outs


def _ffn_kernel(*refs, cast_plan):
    n_in = 6 + len(cast_plan)
    x_ref, mod_ref, g_ref, w1_ref, w3_ref, w2_ref = refs[:6]
    o_ref = refs[n_in]
    blocks = _row_blocks(x_ref.shape[0])
    ys = _swiglu_half_steps([x_ref[rows, :] for rows in blocks], mod_ref[0], 0, g_ref[...],
                            w1_ref, w3_ref, w2_ref)
    for rows, y in zip(blocks, ys):
        o_ref[rows, :] = y
    cast_out = iter(refs[n_in + 1:])
    for src_ref, (transposed, ranges) in zip(refs[6:n_in], cast_plan):
        block = src_ref[...].T if transposed else src_ref[...]
        for start, width in ranges:
            next(cast_out)[...] = block[:, start:start + width].astype(BF16)


def _const_spec(shape):
    nd = len(shape)
    return pl.BlockSpec(shape, lambda i: (0,) * nd)


def _ffn_call(x, mod, mod_row, gain, w1, w3, w2, casts=()):
    t = x.shape[0]
    tm = FFN_TILE
    steps = t // tm
    in_specs = [
        pl.BlockSpec((tm, D_MODEL), lambda i: (i, 0)),
        pl.BlockSpec((1, N_MOD, D_MODEL), lambda i: (mod_row(i * tm), 0, 0)),
        _const_spec((1, D_MODEL)),
        _const_spec((D_MODEL, D_FF)),
        _const_spec((D_MODEL, D_FF)),
        _const_spec((D_FF, D_MODEL)),
    ]
    out_shape = [jax.ShapeDtypeStruct((t, D_MODEL), F32)]
    out_specs = [pl.BlockSpec((tm, D_MODEL), lambda i: (i, 0))]
    for src, transposed, ranges in casts:
        rows, cols = src.shape[::-1] if transposed else src.shape
        if transposed:
            assert rows % (steps * LANES) == 0
            in_specs.append(pl.BlockSpec((cols, rows // steps), lambda i: (0, i)))
        else:
            assert rows % (steps * BF16_SUBLANES) == 0
            in_specs.append(pl.BlockSpec((rows // steps, cols), lambda i: (i, 0)))
        for _, width in ranges:
            out_shape.append(jax.ShapeDtypeStruct((rows, width), BF16))
            out_specs.append(pl.BlockSpec((rows // steps, width), lambda i: (i, 0)))
    outs = pl.pallas_call(
        functools.partial(_ffn_kernel, cast_plan=tuple((tr, r) for _, tr, r in casts)),
        out_shape=out_shape,
        grid=(steps,),
        in_specs=in_specs,
        out_specs=out_specs,
        compiler_params=pltpu.CompilerParams(
            dimension_semantics=("arbitrary",), vmem_limit_bytes=VMEM_LIMIT),
        name="ffn",
    )(x, mod, gain, w1, w3, w2, *[src for src, _, _ in casts])
    return outs[0], outs[1:]


def _mixer_core_kernel(*refs, group_rows, seq_len, use_rope, has_init, emit_state):
    it = iter(refs)
    x_ref, mod_ref, nmix_ref = next(it), next(it), next(it)
    wrqk_ref, wrv_ref, wrg_ref = next(it), next(it), next(it)
    wmqk_ref, wmv_ref, wmo_ref, wgt_ref = next(it), next(it), next(it), next(it)
    convw_ref, convb_ref, gbias_ref, decay_ref = next(it), next(it), next(it), next(it)
    rgn_ref, mgn_ref = next(it), next(it)
    if use_rope:
        cos_ref, sin_ref = next(it), next(it)
    if has_init:
        s0_ref, c0_ref, n0_ref, m0_ref = next(it), next(it), next(it), next(it)
    rg_out, hm_out = next(it), next(it)
    if emit_state:
        sfin_ref, cfin_ref, nfin_ref, mfin_ref = next(it), next(it), next(it), next(it)
    (u_s, q_s, k_s, v_s, t_s, qt_s, kv_s, wide_s, gate_s, pre_s, igb_s, zt_s, wt_s, dec_s, gch_s, stat_s,
     mprev_s) = (next(it) for _ in range(17))

    rows_total = group_rows
    n_chunks = rows_total // CHUNK
    cps = seq_len // CHUNK
    n_seq = rows_total // seq_len
    n_rb = rows_total // ROW_BLOCK
    mod = mod_ref[0]

    def rb_rows(rb):
        return slice(rb * ROW_BLOCK, (rb + 1) * ROW_BLOCK)

    def chunk_rows(n):
        return slice(n * CHUNK, (n + 1) * CHUNK)

    def head_cols(h, width):
        return slice(h * width, (h + 1) * width)

    @pl.when(pl.program_id(0) == 0)
    def _():
        pos_i = lax.broadcasted_iota(jnp.int32, (CHUNK, CHUNK), 0).astype(F32)
        pos_j = lax.broadcasted_iota(jnp.int32, (CHUNK, CHUNK), 1).astype(F32)
        lg_rows = _log_sigmoid(decay_ref[...])
        for h in range(H_RET):
            lgf = lg_rows[h:h + 1, :CHUNK]
            lgb = lg_rows[H_RET + h:H_RET + h + 1, :CHUNK]
            dec_s[h, 0] = jnp.where(pos_i > pos_j, jnp.exp((pos_i - pos_j) * lgf),
                                    jnp.where(pos_i < pos_j, jnp.exp((pos_j - pos_i) * lgb), 2.0))
            dec_s[h, 1] = jnp.exp((CHUNK - 1.0 - pos_j) * lgf)
            dec_s[h, 2] = jnp.exp(pos_j * lgb)
            dec_s[h, 3] = jnp.exp((pos_i + 1.0) * lgf)
            dec_s[h, 4] = jnp.exp((CHUNK - pos_i) * lgb)
        gch_s[...] = jnp.exp(CHUNK * lg_rows)

    g_chunk = gch_s[...]

    chunks_per_rb = ROW_BLOCK // CHUNK
    for rb in range(n_rb):
        rows = rb_rows(rb)
        u_s[rows, :] = _norm_mod(x_ref[rows, :], nmix_ref[...], mod[3:4], mod[4:5]).astype(BF16)
        ub = u_s[rows, :]
        qk = _dot(ub, wrqk_ref[...])
        q = qk[:, :RET_QK]
        k = qk[:, RET_QK:] * (DK_RET ** -0.5)
        for h in range(H_RET):
            cols = head_cols(h, DK_RET)
            qh, kh = q[:, cols], k[:, cols]
            if use_rope:
                cs, sn = cos_ref[rows, :], sin_ref[rows, :]
                qh = qh * cs + pltpu.roll(qh, DK_RET // 2, 1) * sn
                kh = kh * cs + pltpu.roll(kh, DK_RET // 2, 1) * sn
            q_s[rows, cols] = qh
            for ch in range(chunks_per_rb):
                t_s[rb * chunks_per_rb + ch, h] = kh[ch * CHUNK:(ch + 1) * CHUNK, :].T
        v_s[rows, :] = _dot(ub, wrv_ref[...]).astype(BF16)

    for n in range(n_chunks):
        for h in range(H_RET):
            k_t = t_s[n, h]
            kcat = jnp.concatenate([k_t * dec_s[h, 1], k_t * dec_s[h, 2]], axis=0).astype(BF16)
            kv = _dot(kcat, v_s[chunk_rows(n), head_cols(h, DV_RET)])
            kv_s[n, 2 * h] = kv[:DK_RET]
            kv_s[n, 2 * h + 1] = kv[DK_RET:]

    for rb in range(n_rb):
        gate_s[rb_rows(rb), :] = _dot(u_s[rb_rows(rb), :], wrg_ref[...])

    for g in range(n_seq):
        for h in range(H_RET):
            for d in range(N_DIR):
                slot = 2 * h + d
                decay = g_chunk[d * H_RET + h:d * H_RET + h + 1, :]
                state = s0_ref[0, 0, d, h] if has_init else jnp.zeros((DK_RET, DV_RET), F32)
                order = range(g * cps, (g + 1) * cps)
                for n in (order if d == 0 else reversed(order)):
                    local = kv_s[n, slot]
                    kv_s[n, slot] = state
                    state = decay * state + local
                if emit_state:
                    sfin_ref[g, 0, d, h] = state

    def ret_scores(n):
        return [_dot(q_s[chunk_rows(n), head_cols(h, DK_RET)].astype(BF16), t_s[n, h].astype(BF16))
                for h in range(H_RET)]

    def ret_outputs(n, all_scores):
        rows = chunk_rows(n)
        for h in range(H_RET):
            qh = q_s[rows, head_cols(h, DK_RET)]
            vh = v_s[rows, head_cols(h, DV_RET)]
            sm = (all_scores[h] * dec_s[h, 0]).astype(BF16)
            qcat = jnp.concatenate([qh * dec_s[h, 3], qh * dec_s[h, 4]], axis=1).astype(BF16)
            scat = jnp.concatenate([kv_s[n, 2 * h], kv_s[n, 2 * h + 1]], axis=0).astype(BF16)
            o = _dot(sm, vh) + _dot(qcat, scat)
            wide_s[rows, head_cols(h, DV_RET)] = _head_norm(o, rgn_ref[:, head_cols(h, DV_RET)])

    for rb in range(n_rb):
        rows = rb_rows(rb)
        chunks = range(rb * chunks_per_rb, (rb + 1) * chunks_per_rb)
        scores = [ret_scores(n) for n in chunks]
        ub = u_s[rows, :]
        glane = lax.broadcasted_iota(jnp.int32, (ROW_BLOCK, GATE_LANES), 1)
        gt = jnp.where(glane < 2 * N_PAIR, _dot(ub, wgt_ref[...]) + gbias_ref[...], 0.0)
        igb_s[rows, :] = jnp.where(glane < N_PAIR, gt, _log_sigmoid(gt))
        pre_s[rows, :] = _dot(ub, wmqk_ref[...])
        for n, sc in zip(chunks, scores):
            ret_outputs(n, sc)

    for rb in range(n_rb):
        rows = rb_rows(rb)
        rg_out[rows, :] = (wide_s[rows, :] * _silu(gate_s[rows, :])).astype(BF16)

    for rb in range(n_rb):
        mv = _dot(u_s[rb_rows(rb), :], wmv_ref[...])
        for ch in range(chunks_per_rb):
            for h in range(H_M):
                t_s[rb * chunks_per_rb + ch, h] = mv[ch * CHUNK:(ch + 1) * CHUNK, head_cols(h, DH_M)].T

    row_id = lax.broadcasted_iota(jnp.int32, (rows_total, CHUNK), 0)
    seq_pos = row_id & (seq_len - 1)
    for cb in range(2 * M_W // CHUNK):
        cols = slice(cb * CHUNK, (cb + 1) * CHUNK)
        xc = pre_s[:, cols]
        prev = jnp.where(seq_pos == 0, 0.0, pltpu.roll(xc, 1, 0))
        nxt = jnp.where(seq_pos == seq_len - 1, 0.0, pltpu.roll(xc, rows_total - 1, 0))
        y = convb_ref[:, cols] + convw_ref[0:1, cols] * prev
        y = y + convw_ref[1:2, cols] * xc
        y = y + convw_ref[2:3, cols] * nxt
        y = _silu(y)
        if cb < H_M:
            for n in range(n_chunks):
                qt_s[n, cb] = y[n * CHUNK:(n + 1) * CHUNK, :].T.astype(BF16)
        else:
            k_s[:, head_cols(cb - H_M, DH_M)] = y * (DH_M ** -0.5)

    z = igb_s[...]
    cpos = row_id & (CHUNK - 1)
    glane = lax.broadcasted_iota(jnp.int32, (rows_total, GATE_LANES), 1)
    pre, suf = z, z
    step = 1
    while step < CHUNK:
        pre = pre + jnp.where(cpos >= step, pltpu.roll(pre, step, 0), 0.0)
        suf = suf + jnp.where(cpos < CHUNK - step, pltpu.roll(suf, rows_total - step, 0), 0.0)
        step *= 2
    cum = jnp.where(glane < N_PAIR + H_M, pre, suf)
    i_minus_b = z - pltpu.roll(cum, GATE_LANES - N_PAIR, 1)
    igb = jnp.where(glane < N_PAIR, z,
                    jnp.where(glane < 2 * N_PAIR, cum, pltpu.roll(i_minus_b, 2 * N_PAIR, 1)))
    igb_s[...] = igb

    for n in range(n_chunks):
        zt_s[n] = igb[n * CHUNK:(n + 1) * CHUNK, :].T[0:3 * N_PAIR, :]
    ig_all = zt_s[:, 0:N_PAIR, :]
    b_all = zt_s[:, N_PAIR:2 * N_PAIR, :]
    pair = lax.broadcasted_iota(jnp.int32, (n_chunks, N_PAIR, 1), 1)
    b_last = jnp.where(pair < H_M, b_all[:, :, CHUNK - 1:CHUNK], b_all[:, :, 0:1])
    g_all = (b_last - b_all) + ig_all
    m_loc = jnp.max(g_all, axis=-1, keepdims=True)
    wt_s[...] = jnp.exp(g_all - m_loc)
    stat_s[:, 0] = jnp.broadcast_to(b_last, (n_chunks, N_PAIR, CHUNK))
    stat_s[:, 1] = jnp.broadcast_to(m_loc, (n_chunks, N_PAIR, CHUNK))

    for n in range(n_chunks):
        rows = chunk_rows(n)
        w_t = wt_s[n]
        for h in range(H_M):
            kh = k_s[rows, head_cols(h, DH_M)].astype(BF16)
            v_t = t_s[n, h]
            parts = []
            for d in range(N_DIR):
                w_row = w_t[d * H_M + h:d * H_M + h + 1, :]
                parts += [v_t * w_row, jnp.broadcast_to(w_row, (NORM_ROWS, CHUNK))]
            local = _dot(jnp.concatenate(parts, axis=0).astype(BF16), kh)
            for d in range(N_DIR):
                c = d * H_M + h
                base = d * (DH_M + NORM_ROWS)
                kv_s[n, c, :, :DH_M] = local[base:base + DH_M]
                kv_s[n, c, 0:NORM_ROWS, DH_M:] = local[base + DH_M:base + DH_M + NORM_ROWS]

    for rb in range(n_rb):
        gate_s[rb_rows(rb), :M_W] = _dot(u_s[rb_rows(rb), :], wmo_ref[...])

    for g in range(n_seq):
        for d in range(N_DIR):
            order = list(range(g * cps, (g + 1) * cps))
            if d == 1:
                order.reverse()
            m_prev = m0_ref[0] if has_init else jnp.zeros((N_PAIR, CHUNK), F32)
            a1s, a2s = [], []
            for n in order:
                b_last_n, m_loc_n = stat_s[n, 0], stat_s[n, 1]
                m_new = jnp.maximum(b_last_n + m_prev, m_loc_n)
                a1s.append(jnp.exp(b_last_n + m_prev - m_new))
                a2s.append(jnp.exp(m_loc_n - m_new))
                mprev_s[n, d] = m_prev
                m_prev = m_new
            if emit_state:
                mfin_ref[g, d] = m_prev
            for h in range(H_M):
                c = d * H_M + h
                if has_init:
                    mem = c0_ref[0, 0, d, h]
                    nrm = jnp.broadcast_to(n0_ref[0][c:c + 1, :], (NORM_ROWS, DH_M))
                else:
                    mem = jnp.zeros((DH_M, DH_M), F32)
                    nrm = jnp.zeros((NORM_ROWS, DH_M), F32)
                for idx, n in enumerate(order):
                    a1 = a1s[idx][c:c + 1, :]
                    a2 = a2s[idx][c:c + 1, :]
                    local_mem = kv_s[n, c, :, :DH_M]
                    local_nrm = kv_s[n, c, 0:NORM_ROWS, DH_M:]
                    kv_s[n, c, :, :DH_M] = mem
                    kv_s[n, c, 0:NORM_ROWS, DH_M:] = nrm
                    mem = a1 * mem + a2 * local_mem
                    nrm = a1 * nrm + a2 * local_nrm
                if emit_state:
                    cfin_ref[g, 0, d, h] = mem
                    nfin_ref[g, c:c + 1, :] = nrm[0:1]

    tri_j = lax.broadcasted_iota(jnp.int32, (CHUNK, CHUNK), 0)
    tri_i = lax.broadcasted_iota(jnp.int32, (CHUNK, CHUNK), 1)
    mem_rows = DH_M + NORM_ROWS

    for n in range(n_chunks):
        rows = chunk_rows(n)
        zc = igb_s[rows, :]
        zt = zt_s[n]
        bigs = []
        for h in range(H_M):
            kh = k_s[rows, head_cols(h, DH_M)].astype(BF16)
            mems = [jnp.concatenate([kv_s[n, d * H_M + h, :, :DH_M], kv_s[n, d * H_M + h, 0:NORM_ROWS, DH_M:]],
                                    axis=0).astype(BF16) for d in range(N_DIR)]
            bigs.append(_dot(jnp.concatenate([kh] + mems, axis=0), qt_s[n, h]))
        for h in range(H_M):
            big = bigs[h]
            scores_t = big[:CHUNK]
            weights, crosses, w_ints, m_is = [], [], [], []
            for d in range(N_DIR):
                c = d * H_M + h
                a_col = zc[:, 2 * N_PAIR + c:2 * N_PAIR + c + 1]
                b_row = zt[N_PAIR + c:N_PAIR + c + 1, :]
                mask = (tri_j <= tri_i) if d == 0 else (tri_j >= tri_i)
                dlog = jnp.where(mask, b_row + a_col, -jnp.inf)
                lin = b_row + mprev_s[n, d][c:c + 1, :]
                m_i = jnp.maximum(jnp.max(dlog, axis=0, keepdims=True), lin)
                weights.append(scores_t * jnp.exp(dlog - m_i))
                crosses.append(big[CHUNK + d * mem_rows:CHUNK + (d + 1) * mem_rows])
                w_ints.append(jnp.exp(lin - m_i))
                m_is.append(m_i)
            intra = _dot(t_s[n, h].astype(BF16), jnp.concatenate(weights, axis=1).astype(BF16))
            h_sum_t = jnp.zeros((DH_M, CHUNK), F32)
            for d in range(N_DIR):
                cross, w_int = crosses[d], w_ints[d]
                num = intra[:, d * CHUNK:(d + 1) * CHUNK] + cross[:DH_M] * w_int
                den = jnp.sum(weights[d], axis=0, keepdims=True) + cross[DH_M:DH_M + 1] * w_int
                denom = jnp.maximum(jnp.abs(den), jnp.exp(-m_is[d]))
                h_sum_t = h_sum_t + num * (1.0 / denom)
            wide_s[rows, head_cols(h, DH_M)] = h_sum_t.T

    for rb in range(n_rb):
        rows = rb_rows(rb)
        hm = _sigmoid(gate_s[rows, :M_W]) * wide_s[rows, :M_W]
        for h in range(H_M):
            cols = head_cols(h, DH_M)
            hm_out[rows, cols] = _head_norm(hm[:, cols], mgn_ref[:, cols]).astype(BF16)


def _mixer_core_call(x, mod, mod_row, p, seq_len, group_rows, rope, init):
    t = x.shape[0]
    n_groups = t // group_rows
    n_chunks = group_rows // CHUNK
    seq_per_group = group_rows // seq_len
    use_rope = rope is not None
    has_init = init is not None
    emit_state = not has_init

    w_in = p["w_in"]
    groups = []
    start = 0
    for width in (2 * RET_QK, RET_V, RET_V, 2 * M_W, M_W, M_W, GATE_LANES):
        assert start % width == 0
        groups.append(pl.BlockSpec((D_MODEL, width), functools.partial(lambda i, c: (0, c), c=start // width)))
        start += width
    small = [p["conv_w"], p["conv_b"], p["gate_bias"], p["decay_rows"], p["ret_gn"], p["m_gn"]]
    inputs = [x, mod, p["norm_mix"]] + [w_in] * len(groups) + small
    in_specs = [
        pl.BlockSpec((group_rows, D_MODEL), lambda i: (i, 0)),
        pl.BlockSpec((1, N_MOD, D_MODEL), lambda i: (mod_row(i * group_rows), 0, 0)),
        _const_spec(p["norm_mix"].shape),
    ] + groups + [_const_spec(a.shape) for a in small]
    if use_rope:
        inputs += [rope[0], rope[1]]
        in_specs += [_const_spec(rope[0].shape), _const_spec(rope[1].shape)]
    if has_init:
        s0, c0, n0, m0 = init
        inputs += [s0, c0, n0, m0]
        in_specs += [
            pl.BlockSpec((1, 1, N_DIR, H_RET, DK_RET, DV_RET), lambda i: (i, 0, 0, 0, 0, 0)),
            pl.BlockSpec((1, 1, N_DIR, H_M, DH_M, DH_M), lambda i: (i, 0, 0, 0, 0, 0)),
            pl.BlockSpec((1, N_PAIR, DH_M), lambda i: (i, 0, 0)),
            pl.BlockSpec((1, N_PAIR, CHUNK), lambda i: (i, 0, 0)),
        ]

    out_shape = [jax.ShapeDtypeStruct((t, RET_V), BF16), jax.ShapeDtypeStruct((t, M_W), BF16)]
    out_specs = [pl.BlockSpec((group_rows, RET_V), lambda i: (i, 0)),
                 pl.BlockSpec((group_rows, M_W), lambda i: (i, 0))]
    if emit_state:
        n_seq = t // seq_len
        out_shape += [
            jax.ShapeDtypeStruct((n_seq, 1, N_DIR, H_RET, DK_RET, DV_RET), F32),
            jax.ShapeDtypeStruct((n_seq, 1, N_DIR, H_M, DH_M, DH_M), F32),
            jax.ShapeDtypeStruct((n_seq, N_PAIR, DH_M), F32),
            jax.ShapeDtypeStruct((n_seq, N_DIR, N_PAIR, CHUNK), F32),
        ]
        out_specs += [
            pl.BlockSpec((seq_per_group, 1, N_DIR, H_RET, DK_RET, DV_RET), lambda i: (i, 0, 0, 0, 0, 0)),
            pl.BlockSpec((seq_per_group, 1, N_DIR, H_M, DH_M, DH_M), lambda i: (i, 0, 0, 0, 0, 0)),
            pl.BlockSpec((seq_per_group, N_PAIR, DH_M), lambda i: (i, 0, 0)),
            pl.BlockSpec((seq_per_group, N_DIR, N_PAIR, CHUNK), lambda i: (i, 0, 0, 0)),
        ]

    scratch = [
        pltpu.VMEM((group_rows, D_MODEL), BF16),
        pltpu.VMEM((group_rows, RET_QK), F32),
        pltpu.VMEM((group_rows, RET_QK), F32),
        pltpu.VMEM((group_rows, RET_V), BF16),
        pltpu.VMEM((n_chunks, H_M, DH_M, CHUNK), F32),
        pltpu.VMEM((n_chunks, H_M, DH_M, CHUNK), BF16),
        pltpu.VMEM((n_chunks, N_PAIR, DK_RET, DV_RET), F32),
        pltpu.VMEM((group_rows, RET_V), F32),
        pltpu.VMEM((group_rows, RET_V), F32),
        pltpu.VMEM((group_rows, 2 * M_W), F32),
        pltpu.VMEM((group_rows, GATE_LANES), F32),
        pltpu.VMEM((n_chunks, 3 * N_PAIR, CHUNK), F32),
        pltpu.VMEM((n_chunks, N_PAIR, CHUNK), F32),
        pltpu.VMEM((H_RET, 5, CHUNK, CHUNK), F32),
        pltpu.VMEM((N_DIR * H_RET, DV_RET), F32),
        pltpu.VMEM((n_chunks, 2, N_PAIR, CHUNK), F32),
        pltpu.VMEM((n_chunks, N_DIR, N_PAIR, CHUNK), F32),
    ]
    kern = functools.partial(_mixer_core_kernel, group_rows=group_rows, seq_len=seq_len, use_rope=use_rope,
                             has_init=has_init, emit_state=emit_state)
    return pl.pallas_call(
        kern,
        out_shape=out_shape,
        grid=(n_groups,),
        in_specs=in_specs,
        out_specs=out_specs,
        scratch_shapes=scratch,
        compiler_params=pltpu.CompilerParams(
            dimension_semantics=("arbitrary",), vmem_limit_bytes=VMEM_LIMIT),
        name="mixer_core",
    )(*inputs)


def _mixer_out_kernel(x_ref, rg_ref, hm_ref, mod_ref, nmix_ref, wbg_ref, wru_ref, wmu_ref, wout_ref,
                      nffn_ref, w1_ref, w3_ref, w2_ref, nfin_ref, o_ref):
    mod = mod_ref[0]
    blocks = _row_blocks(x_ref.shape[0])

    def merge(x, bg, ret_branch, m_branch):
        gates = _sigmoid(bg)
        merged = (gates[:, :D_MODEL] * ret_branch + gates[:, D_MODEL:] * m_branch).astype(BF16)
        return x + mod[5:6] * _dot(merged, wout_ref[...])

    xs, pending = [], None
    for rows in blocks:
        x = x_ref[rows, :]
        u = _norm_mod(x, nmix_ref[...], mod[3:4], mod[4:5]).astype(BF16)
        branches = (x, _dot(u, wbg_ref[...]), _dot(rg_ref[rows, :], wru_ref[...]),
                    _dot(hm_ref[rows, :], wmu_ref[...]))
        if pending is not None:
            xs.append(merge(*pending))
        pending = branches
    xs.append(merge(*pending))

    xs = _swiglu_half_steps(xs, mod, 6, nffn_ref[...], w1_ref, w3_ref, w2_ref)
    for rows, x in zip(blocks, xs):
        ms = jnp.mean(x * x, axis=-1, keepdims=True)
        o_ref[rows, :] = x * lax.rsqrt(ms + EPS) * nfin_ref[...]


def _mixer_out_call(x, rgated, hmn, mod, mod_row, p):
    t = x.shape[0]
    tm = FFN_TILE
    consts = [p["norm_mix"], p["w_bg"], p["w_ret_up"], p["w_m_up"], p["w_out"],
              p["norm_ffn2"], p["w1_ffn2"], p["w3_ffn2"], p["w2_ffn2"], p["norm_final"]]
    return pl.pallas_call(
        _mixer_out_kernel,
        out_shape=jax.ShapeDtypeStruct((t, D_MODEL), F32),
        grid=(t // tm,),
        in_specs=[
            pl.BlockSpec((tm, D_MODEL), lambda i: (i, 0)),
            pl.BlockSpec((tm, RET_V), lambda i: (i, 0)),
            pl.BlockSpec((tm, M_W), lambda i: (i, 0)),
            pl.BlockSpec((1, N_MOD, D_MODEL), lambda i: (mod_row(i * tm), 0, 0)),
        ] + [_const_spec(a.shape) for a in consts],
        out_specs=pl.BlockSpec((tm, D_MODEL), lambda i: (i, 0)),
        compiler_params=pltpu.CompilerParams(
            dimension_semantics=("arbitrary",), vmem_limit_bytes=VMEM_LIMIT),
        name="mixer_out",
    )(x, rgated, hmn, mod, *consts)


def _rope_tables(seq_len):
    rows = seq_len // GRID_W
    r = jnp.repeat(jnp.arange(rows, dtype=F32), GRID_W)
    col = (jnp.arange(seq_len) % GRID_W).astype(F32)
    n_f = DK_RET // 4
    freqs = ROPE_BASE ** (-jnp.arange(n_f, dtype=F32) / n_f)
    ang = jnp.concatenate([r[:, None] * freqs, col[:, None] * freqs], axis=-1)
    cos, sin = jnp.cos(ang), jnp.sin(ang)
    return jnp.concatenate([cos, cos], axis=-1), jnp.concatenate([-sin, sin], axis=-1)


def _layer_params(l, norm_ffn1, w1_ffn1, w3_ffn1, w2_ffn1, norm_mix, conv_w, conv_b,
                  ret_decay_logit, b_igate, b_fgate, ret_gn, m_gn, norm_ffn2, norm_final):
    gate_bias = jnp.pad(jnp.concatenate([b_igate[l], b_fgate[l]]), (0, GATE_LANES - 2 * N_PAIR))
    p = dict(
        norm_ffn1=norm_ffn1[l][None, :], norm_mix=norm_mix[l][None, :], norm_ffn2=norm_ffn2[l][None, :],
        norm_final=norm_final[None, :],
        w1_ffn1=w1_ffn1[l].astype(BF16), w3_ffn1=w3_ffn1[l].astype(BF16), w2_ffn1=w2_ffn1[l].astype(BF16),
        conv_w=conv_w[l], conv_b=conv_b[l][None, :],
        gate_bias=gate_bias[None, :],
        decay_rows=jnp.broadcast_to(ret_decay_logit[l].reshape(N_DIR * H_RET, 1), (N_DIR * H_RET, DV_RET)),
        ret_gn=ret_gn[l][None, :], m_gn=m_gn[l][None, :],
    )
    return p


def kernel(x_prompt, x_sample, c, state_ret, state_mlstm_C, state_mlstm_n, state_mlstm_m, c_ctx, w_ada, b_ada, norm_ffn1, w1_ffn1, w3_ffn1, w2_ffn1, norm_mix, w_in, conv_w, conv_b, ret_decay_logit, b_igate, b_fgate, ret_gn, m_gn, w_ret_up, w_m_up, w_out, norm_ffn2, w1_ffn2, w3_ffn2, w2_ffn2, norm_final):
    bp, lp, _ = x_prompt.shape
    bs, ls, _ = x_sample.shape
    depth = w_ada.shape[0]
    assert depth == 1 and lp % CHUNK == 0 and PROMPT_GROUP_ROWS % lp == 0 and ls <= MAX_GROUP_ROWS
    assert lp & (lp - 1) == 0 and ls & (ls - 1) == 0

    ctx_row = F32_SUBLANES * pl.cdiv(bs, F32_SUBLANES)
    n_cond = ctx_row + F32_SUBLANES
    cvec = jnp.zeros((n_cond, D_MODEL), F32).at[:bs].set(c).at[ctx_row].set(c_ctx)
    rope = _rope_tables(ls)

    xp = x_prompt.reshape(bp * lp, D_MODEL)
    xs = x_sample.reshape(bs * ls, D_MODEL)
    prompt_row = lambda r: ctx_row
    sample_row = lambda r: r // ls

    l = 0
    p = _layer_params(l, norm_ffn1, w1_ffn1, w3_ffn1, w2_ffn1, norm_mix, conv_w, conv_b,
                      ret_decay_logit, b_igate, b_fgate, ret_gn, m_gn, norm_ffn2, norm_final)
    mod = _ada_call(cvec, w_ada[l], b_ada[l][None, :]).reshape(n_cond, N_MOD, D_MODEL)

    n0 = state_mlstm_n[:, l].astype(F32).reshape(bs, N_PAIR, DH_M)
    m0 = jnp.broadcast_to(state_mlstm_m[:, l].astype(F32).reshape(bs, N_PAIR, 1), (bs, N_PAIR, CHUNK))
    init = (state_ret.astype(F32), state_mlstm_C.astype(F32), n0, m0)

    whole = lambda a: (a, False, ((0, a.shape[1]),))
    bg_start = D_IN - N_DIR * D_MODEL
    xp1, (p["w_in"], p["w_bg"]) = _ffn_call(
        xp, mod, prompt_row, p["norm_ffn1"], p["w1_ffn1"], p["w3_ffn1"], p["w2_ffn1"],
        casts=((w_in[l].T, True, ((0, D_IN), (bg_start, N_DIR * D_MODEL))),))
    xs1, (p["w1_ffn2"], p["w3_ffn2"], p["w2_ffn2"], p["w_ret_up"], p["w_m_up"], p["w_out"]) = _ffn_call(
        xs, mod, sample_row, p["norm_ffn1"], p["w1_ffn1"], p["w3_ffn1"], p["w2_ffn1"],
        casts=(whole(w1_ffn2[l]), whole(w3_ffn2[l]), whole(w2_ffn2[l]),
               whole(w_ret_up[l]), whole(w_m_up[l]), whole(w_out[l])))

    rg_p, hm_p, s_fin, c_fin, n_fin, m_fin = _mixer_core_call(xp1, mod, prompt_row, p, lp, PROMPT_GROUP_ROWS, None, None)
    rg_s, hm_s = _mixer_core_call(xs1, mod, sample_row, p, ls, ls, rope, init)

    yp = _mixer_out_call(xp1, rg_p, hm_p, mod, prompt_row, p)
    ys = _mixer_out_call(xs1, rg_s, hm_s, mod, sample_row, p)

    dt = x_prompt.dtype
    new_c = c_fin
    new_n = n_fin.reshape(bp, 1, N_DIR, H_M, DH_M)
    m_diag = jnp.stack([m_fin[:, d, d * H_M:(d + 1) * H_M, 0] for d in range(N_DIR)], axis=1)
    new_m = m_diag.reshape(bp, 1, N_DIR, H_M)
    return (yp.reshape(bp, lp, D_MODEL).astype(dt), ys.reshape(bs, ls, D_MODEL).astype(dt),
            s_fin.astype(dt), new_c.astype(dt), new_n.astype(dt), new_m.astype(dt))
```

```python
import functools

import jax
import jax.numpy as jnp
from jax import lax
from jax.experimental import pallas as pl
from jax.experimental.pallas import tpu as pltpu

F32 = jnp.float32
BF16 = jnp.bfloat16

D_MODEL = 1024
D_FF = 2816
CHUNK = 128
N_DIR = 2
H_RET = 4
DK_RET = 128
DV_RET = 256
H_M = 4
DH_M = 128
RET_QK = H_RET * DK_RET
RET_V = H_RET * DV_RET
M_W = H_M * DH_M
D_IN = 2 * RET_QK + 2 * RET_V + 4 * M_W + 2 * N_DIR * H_M + N_DIR * D_MODEL
GRID_W = 64
ROPE_BASE = 10000.0
EPS = 1e-6
GN_EPS = 1e-5
N_MOD = 9
N_PAIR = N_DIR * H_M
GATE_LANES = 128
LANES = 128
F32_SUBLANES = 8
BF16_SUBLANES = 16
NORM_ROWS = BF16_SUBLANES

MAX_GROUP_ROWS = 1024
PROMPT_GROUP_ROWS = 512
ROW_BLOCK = 256
ADA_TILE = 3 * D_MODEL
ADA_STREAMS = 4
SUB_ROWS = 256
FFN_TILE = 512
V7X_VMEM_BYTES = 64 * 1024 * 1024
VMEM_LIMIT = V7X_VMEM_BYTES - 4 * 1024 * 1024


def _sigmoid(x):
    return 1.0 / (1.0 + jnp.exp(-x))


def _silu(x):
    return x * _sigmoid(x)


def _log_sigmoid(x):
    return -(jnp.maximum(-x, 0.0) + jnp.log1p(jnp.exp(-jnp.abs(x))))


def _norm_mod(x, gain, shift, scale):
    ms = jnp.mean(x * x, axis=-1, keepdims=True)
    y = x * lax.rsqrt(ms + EPS) * gain
    return y * (1.0 + scale) + shift


def _dot(a, b):
    return jnp.dot(a, b, preferred_element_type=F32)


def _dot_nt(a, b):
    return lax.dot_general(a, b, (((1,), (1,)), ((), ())), preferred_element_type=F32)


def _head_norm(o, gain):
    mu = jnp.mean(o, axis=-1, keepdims=True)
    oc = o - mu
    var = jnp.mean(oc * oc, axis=-1, keepdims=True)
    return oc * lax.rsqrt(var + GN_EPS) * gain


def _ada_kernel(c_ref, *refs):
    w_refs, b_ref, o_ref = refs[:ADA_STREAMS], refs[ADA_STREAMS], refs[ADA_STREAMS + 1]
    s = _silu(c_ref[...]).astype(BF16)
    slab = D_MODEL // ADA_STREAMS
    acc = b_ref[...]
    for k, w_ref in enumerate(w_refs):
        acc = acc + _dot(s[:, k * slab:(k + 1) * slab], w_ref[...].astype(BF16))
    o_ref[...] = acc


def _ada_call(cvec, w_ada, b_ada):
    rows = cvec.shape[0]
    n_out = w_ada.shape[1]
    tile = ADA_TILE
    slab = D_MODEL // ADA_STREAMS
    return pl.pallas_call(
        _ada_kernel,
        out_shape=jax.ShapeDtypeStruct((rows, n_out), F32),
        grid=(n_out // tile,),
        in_specs=[pl.BlockSpec((rows, D_MODEL), lambda j: (0, 0))]
        + [pl.BlockSpec((slab, tile), functools.partial(lambda j, k: (k, j), k=k)) for k in range(ADA_STREAMS)]
        + [pl.BlockSpec((1, tile), lambda j: (0, j))],
        out_specs=pl.BlockSpec((rows, tile), lambda j: (0, j)),
        compiler_params=pltpu.CompilerParams(
            dimension_semantics=("arbitrary",), vmem_limit_bytes=VMEM_LIMIT),
        name="ada",
    )(cvec, *[w_ada] * ADA_STREAMS, b_ada)


def _row_blocks(n_rows):
    return [slice(r, r + SUB_ROWS) for r in range(0, n_rows, SUB_ROWS)]


def _swiglu_half_steps(xs, mod, k0, gain, w1_ref, w3_ref, w2_ref):
    shift, scale, gate = mod[k0:k0 + 1], mod[k0 + 1:k0 + 2], mod[k0 + 2:k0 + 3]

    def finish(x, h1, h3):
        h = (_silu(h1) * h3).astype(BF16)
        return x + 0.5 * gate * _dot(h, w2_ref[...])

    outs, pending = [], None
    for x in xs:
        u = _norm_mod(x, gain, shift, scale).astype(BF16)
        up = (x, _dot(u, w1_ref[...]), _dot(u, w3_ref[...]))
        if pending is not None:
            outs.append(finish(*pending))
        pending = up
    outs.append(finish(*pending))
    return outs


def _ffn_kernel(*refs, cast_plan):
    n_in = 6 + len(cast_plan)
    x_ref, mod_ref, g_ref, w1_ref, w3_ref, w2_ref = refs[:6]
    o_ref = refs[n_in]
    blocks = _row_blocks(x_ref.shape[0])
    ys = _swiglu_half_steps([x_ref[rows, :] for rows in blocks], mod_ref[0], 0, g_ref[...],
                            w1_ref, w3_ref, w2_ref)
    for rows, y in zip(blocks, ys):
        o_ref[rows, :] = y
    cast_out = iter(refs[n_in + 1:])
    for src_ref, (transposed, ranges) in zip(refs[6:n_in], cast_plan):
        block = src_ref[...].T if transposed else src_ref[...]
        for start, width in ranges:
            next(cast_out)[...] = block[:, start:start + width].astype(BF16)


def _const_spec(shape):
    nd = len(shape)
    return pl.BlockSpec(shape, lambda i: (0,) * nd)


def _ffn_call(x, mod, mod_row, gain, w1, w3, w2, casts=()):
    t = x.shape[0]
    tm = FFN_TILE
    steps = t // tm
    in_specs = [
        pl.BlockSpec((tm, D_MODEL), lambda i: (i, 0)),
        pl.BlockSpec((1, N_MOD, D_MODEL), lambda i: (mod_row(i * tm), 0, 0)),
        _const_spec((1, D_MODEL)),
        _const_spec((D_MODEL, D_FF)),
        _const_spec((D_MODEL, D_FF)),
        _const_spec((D_FF, D_MODEL)),
    ]
    out_shape = [jax.ShapeDtypeStruct((t, D_MODEL), F32)]
    out_specs = [pl.BlockSpec((tm, D_MODEL), lambda i: (i, 0))]
    for src, transposed, ranges in casts:
        rows, cols = src.shape[::-1] if transposed else src.shape
        if transposed:
            assert rows % (steps * LANES) == 0
            in_specs.append(pl.BlockSpec((cols, rows // steps), lambda i: (0, i)))
        else:
            assert rows % (steps * BF16_SUBLANES) == 0
            in_specs.append(pl.BlockSpec((rows // steps, cols), lambda i: (i, 0)))
        for _, width in ranges:
            out_shape.append(jax.ShapeDtypeStruct((rows, width), BF16))
            out_specs.append(pl.BlockSpec((rows // steps, width), lambda i: (i, 0)))
    outs = pl.pallas_call(
        functools.partial(_ffn_kernel, cast_plan=tuple((tr, r) for _, tr, r in casts)),
        out_shape=out_shape,
        grid=(steps,),
        in_specs=in_specs,
        out_specs=out_specs,
        compiler_params=pltpu.CompilerParams(
            dimension_semantics=("arbitrary",), vmem_limit_bytes=VMEM_LIMIT),
        name="ffn",
    )(x, mod, gain, w1, w3, w2, *[src for src, _, _ in casts])
    return outs[0], outs[1:]


def _mixer_core_kernel(*refs, group_rows, seq_len, use_rope, has_init, emit_state):
    it = iter(refs)
    x_ref, mod_ref, nmix_ref = next(it), next(it), next(it)
    wrqk_ref, wrv_ref, wrg_ref = next(it), next(it), next(it)
    wmqk_ref, wmv_ref, wmo_ref, wgt_ref = next(it), next(it), next(it), next(it)
    convw_ref, convb_ref, gbias_ref, decay_ref = next(it), next(it), next(it), next(it)
    rgn_ref, mgn_ref = next(it), next(it)
    if use_rope:
        cos_ref, sin_ref = next(it), next(it)
    if has_init:
        s0_ref, c0_ref, n0_ref, m0_ref = next(it), next(it), next(it), next(it)
    rg_out, hm_out = next(it), next(it)
    if emit_state:
        sfin_ref, cfin_ref, nfin_ref, mfin_ref = next(it), next(it), next(it), next(it)
    (u_s, q_s, k_s, v_s, t_s, qt_s, kv_s, wide_s, gate_s, pre_s, igb_s, zt_s, wt_s, dec_s, gch_s, stat_s,
     mprev_s) = (next(it) for _ in range(17))

    rows_total = group_rows
    n_chunks = rows_total // CHUNK
    cps = seq_len // CHUNK
    n_seq = rows_total // seq_len
    n_rb = rows_total // ROW_BLOCK
    mod = mod_ref[0]

    def rb_rows(rb):
        return slice(rb * ROW_BLOCK, (rb + 1) * ROW_BLOCK)

    def chunk_rows(n):
        return slice(n * CHUNK, (n + 1) * CHUNK)

    def head_cols(h, width):
        return slice(h * width, (h + 1) * width)

    @pl.when(pl.program_id(0) == 0)
    def _():
        pos_i = lax.broadcasted_iota(jnp.int32, (CHUNK, CHUNK), 0).astype(F32)
        pos_j = lax.broadcasted_iota(jnp.int32, (CHUNK, CHUNK), 1).astype(F32)
        lg_rows = _log_sigmoid(decay_ref[...])
        for h in range(H_RET):
            lgf = lg_rows[h:h + 1, :CHUNK]
            lgb = lg_rows[H_RET + h:H_RET + h + 1, :CHUNK]
            dec_s[h, 0] = jnp.where(pos_i > pos_j, jnp.exp((pos_i - pos_j) * lgf),
                                    jnp.where(pos_i < pos_j, jnp.exp((pos_j - pos_i) * lgb), 2.0))
            dec_s[h, 1] = jnp.exp((CHUNK - 1.0 - pos_j) * lgf)
            dec_s[h, 2] = jnp.exp(pos_j * lgb)
            dec_s[h, 3] = jnp.exp((pos_i + 1.0) * lgf)
            dec_s[h, 4] = jnp.exp((CHUNK - pos_i) * lgb)
        gch_s[...] = jnp.exp(CHUNK * lg_rows)

    g_chunk = gch_s[...]

    chunks_per_rb = ROW_BLOCK // CHUNK
    for rb in range(n_rb):
        rows = rb_rows(rb)
        u_s[rows, :] = _norm_mod(x_ref[rows, :], nmix_ref[...], mod[3:4], mod[4:5]).astype(BF16)
        ub = u_s[rows, :]
        qk = _dot(ub, wrqk_ref[...])
        q = qk[:, :RET_QK]
        k = qk[:, RET_QK:] * (DK_RET ** -0.5)
        for h in range(H_RET):
            cols = head_cols(h, DK_RET)
            qh, kh = q[:, cols], k[:, cols]
            if use_rope:
                cs, sn = cos_ref[rows, :], sin_ref[rows, :]
                qh = qh * cs + pltpu.roll(qh, DK_RET // 2, 1) * sn
                kh = kh * cs + pltpu.roll(kh, DK_RET // 2, 1) * sn
            q_s[rows, cols] = qh
            for ch in range(chunks_per_rb):
                t_s[rb * chunks_per_rb + ch, h] = kh[ch * CHUNK:(ch + 1) * CHUNK, :].T
        v_s[rows, :] = _dot(ub, wrv_ref[...]).astype(BF16)

    for n in range(n_chunks):
        for h in range(H_RET):
            k_t = t_s[n, h]
            kcat = jnp.concatenate([k_t * dec_s[h, 1], k_t * dec_s[h, 2]], axis=0).astype(BF16)
            kv = _dot(kcat, v_s[chunk_rows(n), head_cols(h, DV_RET)])
            kv_s[n, 2 * h] = kv[:DK_RET]
            kv_s[n, 2 * h + 1] = kv[DK_RET:]

    for rb in range(n_rb):
        gate_s[rb_rows(rb), :] = _dot(u_s[rb_rows(rb), :], wrg_ref[...])

    for g in range(n_seq):
        for h in range(H_RET):
            for d in range(N_DIR):
                slot = 2 * h + d
                order = list(range(g * cps, (g + 1) * cps))
                if d == 1:
                    order.reverse()
                for cols in (slice(c, c + LANES) for c in range(0, DV_RET, LANES)):
                    decay = g_chunk[d * H_RET + h:d * H_RET + h + 1, cols]
                    state = s0_ref[0, 0, d, h, :, cols] if has_init else jnp.zeros((DK_RET, LANES), F32)
                    for n in order:
                        local = kv_s[n, slot, :, cols]
                        kv_s[n, slot, :, cols] = state
                        state = decay * state + local
                    if emit_state:
                        sfin_ref[g, 0, d, h, :, cols] = state

    def ret_scores(n):
        return [_dot(q_s[chunk_rows(n), head_cols(h, DK_RET)].astype(BF16), t_s[n, h].astype(BF16))
                for h in range(H_RET)]

    def ret_outputs(n, all_scores):
        rows = chunk_rows(n)
        for h in range(H_RET):
            qh = q_s[rows, head_cols(h, DK_RET)]
            vh = v_s[rows, head_cols(h, DV_RET)]
            sm = (all_scores[h] * dec_s[h, 0]).astype(BF16)
            qcat = jnp.concatenate([qh * dec_s[h, 3], qh * dec_s[h, 4]], axis=1).astype(BF16)
            scat = jnp.concatenate([kv_s[n, 2 * h], kv_s[n, 2 * h + 1]], axis=0).astype(BF16)
            o = _dot(sm, vh) + _dot(qcat, scat)
            wide_s[rows, head_cols(h, DV_RET)] = _head_norm(o, rgn_ref[:, head_cols(h, DV_RET)])

    for rb in range(n_rb):
        rows = rb_rows(rb)
        chunks = range(rb * chunks_per_rb, (rb + 1) * chunks_per_rb)
        scores = [ret_scores(n) for n in chunks]
        ub = u_s[rows, :]
        glane = lax.broadcasted_iota(jnp.int32, (ROW_BLOCK, GATE_LANES), 1)
        gt = jnp.where(glane < 2 * N_PAIR, _dot(ub, wgt_ref[...]) + gbias_ref[...], 0.0)
        igb_s[rows, :] = jnp.where(glane < N_PAIR, gt, _log_sigmoid(gt))
        pre_s[pl.ds(F32_SUBLANES + rb * ROW_BLOCK, ROW_BLOCK), :] = _dot(ub, wmqk_ref[...])
        for n, sc in zip(chunks, scores):
            ret_outputs(n, sc)

    for rb in range(n_rb):
        rows = rb_rows(rb)
        rg_out[rows, :] = (wide_s[rows, :] * _silu(gate_s[rows, :])).astype(BF16)

    for rb in range(n_rb):
        mv = _dot(u_s[rb_rows(rb), :], wmv_ref[...])
        for ch in range(chunks_per_rb):
            for h in range(H_M):
                t_s[rb * chunks_per_rb + ch, h] = mv[ch * CHUNK:(ch + 1) * CHUNK, head_cols(h, DH_M)].T

    pad_rows = jnp.zeros((F32_SUBLANES, 2 * M_W), F32)
    pre_s[0:F32_SUBLANES, :] = pad_rows
    pre_s[F32_SUBLANES + rows_total:, :] = pad_rows
    row_id = lax.broadcasted_iota(jnp.int32, (rows_total, CHUNK), 0)
    seq_pos = row_id & (seq_len - 1)
    for cb in range(2 * M_W // CHUNK):
        cols = slice(cb * CHUNK, (cb + 1) * CHUNK)
        xc = pre_s[F32_SUBLANES:F32_SUBLANES + rows_total, cols]
        prev = jnp.where(seq_pos == 0, 0.0, pre_s[F32_SUBLANES - 1:F32_SUBLANES - 1 + rows_total, cols])
        nxt = jnp.where(seq_pos == seq_len - 1, 0.0, pre_s[F32_SUBLANES + 1:F32_SUBLANES + 1 + rows_total, cols])
        y = convb_ref[:, cols] + convw_ref[0:1, cols] * prev
        y = y + convw_ref[1:2, cols] * xc
        y = y + convw_ref[2:3, cols] * nxt
        y = _silu(y)
        if cb < H_M:
            for n in range(n_chunks):
                qt_s[n, cb] = y[n * CHUNK:(n + 1) * CHUNK, :].T.astype(BF16)
        else:
            k_s[:, head_cols(cb - H_M, DH_M)] = y * (DH_M ** -0.5)

    z = igb_s[...]
    cpos = row_id & (CHUNK - 1)
    glane = lax.broadcasted_iota(jnp.int32, (rows_total, GATE_LANES), 1)
    pre, suf = z, z
    step = 1
    while step < CHUNK:
        pre = pre + jnp.where(cpos >= step, pltpu.roll(pre, step, 0), 0.0)
        suf = suf + jnp.where(cpos < CHUNK - step, pltpu.roll(suf, rows_total - step, 0), 0.0)
        step *= 2
    cum = jnp.where(glane < N_PAIR + H_M, pre, suf)
    i_minus_b = z - pltpu.roll(cum, GATE_LANES - N_PAIR, 1)
    igb = jnp.where(glane < N_PAIR, z,
                    jnp.where(glane < 2 * N_PAIR, cum, pltpu.roll(i_minus_b, 2 * N_PAIR, 1)))
    igb_s[...] = igb

    for n in range(n_chunks):
        zt_s[n] = igb[n * CHUNK:(n + 1) * CHUNK, :].T[0:3 * N_PAIR, :]
    ig_all = zt_s[:, 0:N_PAIR, :]
    b_all = zt_s[:, N_PAIR:2 * N_PAIR, :]
    pair = lax.broadcasted_iota(jnp.int32, (n_chunks, N_PAIR, 1), 1)
    b_last = jnp.where(pair < H_M, b_all[:, :, CHUNK - 1:CHUNK], b_all[:, :, 0:1])
    g_all = (b_last - b_all) + ig_all
    m_loc = jnp.max(g_all, axis=-1, keepdims=True)
    wt_s[...] = jnp.exp(g_all - m_loc)
    stat_s[:, 0] = jnp.broadcast_to(b_last, (n_chunks, N_PAIR, CHUNK))
    stat_s[:, 1] = jnp.broadcast_to(m_loc, (n_chunks, N_PAIR, CHUNK))

    for n in range(n_chunks):
        rows = chunk_rows(n)
        w_t = wt_s[n]
        for h in range(H_M):
            kh = k_s[rows, head_cols(h, DH_M)].astype(BF16)
            v_t = t_s[n, h]
            parts = []
            for d in range(N_DIR):
                w_row = w_t[d * H_M + h:d * H_M + h + 1, :]
                parts += [v_t * w_row, jnp.broadcast_to(w_row, (NORM_ROWS, CHUNK))]
            local = _dot(jnp.concatenate(parts, axis=0).astype(BF16), kh)
            for d in range(N_DIR):
                c = d * H_M + h
                base = d * (DH_M + NORM_ROWS)
                kv_s[n, c, :, :DH_M] = local[base:base + DH_M]
                kv_s[n, c, 0:NORM_ROWS, DH_M:] = local[base + DH_M:base + DH_M + NORM_ROWS]

    for rb in range(n_rb):
        gate_s[rb_rows(rb), :M_W] = _dot(u_s[rb_rows(rb), :], wmo_ref[...])

    for g in range(n_seq):
        for d in range(N_DIR):
            order = list(range(g * cps, (g + 1) * cps))
            if d == 1:
                order.reverse()
            m_prev = m0_ref[0] if has_init else jnp.zeros((N_PAIR, CHUNK), F32)
            a1s, a2s = [], []
            for n in order:
                b_last_n, m_loc_n = stat_s[n, 0], stat_s[n, 1]
                m_new = jnp.maximum(b_last_n + m_prev, m_loc_n)
                a1s.append(jnp.exp(b_last_n + m_prev - m_new))
                a2s.append(jnp.exp(m_loc_n - m_new))
                mprev_s[n, d] = m_prev
                m_prev = m_new
            if emit_state:
                mfin_ref[g, d] = m_prev
            for h in range(H_M):
                c = d * H_M + h
                if has_init:
                    mem = c0_ref[0, 0, d, h]
                    nrm = jnp.broadcast_to(n0_ref[0][c:c + 1, :], (NORM_ROWS, DH_M))
                else:
                    mem = jnp.zeros((DH_M, DH_M), F32)
                    nrm = jnp.zeros((NORM_ROWS, DH_M), F32)
                for idx, n in enumerate(order):
                    a1 = a1s[idx][c:c + 1, :]
                    a2 = a2s[idx][c:c + 1, :]
                    local_mem = kv_s[n, c, :, :DH_M]
                    local_nrm = kv_s[n, c, 0:NORM_ROWS, DH_M:]
                    kv_s[n, c, :, :DH_M] = mem
                    kv_s[n, c, 0:NORM_ROWS, DH_M:] = nrm
                    mem = a1 * mem + a2 * local_mem
                    nrm = a1 * nrm + a2 * local_nrm
                if emit_state:
                    cfin_ref[g, 0, d, h] = mem
                    nfin_ref[g, c:c + 1, :] = nrm[0:1]

    tri_j = lax.broadcasted_iota(jnp.int32, (CHUNK, CHUNK), 0)
    tri_i = lax.broadcasted_iota(jnp.int32, (CHUNK, CHUNK), 1)
    mem_rows = DH_M + NORM_ROWS

    for n in range(n_chunks):
        rows = chunk_rows(n)
        zc = igb_s[rows, :]
        zt = zt_s[n]
        bigs = []
        for h in range(H_M):
            kh = k_s[rows, head_cols(h, DH_M)].astype(BF16)
            mems = [jnp.concatenate([kv_s[n, d * H_M + h, :, :DH_M], kv_s[n, d * H_M + h, 0:NORM_ROWS, DH_M:]],
                                    axis=0).astype(BF16) for d in range(N_DIR)]
            bigs.append(_dot(jnp.concatenate([kh] + mems, axis=0), qt_s[n, h]))
        for h in range(H_M):
            big = bigs[h]
            scores_t = big[:CHUNK]
            weights, crosses, w_ints, m_is = [], [], [], []
            for d in range(N_DIR):
                c = d * H_M + h
                a_col = zc[:, 2 * N_PAIR + c:2 * N_PAIR + c + 1]
                b_row = zt[N_PAIR + c:N_PAIR + c + 1, :]
                mask = (tri_j <= tri_i) if d == 0 else (tri_j >= tri_i)
                dlog = jnp.where(mask, b_row + a_col, -jnp.inf)
                lin = b_row + mprev_s[n, d][c:c + 1, :]
                m_i = jnp.maximum(jnp.max(dlog, axis=0, keepdims=True), lin)
                weights.append(scores_t * jnp.exp(dlog - m_i))
                crosses.append(big[CHUNK + d * mem_rows:CHUNK + (d + 1) * mem_rows])
                w_ints.append(jnp.exp(lin - m_i))
                m_is.append(m_i)
            intra = _dot(t_s[n, h].astype(BF16), jnp.concatenate(weights, axis=1).astype(BF16))
            h_sum_t = jnp.zeros((DH_M, CHUNK), F32)
            for d in range(N_DIR):
                cross, w_int = crosses[d], w_ints[d]
                num = intra[:, d * CHUNK:(d + 1) * CHUNK] + cross[:DH_M] * w_int
                den = jnp.sum(weights[d], axis=0, keepdims=True) + cross[DH_M:DH_M + 1] * w_int
                denom = jnp.maximum(jnp.abs(den), jnp.exp(-m_is[d]))
                h_sum_t = h_sum_t + num * (1.0 / denom)
            wide_s[rows, head_cols(h, DH_M)] = h_sum_t.T

    for rb in range(n_rb):
        rows = rb_rows(rb)
        hm = _sigmoid(gate_s[rows, :M_W]) * wide_s[rows, :M_W]
        for h in range(H_M):
            cols = head_cols(h, DH_M)
            hm_out[rows, cols] = _head_norm(hm[:, cols], mgn_ref[:, cols]).astype(BF16)


def _mixer_core_call(x, mod, mod_row, p, seq_len, group_rows, rope, init):
    t = x.shape[0]
    n_groups = t // group_rows
    n_chunks = group_rows // CHUNK
    seq_per_group = group_rows // seq_len
    use_rope = rope is not None
    has_init = init is not None
    emit_state = not has_init

    w_in = p["w_in"]
    groups = []
    start = 0
    for width in (2 * RET_QK, RET_V, RET_V, 2 * M_W, M_W, M_W, GATE_LANES):
        assert start % width == 0
        groups.append(pl.BlockSpec((D_MODEL, width), functools.partial(lambda i, c: (0, c), c=start // width)))
        start += width
    small = [p["conv_w"], p["conv_b"], p["gate_bias"], p["decay_rows"], p["ret_gn"], p["m_gn"]]
    inputs = [x, mod, p["norm_mix"]] + [w_in] * len(groups) + small
    in_specs = [
        pl.BlockSpec((group_rows, D_MODEL), lambda i: (i, 0)),
        pl.BlockSpec((1, N_MOD, D_MODEL), lambda i: (mod_row(i * group_rows), 0, 0)),
        _const_spec(p["norm_mix"].shape),
    ] + groups + [_const_spec(a.shape) for a in small]
    if use_rope:
        inputs += [rope[0], rope[1]]
        in_specs += [_const_spec(rope[0].shape), _const_spec(rope[1].shape)]
    if has_init:
        s0, c0, n0, m0 = init
        inputs += [s0, c0, n0, m0]
        in_specs += [
            pl.BlockSpec((1, 1, N_DIR, H_RET, DK_RET, DV_RET), lambda i: (i, 0, 0, 0, 0, 0)),
            pl.BlockSpec((1, 1, N_DIR, H_M, DH_M, DH_M), lambda i: (i, 0, 0, 0, 0, 0)),
            pl.BlockSpec((1, N_PAIR, DH_M), lambda i: (i, 0, 0)),
            pl.BlockSpec((1, N_PAIR, CHUNK), lambda i: (i, 0, 0)),
        ]

    out_shape = [jax.ShapeDtypeStruct((t, RET_V), BF16), jax.ShapeDtypeStruct((t, M_W), BF16)]
    out_specs = [pl.BlockSpec((group_rows, RET_V), lambda i: (i, 0)),
                 pl.BlockSpec((group_rows, M_W), lambda i: (i, 0))]
    if emit_state:
        n_seq = t // seq_len
        out_shape += [
            jax.ShapeDtypeStruct((n_seq, 1, N_DIR, H_RET, DK_RET, DV_RET), F32),
            jax.ShapeDtypeStruct((n_seq, 1, N_DIR, H_M, DH_M, DH_M), F32),
            jax.ShapeDtypeStruct((n_seq, N_PAIR, DH_M), F32),
            jax.ShapeDtypeStruct((n_seq, N_DIR, N_PAIR, CHUNK), F32),
        ]
        out_specs += [
            pl.BlockSpec((seq_per_group, 1, N_DIR, H_RET, DK_RET, DV_RET), lambda i: (i, 0, 0, 0, 0, 0)),
            pl.BlockSpec((seq_per_group, 1, N_DIR, H_M, DH_M, DH_M), lambda i: (i, 0, 0, 0, 0, 0)),
            pl.BlockSpec((seq_per_group, N_PAIR, DH_M), lambda i: (i, 0, 0)),
            pl.BlockSpec((seq_per_group, N_DIR, N_PAIR, CHUNK), lambda i: (i, 0, 0, 0)),
        ]

    scratch = [
        pltpu.VMEM((group_rows, D_MODEL), BF16),
        pltpu.VMEM((group_rows, RET_QK), F32),
        pltpu.VMEM((group_rows, RET_QK), F32),
        pltpu.VMEM((group_rows, RET_V), BF16),
        pltpu.VMEM((n_chunks, H_M, DH_M, CHUNK), F32),
        pltpu.VMEM((n_chunks, H_M, DH_M, CHUNK), BF16),
        pltpu.VMEM((n_chunks, N_PAIR, DK_RET, DV_RET), F32),
        pltpu.VMEM((group_rows, RET_V), F32),
        pltpu.VMEM((group_rows, RET_V), F32),
        pltpu.VMEM((group_rows + 2 * F32_SUBLANES, 2 * M_W), F32),
        pltpu.VMEM((group_rows, GATE_LANES), F32),
        pltpu.VMEM((n_chunks, 3 * N_PAIR, CHUNK), F32),
        pltpu.VMEM((n_chunks, N_PAIR, CHUNK), F32),
        pltpu.VMEM((H_RET, 5, CHUNK, CHUNK), F32),
        pltpu.VMEM((N_DIR * H_RET, DV_RET), F32),
        pltpu.VMEM((n_chunks, 2, N_PAIR, CHUNK), F32),
        pltpu.VMEM((n_chunks, N_DIR, N_PAIR, CHUNK), F32),
    ]
    kern = functools.partial(_mixer_core_kernel, group_rows=group_rows, seq_len=seq_len, use_rope=use_rope,
                             has_init=has_init, emit_state=emit_state)
    return pl.pallas_call(
        kern,
        out_shape=out_shape,
        grid=(n_groups,),
        in_specs=in_specs,
        out_specs=out_specs,
        scratch_shapes=scratch,
        compiler_params=pltpu.CompilerParams(
            dimension_semantics=("arbitrary",), vmem_limit_bytes=VMEM_LIMIT),
        name="mixer_core",
    )(*inputs)


def _mixer_out_kernel(x_ref, rg_ref, hm_ref, mod_ref, nmix_ref, wbg_ref, wru_ref, wmu_ref, wout_ref,
                      nffn_ref, w1_ref, w3_ref, w2_ref, nfin_ref, o_ref):
    mod = mod_ref[0]
    blocks = _row_blocks(x_ref.shape[0])

    def merge(x, bg, ret_branch, m_branch):
        gates = _sigmoid(bg)
        merged = (gates[:, :D_MODEL] * ret_branch + gates[:, D_MODEL:] * m_branch).astype(BF16)
        return x + mod[5:6] * _dot(merged, wout_ref[...])

    xs, pending = [], None
    for rows in blocks:
        x = x_ref[rows, :]
        u = _norm_mod(x, nmix_ref[...], mod[3:4], mod[4:5]).astype(BF16)
        branches = (x, _dot(u, wbg_ref[...]), _dot(rg_ref[rows, :], wru_ref[...]),
                    _dot(hm_ref[rows, :], wmu_ref[...]))
        if pending is not None:
            xs.append(merge(*pending))
        pending = branches
    xs.append(merge(*pending))

    xs = _swiglu_half_steps(xs, mod, 6, nffn_ref[...], w1_ref, w3_ref, w2_ref)
    for rows, x in zip(blocks, xs):
        ms = jnp.mean(x * x, axis=-1, keepdims=True)
        o_ref[rows, :] = x * lax.rsqrt(ms + EPS) * nfin_ref[...]


def _mixer_out_call(x, rgated, hmn, mod, mod_row, p):
    t = x.shape[0]
    tm = FFN_TILE
    consts = [p["norm_mix"], p["w_bg"], p["w_ret_up"], p["w_m_up"], p["w_out"],
              p["norm_ffn2"], p["w1_ffn2"], p["w3_ffn2"], p["w2_ffn2"], p["norm_final"]]
    return pl.pallas_call(
        _mixer_out_kernel,
        out_shape=jax.ShapeDtypeStruct((t, D_MODEL), F32),
        grid=(t // tm,),
        in_specs=[
            pl.BlockSpec((tm, D_MODEL), lambda i: (i, 0)),
            pl.BlockSpec((tm, RET_V), lambda i: (i, 0)),
            pl.BlockSpec((tm, M_W), lambda i: (i, 0)),
            pl.BlockSpec((1, N_MOD, D_MODEL), lambda i: (mod_row(i * tm), 0, 0)),
        ] + [_const_spec(a.shape) for a in consts],
        out_specs=pl.BlockSpec((tm, D_MODEL), lambda i: (i, 0)),
        compiler_params=pltpu.CompilerParams(
            dimension_semantics=("arbitrary",), vmem_limit_bytes=VMEM_LIMIT),
        name="mixer_out",
    )(x, rgated, hmn, mod, *consts)


def _rope_tables(seq_len):
    rows = seq_len // GRID_W
    r = jnp.repeat(jnp.arange(rows, dtype=F32), GRID_W)
    col = (jnp.arange(seq_len) % GRID_W).astype(F32)
    n_f = DK_RET // 4
    freqs = ROPE_BASE ** (-jnp.arange(n_f, dtype=F32) / n_f)
    ang = jnp.concatenate([r[:, None] * freqs, col[:, None] * freqs], axis=-1)
    cos, sin = jnp.cos(ang), jnp.sin(ang)
    return jnp.concatenate([cos, cos], axis=-1), jnp.concatenate([-sin, sin], axis=-1)


def _layer_params(l, norm_ffn1, w1_ffn1, w3_ffn1, w2_ffn1, norm_mix, conv_w, conv_b,
                  ret_decay_logit, b_igate, b_fgate, ret_gn, m_gn, norm_ffn2, norm_final):
    gate_bias = jnp.pad(jnp.concatenate([b_igate[l], b_fgate[l]]), (0, GATE_LANES - 2 * N_PAIR))
    p = dict(
        norm_ffn1=norm_ffn1[l][None, :], norm_mix=norm_mix[l][None, :], norm_ffn2=norm_ffn2[l][None, :],
        norm_final=norm_final[None, :],
        w1_ffn1=w1_ffn1[l].astype(BF16), w3_ffn1=w3_ffn1[l].astype(BF16), w2_ffn1=w2_ffn1[l].astype(BF16),
        conv_w=conv_w[l], conv_b=conv_b[l][None, :],
        gate_bias=gate_bias[None, :],
        decay_rows=jnp.broadcast_to(ret_decay_logit[l].reshape(N_DIR * H_RET, 1), (N_DIR * H_RET, DV_RET)),
        ret_gn=ret_gn[l][None, :], m_gn=m_gn[l][None, :],
    )
    return p


def kernel(x_prompt, x_sample, c, state_ret, state_mlstm_C, state_mlstm_n, state_mlstm_m, c_ctx, w_ada, b_ada, norm_ffn1, w1_ffn1, w3_ffn1, w2_ffn1, norm_mix, w_in, conv_w, conv_b, ret_decay_logit, b_igate, b_fgate, ret_gn, m_gn, w_ret_up, w_m_up, w_out, norm_ffn2, w1_ffn2, w3_ffn2, w2_ffn2, norm_final):
    bp, lp, _ = x_prompt.shape
    bs, ls, _ = x_sample.shape
    depth = w_ada.shape[0]
    assert depth == 1 and lp % CHUNK == 0 and PROMPT_GROUP_ROWS % lp == 0 and ls <= MAX_GROUP_ROWS
    assert lp & (lp - 1) == 0 and ls & (ls - 1) == 0

    ctx_row = F32_SUBLANES * pl.cdiv(bs, F32_SUBLANES)
    n_cond = ctx_row + F32_SUBLANES
    cvec = jnp.zeros((n_cond, D_MODEL), F32).at[:bs].set(c).at[ctx_row].set(c_ctx)
    rope = _rope_tables(ls)

    xp = x_prompt.reshape(bp * lp, D_MODEL)
    xs = x_sample.reshape(bs * ls, D_MODEL)
    prompt_row = lambda r: ctx_row
    sample_row = lambda r: r // ls

    l = 0
    p = _layer_params(l, norm_ffn1, w1_ffn1, w3_ffn1, w2_ffn1, norm_mix, conv_w, conv_b,
                      ret_decay_logit, b_igate, b_fgate, ret_gn, m_gn, norm_ffn2, norm_final)
    mod = _ada_call(cvec, w_ada[l], b_ada[l][None, :]).reshape(n_cond, N_MOD, D_MODEL)

    n0 = state_mlstm_n[:, l].astype(F32).reshape(bs, N_PAIR, DH_M)
    m0 = jnp.broadcast_to(state_mlstm_m[:, l].astype(F32).reshape(bs, N_PAIR, 1), (bs, N_PAIR, CHUNK))
    init = (state_ret.astype(F32), state_mlstm_C.astype(F32), n0, m0)

    whole = lambda a: (a, False, ((0, a.shape[1]),))
    bg_start = D_IN - N_DIR * D_MODEL
    xp1, (p["w_in"], p["w_bg"]) = _ffn_call(
        xp, mod, prompt_row, p["norm_ffn1"], p["w1_ffn1"], p["w3_ffn1"], p["w2_ffn1"],
        casts=((w_in[l].T, True, ((0, D_IN), (bg_start, N_DIR * D_MODEL))),))
    xs1, (p["w1_ffn2"], p["w3_ffn2"], p["w2_ffn2"], p["w_ret_up"], p["w_m_up"], p["w_out"]) = _ffn_call(
        xs, mod, sample_row, p["norm_ffn1"], p["w1_ffn1"], p["w3_ffn1"], p["w2_ffn1"],
        casts=(whole(w1_ffn2[l]), whole(w3_ffn2[l]), whole(w2_ffn2[l]),
               whole(w_ret_up[l]), whole(w_m_up[l]), whole(w_out[l])))

    rg_p, hm_p, s_fin, c_fin, n_fin, m_fin = _mixer_core_call(xp1, mod, prompt_row, p, lp, PROMPT_GROUP_ROWS, None, None)
    rg_s, hm_s = _mixer_core_call(xs1, mod, sample_row, p, ls, ls, rope, init)

    yp = _mixer_out_call(xp1, rg_p, hm_p, mod, prompt_row, p)
    ys = _mixer_out_call(xs1, rg_s, hm_s, mod, sample_row, p)

    dt = x_prompt.dtype
    new_c = c_fin
    new_n = n_fin.reshape(bp, 1, N_DIR, H_M, DH_M)
    m_diag = jnp.stack([m_fin[:, d, d * H_M:(d + 1) * H_M, 0] for d in range(N_DIR)], axis=1)
    new_m = m_diag.reshape(bp, 1, N_DIR, H_M)
    return (yp.reshape(bp, lp, D_MODEL).astype(dt), ys.reshape(bs, ls, D_MODEL).astype(dt),
            s_fin.astype(dt), new_c.astype(dt), new_n.astype(dt), new_m.astype(dt))
```
